```python
import math
import jax, jax.numpy as jnp
from jax import lax
import numpy as np

D_MODEL = 1024
BATCH = 8
SEQ = 2048
DEPTH = 2
DEC_BATCH = 128
DEC_SEQ = 1
PAST_LEN = 16384
PAGE_SIZE = 128

RET_HEADS = 4
RET_DK = 128
RET_DV = 256
ROPE_BASE = 10000.0
HG_HEADS = 8
HG_EXPAND = 128
HG_DV = 128
SSM_HEADS = 16
SSM_HEADDIM = 64
SSM_GROUPS = 4
SSM_STATE = 128
SSM_CONV = 4
SSM_INNER = SSM_HEADS * SSM_HEADDIM
SSM_CONV_CH = SSM_INNER + 2 * SSM_GROUPS * SSM_STATE
RET_Q = RET_HEADS * RET_DK
RET_V = RET_HEADS * RET_DV
HG_K = HG_HEADS * HG_EXPAND
HG_V = HG_HEADS * HG_DV
BRANCH_W = 1024
N_BRANCH = 3
D_FF = 4 * D_MODEL
CHUNK = 64
EPS = 1e-6
SPLITS = (RET_Q, RET_Q, RET_V, RET_V, HG_K, HG_K, HG_V, HG_V, SSM_INNER, SSM_CONV_CH, SSM_HEADS, N_BRANCH * D_MODEL)
D_IN_PROJ = sum(SPLITS)

kernel_name = "hybrid_retention_hgrn2_ssd_step"


def rmsnorm(x, gain):
    xf = x.astype(jnp.float32)
    y = xf * lax.rsqrt(jnp.mean(xf * xf, axis=-1, keepdims=True) + EPS)
    return (y * gain.astype(jnp.float32)).astype(x.dtype)


def head_layernorm(o, gain):
    mu = jnp.mean(o, axis=-1, keepdims=True)
    oc = o - mu
    return oc * lax.rsqrt(jnp.mean(oc * oc, axis=-1, keepdims=True) + EPS) * gain.astype(jnp.float32)


def head_rmsnorm(o, gain):
    return o * lax.rsqrt(jnp.mean(o * o, axis=-1, keepdims=True) + EPS) * gain.astype(jnp.float32)


def rope(t, pos):
    half = t.shape[-1] // 2
    inv = ROPE_BASE ** (-jnp.arange(half, dtype=jnp.float32) / half)
    ang = pos.astype(jnp.float32)[:, None] * inv[None, :]
    cos = jnp.cos(ang)[None, :, None, :]
    sin = jnp.sin(ang)[None, :, None, :]
    tf = t.astype(jnp.float32)
    t1, t2 = tf[..., :half], tf[..., half:]
    return jnp.concatenate([t1 * cos - t2 * sin, t1 * sin + t2 * cos], axis=-1)


def chunk_len(L):
    return CHUNK if L % CHUNK == 0 else math.gcd(L, CHUNK)


def chunked_gla(q, k, v, g, s0):
    B, L, H, K = q.shape
    V = v.shape[-1]
    c = chunk_len(L)
    n = L // c
    per_channel = g.shape[-1] != 1

    def to_chunks(a):
        return a.astype(jnp.float32).reshape(B, n, c, H, a.shape[-1]).transpose(1, 0, 3, 2, 4)

    qc, kc, vc, gc = to_chunks(q), to_chunks(k), to_chunks(v), to_chunks(g)
    causal = jnp.tril(jnp.ones((c, c), dtype=bool))

    def step(S, inp):
        qi, ki, vi, gi = inp
        G = jnp.cumsum(gi, axis=2)
        Gl = G[:, :, -1:, :]
        if per_channel:
            diff = G[:, :, :, None, :] - G[:, :, None, :, :]
            m = causal[:, :, None]
            dec = jnp.where(m, jnp.exp(jnp.where(m, diff, 0.0)), 0.0)
            att = jnp.einsum('bhik,bhjk,bhijk->bhij', qi, ki, dec)
        else:
            Gs = G[..., 0]
            diff = Gs[:, :, :, None] - Gs[:, :, None, :]
            dec = jnp.where(causal, jnp.exp(jnp.where(causal, diff, 0.0)), 0.0)
            att = jnp.einsum('bhik,bhjk->bhij', qi, ki) * dec
        o = jnp.einsum('bhij,bhjv->bhiv', att, vi) + jnp.einsum('bhik,bhkv->bhiv', qi * jnp.exp(G), S)
        S_new = jnp.exp(Gl[:, :, 0, :])[..., None] * S + jnp.einsum('bhjk,bhjv->bhkv', ki * jnp.exp(Gl - G), vi)
        return S_new, o

    S_fin, oc = lax.scan(step, s0.astype(jnp.float32), (qc, kc, vc, gc))
    o = oc.transpose(1, 0, 3, 2, 4).reshape(B, L, H, V)
    return o, S_fin


def causal_conv(u, buf, w, b):
    full = jnp.concatenate([buf.astype(u.dtype), u], axis=1)
    out = lax.conv_general_dilated(full, w[:, None, :].astype(u.dtype), window_strides=(1,), padding='VALID',
                                   dimension_numbers=('NWC', 'WIO', 'NWC'), feature_group_count=u.shape[-1])
    return out + b.astype(u.dtype), full[:, -(SSM_CONV - 1):]


def mixer(xn, pos, lb, s_ret, s_hg, s_ssm, s_conv, w_in, ret_norm, hg_norm, conv_w, conv_b,
          dt_bias, a_log, d_skip, ssm_norm, w_branch, w_out):
    B, L, _ = xn.shape
    f32 = jnp.float32
    proj = xn @ w_in
    rq, rk, rv, rg, hq, hf, hi, hgt, z, xbc, dt, gates = jnp.split(proj, np.cumsum(SPLITS)[:-1].tolist(), axis=-1)

    q_r = rope(rq.reshape(B, L, RET_HEADS, RET_DK), pos)
    k_r = rope(rk.reshape(B, L, RET_HEADS, RET_DK), pos) * (RET_DK ** -0.5)
    v_r = rv.reshape(B, L, RET_HEADS, RET_DV)
    log_gamma = jnp.log1p(-jnp.exp2(-5.0 - jnp.arange(RET_HEADS, dtype=f32)))
    g_r = jnp.broadcast_to(log_gamma[:, None], (B, L, RET_HEADS, 1))
    o_r, s_ret_new = chunked_gla(q_r, k_r, v_r, g_r, s_ret)
    o_r = head_layernorm(o_r, ret_norm.reshape(RET_HEADS, RET_DV)).reshape(B, L, RET_V)
    y_r = (jax.nn.silu(rg.astype(f32)) * o_r).astype(xn.dtype)

    q_h = jax.nn.silu(hq.reshape(B, L, HG_HEADS, HG_EXPAND).astype(f32))
    lbh = lb.reshape(HG_HEADS, HG_EXPAND)
    f_h = lbh + (1.0 - lbh) * jax.nn.sigmoid(hf.reshape(B, L, HG_HEADS, HG_EXPAND).astype(f32))
    f_h = jnp.clip(f_h, 1e-6, 1.0)
    g_h = jnp.log(f_h)
    k_h = 1.0 - f_h
    o_h, s_hg_new = chunked_gla(q_h, k_h, hi.reshape(B, L, HG_HEADS, HG_DV), g_h, s_hg)
    o_h = head_rmsnorm(o_h, hg_norm.reshape(HG_HEADS, HG_DV)).reshape(B, L, HG_V)
    y_h = (o_h * jax.nn.sigmoid(hgt.astype(f32))).astype(xn.dtype)

    xbc_c, s_conv_new = causal_conv(xbc, s_conv, conv_w, conv_b)
    xbc_c = jax.nn.silu(xbc_c.astype(f32))
    xs, Bm, Cm = jnp.split(xbc_c, [SSM_INNER, SSM_INNER + SSM_GROUPS * SSM_STATE], axis=-1)
    xs = xs.reshape(B, L, SSM_HEADS, SSM_HEADDIM)
    rep = SSM_HEADS // SSM_GROUPS
    Bm = jnp.repeat(Bm.reshape(B, L, SSM_GROUPS, SSM_STATE), rep, axis=2)
    Cm = jnp.repeat(Cm.reshape(B, L, SSM_GROUPS, SSM_STATE), rep, axis=2)
    dtv = jax.nn.softplus(dt.astype(f32) + dt_bias.astype(f32))
    A = -jnp.exp(a_log.astype(f32))
    o_m, s_ssm_new = chunked_gla(Cm, Bm * dtv[..., None], xs, (dtv * A)[..., None], s_ssm)
    o_m = (o_m + d_skip.astype(f32)[:, None] * xs).reshape(B, L, SSM_INNER)
    y_m = rmsnorm(o_m * jax.nn.silu(z.astype(f32)), ssm_norm).astype(xn.dtype)

    ys = jnp.stack([y_r, y_h, y_m], axis=2)
    branch = jnp.einsum('blnw,nwd->blnd', ys, w_branch)
    gate = jax.nn.sigmoid(gates.reshape(B, L, N_BRANCH, D_MODEL).astype(f32))
    merged = jnp.sum(gate * branch.astype(f32), axis=2).astype(xn.dtype)
    return merged @ w_out, s_ret_new, s_hg_new, s_ssm_new, s_conv_new


def block(x, pos, lb, s_ret, s_hg, s_ssm, s_conv, w_in, ret_norm, hg_norm, conv_w, conv_b, dt_bias, a_log,
          d_skip, ssm_norm, w_branch, w_out, n_mix_pre, n_mix_post, n_ffn_pre, n_ffn_post, w_up, w_down):
    m, s_ret, s_hg, s_ssm, s_conv = mixer(rmsnorm(x, n_mix_pre), pos, lb, s_ret, s_hg, s_ssm, s_conv, w_in,
                                          ret_norm, hg_norm, conv_w, conv_b, dt_bias, a_log, d_skip,
                                          ssm_norm, w_branch, w_out)
    x = x + rmsnorm(m, n_mix_post)
    f = jnp.square(jax.nn.relu(rmsnorm(x, n_ffn_pre) @ w_up)) @ w_down
    x = x + rmsnorm(f, n_ffn_post)
    return x, s_ret, s_hg, s_ssm, s_conv


def setup_inputs(seed: int = 0) -> dict:
    key = jax.random.key(seed)
    ks = jax.random.split(key, 32)

    def nrm(k, shape, s):
        return jax.random.normal(k, shape, jnp.float32) * s

    dt0 = jnp.exp(jax.random.uniform(ks[12], (DEPTH, SSM_HEADS), jnp.float32, math.log(1e-3), math.log(1e-1)))
    return {
        "x_prompt": nrm(ks[0], (BATCH, SEQ, D_MODEL), 1.0),
        "x_sample": nrm(ks[1], (DEC_BATCH, DEC_SEQ, D_MODEL), 1.0),
        "state_ret": nrm(ks[2], (DEPTH, DEC_BATCH, RET_HEADS, RET_DK, RET_DV), 0.5),
        "state_hgrn": nrm(ks[3], (DEPTH, DEC_BATCH, HG_HEADS, HG_EXPAND, HG_DV), 0.5),
        "state_ssm": nrm(ks[4], (DEPTH, DEC_BATCH, SSM_HEADS, SSM_STATE, SSM_HEADDIM), 0.5),
        "state_conv": nrm(ks[5], (DEPTH, DEC_BATCH, SSM_CONV - 1, SSM_CONV_CH), 1.0),
        "w_in": nrm(ks[6], (DEPTH, D_MODEL, D_IN_PROJ), D_MODEL ** -0.5),
        "ret_norm": 1.0 + nrm(ks[7], (DEPTH, RET_V), 0.02),
        "hg_norm": 1.0 + nrm(ks[8], (DEPTH, HG_V), 0.02),
        "hg_lb_logits": nrm(ks[9], (DEPTH, HG_K), 1.0),
        "conv_w": nrm(ks[10], (DEPTH, SSM_CONV, SSM_CONV_CH), SSM_CONV ** -0.5),
        "conv_b": nrm(ks[11], (DEPTH, SSM_CONV_CH), 0.02),
        "dt_bias": dt0 + jnp.log(-jnp.expm1(-dt0)),
        "a_log": jnp.log(jax.random.uniform(ks[13], (DEPTH, SSM_HEADS), jnp.float32, 1.0, 16.0)),
        "d_skip": 1.0 + nrm(ks[14], (DEPTH, SSM_HEADS), 0.1),
        "ssm_norm": 1.0 + nrm(ks[15], (DEPTH, SSM_INNER), 0.02),
        "w_branch": nrm(ks[16], (DEPTH, N_BRANCH, BRANCH_W, D_MODEL), BRANCH_W ** -0.5),
        "w_out": nrm(ks[17], (DEPTH, D_MODEL, D_MODEL), D_MODEL ** -0.5),
        "norm_mix_pre": 1.0 + nrm(ks[18], (DEPTH, D_MODEL), 0.02),
        "norm_mix_post": 1.0 + nrm(ks[19], (DEPTH, D_MODEL), 0.02),
        "norm_ffn_pre": 1.0 + nrm(ks[20], (DEPTH, D_MODEL), 0.02),
        "norm_ffn_post": 1.0 + nrm(ks[21], (DEPTH, D_MODEL), 0.02),
        "w_up": nrm(ks[22], (DEPTH, D_MODEL, D_FF), D_MODEL ** -0.5),
        "w_down": nrm(ks[23], (DEPTH, D_FF, D_MODEL), D_FF ** -0.5),
    }


def reference(x_prompt, x_sample, state_ret, state_hgrn, state_ssm, state_conv, w_in, ret_norm, hg_norm,
              hg_lb_logits, conv_w, conv_b, dt_bias, a_log, d_skip, ssm_norm, w_branch, w_out, norm_mix_pre,
              norm_mix_post, norm_ffn_pre, norm_ffn_post, w_up, w_down):
    f32 = jnp.float32
    Bp, Lp, _ = x_prompt.shape
    Bs, Ls, _ = x_sample.shape
    pos_p = jnp.arange(Lp)
    pos_s = PAST_LEN + jnp.arange(Ls)
    lb_w = jax.nn.softmax(hg_lb_logits.astype(f32), axis=0)
    lb_all = jnp.cumsum(lb_w, axis=0) - lb_w[0]

    sp = (jnp.zeros((Bp, RET_HEADS, RET_DK, RET_DV), f32), jnp.zeros((Bp, HG_HEADS, HG_EXPAND, HG_DV), f32),
          jnp.zeros((Bp, SSM_HEADS, SSM_STATE, SSM_HEADDIM), f32), jnp.zeros((Bp, SSM_CONV - 1, SSM_CONV_CH), x_prompt.dtype))
    hp, hs = x_prompt, x_sample
    p_ret, p_hg, p_ssm, p_conv = [], [], [], []
    s_ret, s_hg, s_ssm, s_conv = [], [], [], []
    for l in range(DEPTH):
        lp = (w_in[l], ret_norm[l], hg_norm[l], conv_w[l], conv_b[l], dt_bias[l], a_log[l], d_skip[l], ssm_norm[l],
              w_branch[l], w_out[l], norm_mix_pre[l], norm_mix_post[l], norm_ffn_pre[l], norm_ffn_post[l], w_up[l], w_down[l])
        hp, a, b, c, d = block(hp, pos_p, lb_all[l], sp[0], sp[1], sp[2], sp[3], *lp)
        p_ret.append(a); p_hg.append(b); p_ssm.append(c); p_conv.append(d)
        hs, a, b, c, d = block(hs, pos_s, lb_all[l], state_ret[l], state_hgrn[l], state_ssm[l], state_conv[l], *lp)
        s_ret.append(a); s_hg.append(b); s_ssm.append(c); s_conv.append(d)
    return (hp, hs, jnp.stack(p_ret), jnp.stack(p_hg), jnp.stack(p_ssm), jnp.stack(p_conv),
            jnp.stack(s_ret), jnp.stack(s_hg), jnp.stack(s_ssm), jnp.stack(s_conv))
```

```python
import functools
import math

import numpy as np
import jax
import jax.numpy as jnp
from jax import lax
from jax.experimental import pallas as pl
from jax.experimental.pallas import tpu as pltpu

F32 = jnp.float32
BF16 = jnp.bfloat16

D_MODEL = 1024
PAST_LEN = 16384
RET_HEADS, RET_DK, RET_DV = 4, 128, 256
ROPE_BASE = 10000.0
HG_HEADS, HG_EXPAND, HG_DV = 8, 128, 128
SSM_HEADS, SSM_HEADDIM, SSM_GROUPS, SSM_STATE, SSM_CONV = 16, 64, 4, 128, 4
SSM_INNER = SSM_HEADS * SSM_HEADDIM
SSM_CONV_CH = SSM_INNER + 2 * SSM_GROUPS * SSM_STATE
D_FF = 4 * D_MODEL
EPS = 1e-6
HG_CLIP_MIN = 1e-6

N_MAIN = 10240
N_PROJ = N_MAIN + 3 * D_MODEL
DT_PAD = 128

LANES = 128
SUBLANES = 8
VMEM_LIMIT = 56 * 1024 * 1024

CHUNK_ROWS = 256
DEC_TOKENS = 8


def _cparams(sem):
    return pltpu.CompilerParams(dimension_semantics=sem, vmem_limit_bytes=VMEM_LIMIT)


def _dot(a, b):
    return jnp.dot(a, b, preferred_element_type=F32)


def _dot_nt(a, b):
    return lax.dot_general(a, b, (((1,), (1,)), ((), ())), preferred_element_type=F32)


def _dot_tn(a, b):
    return lax.dot_general(a, b, (((0,), (0,)), ((), ())), preferred_element_type=F32)


def _split3(a):
    hi = a.astype(BF16)
    r1 = a - hi.astype(F32)
    mid = r1.astype(BF16)
    lo = (r1 - mid.astype(F32)).astype(BF16)
    return hi, mid, lo


def _dot3_left(m01, a):
    hi, mid, lo = _split3(a)
    return _dot(m01, hi) + _dot(m01, mid) + _dot(m01, lo)


def _dot3_right(a, m01):
    hi, mid, lo = _split3(a)
    return _dot(hi, m01) + _dot(mid, m01) + _dot(lo, m01)


def _sigmoid(x):
    return 1.0 / (1.0 + jnp.exp(-x))


def _silu(x):
    return x * _sigmoid(x)


def _softplus(x):
    return jnp.maximum(x, 0.0) + jnp.log1p(jnp.exp(-jnp.abs(x)))


def _rms(x, gain):
    return x * lax.rsqrt(jnp.mean(x * x, axis=-1, keepdims=True) + EPS) * gain


def _in_proj_kernel(x_ref, g_ref, w_ref, wdt_ref, o_ref, dt_ref, xn_scr):
    @pl.when(pl.program_id(1) == 0)
    def _():
        xn_scr[...] = _rms(x_ref[...], g_ref[...]).astype(BF16)
        dt_ref[...] = _dot(xn_scr[...], wdt_ref[...])

    o_ref[...] = _dot(xn_scr[...], w_ref[...])


def _in_proj(x, gain, w_main, w_dt, tm):
    t = x.shape[0]
    tn = 1024
    return pl.pallas_call(
        _in_proj_kernel,
        grid=(t // tm, N_PROJ // tn),
        in_specs=[
            pl.BlockSpec((tm, D_MODEL), lambda i, j: (i, 0)),
            pl.BlockSpec((1, D_MODEL), lambda i, j: (0, 0)),
            pl.BlockSpec((D_MODEL, tn), lambda i, j: (0, j)),
            pl.BlockSpec((D_MODEL, DT_PAD), lambda i, j: (0, 0)),
        ],
        out_specs=[
            pl.BlockSpec((tm, tn), lambda i, j: (i, j)),
            pl.BlockSpec((tm, DT_PAD), lambda i, j: (i, 0)),
        ],
        out_shape=[jax.ShapeDtypeStruct((t, N_PROJ), F32), jax.ShapeDtypeStruct((t, DT_PAD), F32)],
        scratch_shapes=[pltpu.VMEM((tm, D_MODEL), BF16)],
        compiler_params=_cparams(("parallel", "arbitrary")),
        name="in_proj",
    )(x, gain, w_main, w_dt)


def _ret_log_gamma():
    return np.log1p(-np.exp2(-5.0 - np.arange(RET_HEADS, dtype=np.float64)))


def _ret_consts(r):
    lg = _ret_log_gamma()
    i = np.arange(r)
    diff = (i[:, None] - i[None, :]).astype(np.float64)
    dec = np.where(diff >= 0, np.exp(lg[:, None, None] * np.maximum(diff, 0.0)), 0.0)
    qd = np.exp(lg[:, None] * (i + 1.0))
    kd = np.exp(lg[:, None] * (r - 1.0 - i))
    gl = tuple(float(v) for v in np.exp(lg * r))
    bc = lambda a: jnp.asarray(np.broadcast_to(a[:, :, None], (RET_HEADS, r, LANES)), F32)
    return jnp.asarray(dec, F32), bc(qd), bc(kd), gl


def _rope_tables(pos):
    half = RET_DK // 2
    inv = np.float32(ROPE_BASE) ** (-(np.arange(half, dtype=np.float32) / np.float32(half)))
    ang = (pos.astype(np.float32)[:, None] * inv[None, :]).astype(np.float64)
    cos, sin = np.cos(ang), np.sin(ang)
    cos2 = np.concatenate([cos, cos], axis=1)
    sin2 = np.concatenate([-sin, sin], axis=1)
    scale = RET_DK ** -0.5
    return (jnp.asarray(cos2, F32), jnp.asarray(sin2, F32),
            jnp.asarray(cos2 * scale, F32), jnp.asarray(sin2 * scale, F32))


def _rope(t, cos2, sin2):
    return t * cos2 + pltpu.roll(t, RET_DK // 2, 1) * sin2


def _head_layernorm(o, gain):
    oc = o - jnp.mean(o, axis=-1, keepdims=True)
    return oc * lax.rsqrt(jnp.mean(oc * oc, axis=-1, keepdims=True) + EPS) * gain


def _ret_kernel(rq_ref, rk_ref, rv_ref, rg_ref, cq_ref, sq_ref, ck_ref, sk_ref, dec_ref, qd_ref, kd_ref,
                gain_ref, y_ref, sout_ref, s_scr, *, gl):
    c = pl.program_id(1)

    @pl.when(c == 0)
    def _():
        s_scr[...] = jnp.zeros_like(s_scr)

    cq, sq, ck, sk = cq_ref[...], sq_ref[...], ck_ref[...], sk_ref[...]
    for h in range(RET_HEADS):
        ks = slice(h * RET_DK, (h + 1) * RET_DK)
        vs = slice(h * RET_DV, (h + 1) * RET_DV)
        q = _rope(rq_ref[:, ks], cq, sq)
        k = _rope(rk_ref[:, ks], ck, sk)
        v = rv_ref[:, vs].astype(BF16)
        att = (_dot_nt(q.astype(BF16), k.astype(BF16)) * dec_ref[h]).astype(BF16)
        s_old = s_scr[h]
        o = _dot(att, v) + _dot((q * qd_ref[h]).astype(BF16), s_old.astype(BF16))
        s_scr[h] = gl[h] * s_old + _dot_tn((k * kd_ref[h]).astype(BF16), v)
        y = _silu(rg_ref[:, vs]) * _head_layernorm(o, gain_ref[:, vs])
        y_ref[:, vs] = y.astype(y_ref.dtype)

    @pl.when(c == pl.num_programs(1) - 1)
    def _():
        sout_ref[0] = s_scr[...]


def _ret_prompt(proj, gain, batch, seq):
    r = CHUNK_ROWS
    nc = seq // r
    dec, qd, kd, gl = _ret_consts(r)
    cq, sq, ck, sk = _rope_tables(np.arange(seq))
    row = lambda b, c: b * nc + c
    tab = pl.BlockSpec((r, RET_DK), lambda b, c: (c, 0))
    const3 = lambda shape: pl.BlockSpec(shape, lambda b, c: (0, 0, 0))
    return pl.pallas_call(
        functools.partial(_ret_kernel, gl=gl),
        grid=(batch, nc),
        in_specs=[
            pl.BlockSpec((r, 512), lambda b, c: (row(b, c), 0)),
            pl.BlockSpec((r, 512), lambda b, c: (row(b, c), 1)),
            pl.BlockSpec((r, 1024), lambda b, c: (row(b, c), 1)),
            pl.BlockSpec((r, 1024), lambda b, c: (row(b, c), 2)),
            tab, tab, tab, tab,
            const3((RET_HEADS, r, r)), const3((RET_HEADS, r, LANES)), const3((RET_HEADS, r, LANES)),
            pl.BlockSpec((1, 1024), lambda b, c: (0, 0)),
        ],
        out_specs=[
            pl.BlockSpec((r, 1024), lambda b, c: (row(b, c), 0)),
            pl.BlockSpec((1, RET_HEADS, RET_DK, RET_DV), lambda b, c: (b, 0, 0, 0)),
        ],
        out_shape=[jax.ShapeDtypeStruct((batch * seq, 1024), BF16),
                   jax.ShapeDtypeStruct((batch, RET_HEADS, RET_DK, RET_DV), F32)],
        scratch_shapes=[pltpu.VMEM((RET_HEADS, RET_DK, RET_DV), F32)],
        compiler_params=_cparams(("parallel", "arbitrary")),
        name="ret_prompt",
    )(proj, proj, proj, proj, cq, sq, ck, sk, dec, qd, kd, gain)


HG_DIAG = 4


def _hg_levels(r):
    levels, b = [], r // 2
    while b >= HG_DIAG:
        levels.append(b)
        b //= 2
    return levels


def _hg_masks(r):
    i = np.arange(r)
    out = []
    for b in _hg_levels(r):
        same = (i[:, None] // (2 * b)) == (i[None, :] // (2 * b))
        out.append(same & ((i[:, None] & b) != 0) & ((i[None, :] & b) == 0))
    out.append(((i[:, None] // HG_DIAG) == (i[None, :] // HG_DIAG)) & (i[:, None] >= i[None, :]))
    return jnp.asarray(np.stack(out).astype(np.float32))


def _tri_ones(r):
    i = np.arange(r)
    return jnp.asarray((i[:, None] >= i[None, :]).astype(np.float32), BF16)


def _vreg_row_bcast(g3, j):
    return jnp.broadcast_to(g3[:, j:j + 1, :], g3.shape)


def _hgrn_kernel(hq_ref, hf_ref, hi_ref, hg_ref, lb_ref, gain_ref, tri_ref, mask_ref, y_ref, sout_ref,
                 st_scr, *, r):
    c = pl.program_id(1)

    @pl.when(c == 0)
    def _():
        st_scr[...] = jnp.zeros_like(st_scr)

    nt = r // SUBLANES
    row = lax.broadcasted_iota(jnp.int32, (r, LANES), 0)
    tri = tri_ref[...]
    levels = _hg_levels(r)
    for h in range(HG_HEADS):
        sl = slice(h * LANES, (h + 1) * LANES)
        q = _silu(hq_ref[:, sl])
        lb = lb_ref[:, sl]
        f = jnp.clip(lb + (1.0 - lb) * _sigmoid(hf_ref[:, sl]), HG_CLIP_MIN, 1.0)
        kk = 1.0 - f
        g_cum = _dot3_left(tri, jnp.log(f))
        g_last = g_cum[r - 1:r, :]
        g3 = g_cum.reshape(nt, SUBLANES, LANES)
        tile_last = _vreg_row_bcast(g3, SUBLANES - 1)

        att = jnp.zeros((r, r), F32)
        for lvl, b in enumerate(levels):
            if b >= SUBLANES:
                per = 2 * b // SUBLANES
                t4 = tile_last.reshape(nt // per, per, SUBLANES, LANES)
                ref = jnp.broadcast_to(t4[:, per // 2 - 1:per // 2], t4.shape).reshape(r, LANES)
            else:
                ref = _vreg_row_bcast(g3, b - 1).reshape(r, LANES)
            e = jnp.exp(-jnp.abs(g_cum - ref))
            upper = (row & b) != 0
            x = jnp.where(upper, q, kk) * e
            qb = jnp.where(upper, x, 0.0).astype(BF16)
            kb = jnp.where(upper, 0.0, x).astype(BF16)
            att = att + _dot_nt(qb, kb) * mask_ref[lvl]
        ref = jnp.where((row & HG_DIAG) != 0, _vreg_row_bcast(g3, HG_DIAG).reshape(r, LANES),
                        _vreg_row_bcast(g3, 0).reshape(r, LANES))
        qb = (q * jnp.exp(g_cum - ref)).astype(BF16)
        kb = (kk * jnp.exp(ref - g_cum)).astype(BF16)
        att = att + _dot_nt(qb, kb) * mask_ref[len(levels)]

        v = hi_ref[:, sl].astype(BF16)
        st_old = st_scr[h]
        o = _dot(att.astype(BF16), v) + _dot_nt((q * jnp.exp(g_cum)).astype(BF16), st_old.astype(BF16))
        st_scr[h] = st_old * jnp.exp(g_last) + _dot_tn(v, (kk * jnp.exp(g_last - g_cum)).astype(BF16))
        o = o * lax.rsqrt(jnp.mean(o * o, axis=-1, keepdims=True) + EPS) * gain_ref[:, sl]
        y_ref[:, sl] = (o * _sigmoid(hg_ref[:, sl])).astype(y_ref.dtype)

    @pl.when(c == pl.num_programs(1) - 1)
    def _():
        for h in range(HG_HEADS):
            sout_ref[0, h] = st_scr[h].T


def _hgrn_prompt(proj, lb, gain, batch, seq):
    r = CHUNK_ROWS
    nc = seq // r
    masks = _hg_masks(r)
    row = lambda b, c: b * nc + c
    blk = lambda j: pl.BlockSpec((r, 1024), lambda b, c: (row(b, c), j))
    vec = pl.BlockSpec((1, 1024), lambda b, c: (0, 0))
    return pl.pallas_call(
        functools.partial(_hgrn_kernel, r=r),
        grid=(batch, nc),
        in_specs=[blk(3), blk(4), blk(5), blk(6), vec, vec,
                  pl.BlockSpec((r, r), lambda b, c: (0, 0)),
                  pl.BlockSpec(masks.shape, lambda b, c: (0, 0, 0))],
        out_specs=[
            pl.BlockSpec((r, 1024), lambda b, c: (row(b, c), 0)),
            pl.BlockSpec((1, HG_HEADS, HG_EXPAND, HG_DV), lambda b, c: (b, 0, 0, 0)),
        ],
        out_shape=[jax.ShapeDtypeStruct((batch * seq, 1024), BF16),
                   jax.ShapeDtypeStruct((batch, HG_HEADS, HG_EXPAND, HG_DV), F32)],
        scratch_shapes=[pltpu.VMEM((HG_HEADS, HG_DV, HG_EXPAND), F32)],
        compiler_params=_cparams(("parallel", "arbitrary")),
        name="hgrn_prompt",
    )(proj, proj, proj, proj, lb, gain, _tri_ones(r), masks)


HEADS_PER_GROUP = SSM_HEADS // SSM_GROUPS
GROUP_W = HEADS_PER_GROUP * SSM_HEADDIM
CONV_PAD = SUBLANES


def _head_expand(vals, h0, lane_head):
    out = jnp.zeros((vals.shape[0], GROUP_W), F32)
    for j in range(HEADS_PER_GROUP):
        out = jnp.where(lane_head == j, vals[:, h0 + j:h0 + j + 1], out)
    return out


def _ssd_kernel(z_ref, xbc_ref, dt_ref, cw_ref, cb_ref, dtb_ref, alog_ref, dsk_ref, norm_ref, tri_ref,
                y_ref, sout_ref, cout_ref, ext_scr, s_scr, o_scr, *, r):
    c = pl.program_id(1)
    last = pl.num_programs(1) - 1

    @pl.when(c == 0)
    def _():
        ext_scr[0:CONV_PAD, :] = jnp.zeros((CONV_PAD, SSM_CONV_CH), F32)
        s_scr[...] = jnp.zeros_like(s_scr)

    ext_scr[CONV_PAD:CONV_PAD + r, :] = xbc_ref[...]
    conv = cb_ref[...]
    for w in range(SSM_CONV):
        off = CONV_PAD - (SSM_CONV - 1) + w
        conv = conv + cw_ref[w:w + 1, :] * ext_scr[off:off + r, :]

    @pl.when(c == last)
    def _():
        cout_ref[0] = ext_scr[CONV_PAD + r - (SSM_CONV - 1):CONV_PAD + r, :]

    ext_scr[0:CONV_PAD, :] = ext_scr[r:r + CONV_PAD, :]

    xc = _silu(conv)
    xs = xc[:, :SSM_INNER]
    bm = xc[:, SSM_INNER:SSM_INNER + SSM_GROUPS * SSM_STATE]
    cm = xc[:, SSM_INNER + SSM_GROUPS * SSM_STATE:]

    dt = _softplus(dt_ref[...] + dtb_ref[...])
    g_cum = _dot3_left(tri_ref[...], dt * (-jnp.exp(alog_ref[...])))
    g_last = g_cum[r - 1:r, :]
    g_cum_t = g_cum.T
    dt_t = dt.T
    e_cum = jnp.exp(g_cum)
    w_in = dt * jnp.exp(g_last - g_cum)
    e_last = jnp.exp(g_last)

    ri = lax.broadcasted_iota(jnp.int32, (r, r), 0)
    ci = lax.broadcasted_iota(jnp.int32, (r, r), 1)
    causal = ri >= ci
    lane_head = lax.broadcasted_iota(jnp.int32, (r, GROUP_W), 1) // SSM_HEADDIM
    lane_head1 = lax.broadcasted_iota(jnp.int32, (1, GROUP_W), 1) // SSM_HEADDIM

    for g in range(SSM_GROUPS):
        gs = slice(g * SSM_STATE, (g + 1) * SSM_STATE)
        ws = slice(g * GROUP_W, (g + 1) * GROUP_W)
        cg = cm[:, gs].astype(BF16)
        bg = bm[:, gs].astype(BF16)
        cb = _dot_nt(cg, bg)
        xs_g = xs[:, ws]
        s_old = s_scr[g]
        h0 = g * HEADS_PER_GROUP
        o_g = _head_expand(e_cum, h0, lane_head) * _dot(cg, s_old.astype(BF16))
        for j in range(HEADS_PER_GROUP):
            h = h0 + j
            diff = g_cum[:, h:h + 1] - g_cum_t[h:h + 1, :]
            dec = jnp.where(causal, jnp.exp(jnp.where(causal, diff, 0.0)), 0.0)
            att = (cb * dec * dt_t[h:h + 1, :]).astype(BF16)
            o_g = o_g + _dot(att, jnp.where(lane_head == j, xs_g, 0.0).astype(BF16))
        s_scr[g] = (s_old * _head_expand(e_last, h0, lane_head1)
                    + _dot_tn(bg, (xs_g * _head_expand(w_in, h0, lane_head)).astype(BF16)))
        o_scr[:, ws] = o_g + dsk_ref[:, ws] * xs_g

    y_ref[...] = _rms(o_scr[...] * _silu(z_ref[...]), norm_ref[...]).astype(y_ref.dtype)

    @pl.when(c == last)
    def _():
        for h in range(SSM_HEADS):
            j = h % HEADS_PER_GROUP
            sout_ref[0, h] = s_scr[h // HEADS_PER_GROUP][:, j * SSM_HEADDIM:(j + 1) * SSM_HEADDIM]


def _pad_lanes(v):
    return jnp.pad(v.astype(F32), (0, LANES - v.shape[0])).reshape(1, LANES)


def _ssd_prompt(proj, dt, conv_w, conv_b, dt_bias, a_log, dsk_exp, norm, batch, seq):
    r = CHUNK_ROWS
    nc = seq // r
    row = lambda b, c: b * nc + c
    vec = lambda n: pl.BlockSpec((1, n), lambda b, c: (0, 0))
    return pl.pallas_call(
        functools.partial(_ssd_kernel, r=r),
        grid=(batch, nc),
        in_specs=[
            pl.BlockSpec((r, 1024), lambda b, c: (row(b, c), 7)),
            pl.BlockSpec((r, SSM_CONV_CH), lambda b, c: (row(b, c), 4)),
            pl.BlockSpec((r, DT_PAD), lambda b, c: (row(b, c), 0)),
            pl.BlockSpec((SSM_CONV, SSM_CONV_CH), lambda b, c: (0, 0)),
            vec(SSM_CONV_CH), vec(LANES), vec(LANES), vec(SSM_INNER), vec(SSM_INNER),
            pl.BlockSpec((r, r), lambda b, c: (0, 0)),
        ],
        out_specs=[
            pl.BlockSpec((r, 1024), lambda b, c: (row(b, c), 0)),
            pl.BlockSpec((1, SSM_HEADS, SSM_STATE, SSM_HEADDIM), lambda b, c: (b, 0, 0, 0)),
            pl.BlockSpec((1, SSM_CONV - 1, SSM_CONV_CH), lambda b, c: (b, 0, 0)),
        ],
        out_shape=[jax.ShapeDtypeStruct((batch * seq, 1024), BF16),
                   jax.ShapeDtypeStruct((batch, SSM_HEADS, SSM_STATE, SSM_HEADDIM), F32),
                   jax.ShapeDtypeStruct((batch, SSM_CONV - 1, SSM_CONV_CH), F32)],
        scratch_shapes=[pltpu.VMEM((CONV_PAD + r, SSM_CONV_CH), F32),
                        pltpu.VMEM((SSM_GROUPS, SSM_STATE, GROUP_W), F32),
                        pltpu.VMEM((r, SSM_INNER), F32)],
        compiler_params=_cparams(("parallel", "arbitrary")),
        name="ssd_prompt",
    )(proj, proj, dt, conv_w, conv_b.reshape(1, -1), _pad_lanes(dt_bias), _pad_lanes(a_log), dsk_exp, norm,
      _tri_ones(r))


def _merge_kernel(x_ref, yr_ref, yh_ref, ym_ref, g0_ref, g1_ref, g2_ref, wb_ref, wo_ref, npost_ref, o_ref):
    merged = _sigmoid(g0_ref[...]) * _dot(yr_ref[...].astype(BF16), wb_ref[0])
    merged = merged + _sigmoid(g1_ref[...]) * _dot(yh_ref[...].astype(BF16), wb_ref[1])
    merged = merged + _sigmoid(g2_ref[...]) * _dot(ym_ref[...].astype(BF16), wb_ref[2])
    m = _dot(merged.astype(BF16), wo_ref[...])
    o_ref[...] = x_ref[...] + _rms(m, npost_ref[...])


def _merge(x, yr, yh, ym, proj, wb, wo, npost, tm):
    t = x.shape[0]
    rows = lambda j: pl.BlockSpec((tm, D_MODEL), lambda i: (i, j))
    return pl.pallas_call(
        _merge_kernel,
        grid=(t // tm,),
        in_specs=[rows(0), rows(0), rows(0), rows(0), rows(10), rows(11), rows(12),
                  pl.BlockSpec((3, D_MODEL, D_MODEL), lambda i: (0, 0, 0)),
                  pl.BlockSpec((D_MODEL, D_MODEL), lambda i: (0, 0)),
                  pl.BlockSpec((1, D_MODEL), lambda i: (0, 0))],
        out_specs=rows(0),
        out_shape=jax.ShapeDtypeStruct((t, D_MODEL), F32),
        compiler_params=_cparams(("parallel",)),
        name="merge",
    )(x, yr, yh, ym, proj, proj, proj, wb, wo, npost)


def _ffn_kernel(x_ref, npre_ref, npost_ref, wu_ref, wd_ref, o_ref, xn_scr, acc_scr):
    j = pl.program_id(1)

    @pl.when(j == 0)
    def _():
        xn_scr[...] = _rms(x_ref[...], npre_ref[...]).astype(BF16)
        acc_scr[...] = jnp.zeros_like(acc_scr)

    hid = jnp.square(jnp.maximum(_dot(xn_scr[...], wu_ref[...]), 0.0))
    acc_scr[...] += _dot(hid.astype(BF16), wd_ref[...])

    @pl.when(j == pl.num_programs(1) - 1)
    def _():
        o_ref[...] = x_ref[...] + _rms(acc_scr[...], npost_ref[...])


def _ffn(x, npre, npost, wu, wd, tm):
    t = x.shape[0]
    tf = 1024
    vec = pl.BlockSpec((1, D_MODEL), lambda i, j: (0, 0))
    return pl.pallas_call(
        _ffn_kernel,
        grid=(t // tm, D_FF // tf),
        in_specs=[pl.BlockSpec((tm, D_MODEL), lambda i, j: (i, 0)), vec, vec,
                  pl.BlockSpec((D_MODEL, tf), lambda i, j: (0, j)),
                  pl.BlockSpec((tf, D_MODEL), lambda i, j: (j, 0))],
        out_specs=pl.BlockSpec((tm, D_MODEL), lambda i, j: (i, 0)),
        out_shape=jax.ShapeDtypeStruct((t, D_MODEL), F32),
        scratch_shapes=[pltpu.VMEM((tm, D_MODEL), BF16), pltpu.VMEM((tm, D_MODEL), F32)],
        compiler_params=_cparams(("parallel", "arbitrary")),
        name="ffn",
    )(x, npre, npost, wu, wd)


def _dec_prep_kernel(proj_ref, dt_ref, cst_ref, cq_ref, sq_ref, ck_ref, sk_ref, lb_ref, cw_ref, cb_ref,
                     dtb_ref, alog_ref, hexp_ref,
                     cret_ref, chg_ref, cssd_ref, xs_ref, xdt_ref, drow_ref, cnew_ref):
    n = proj_ref.shape[0]
    cq, sq, ck, sk = cq_ref[...], sq_ref[...], ck_ref[...], sk_ref[...]
    for h in range(RET_HEADS):
        ks = slice(h * RET_DK, (h + 1) * RET_DK)
        cret_ref[:, ks] = _rope(proj_ref[:, ks], cq, sq)
        k_off = RET_HEADS * RET_DK
        cret_ref[:, k_off + h * RET_DK:k_off + (h + 1) * RET_DK] = _rope(
            proj_ref[:, k_off + h * RET_DK:k_off + (h + 1) * RET_DK], ck, sk)

    lb = lb_ref[...]
    chg_ref[:, 0:1024] = _silu(proj_ref[:, 3072:4096])
    chg_ref[:, 1024:2048] = jnp.clip(lb + (1.0 - lb) * _sigmoid(proj_ref[:, 4096:5120]), HG_CLIP_MIN, 1.0)

    u = proj_ref[:, 8192:8192 + SSM_CONV_CH]
    conv = cb_ref[...] + cw_ref[SSM_CONV - 1:SSM_CONV, :] * u
    for w in range(SSM_CONV - 1):
        conv = conv + cw_ref[w:w + 1, :] * cst_ref[w]
    cnew_ref[0] = cst_ref[1]
    cnew_ref[1] = cst_ref[2]
    cnew_ref[2] = u
    xc = _silu(conv)
    xs = xc[:, :SSM_INNER]
    xs_ref[...] = xs
    cssd_ref[:, 0:512] = xc[:, SSM_INNER + 512:]
    cssd_ref[:, 512:1024] = xc[:, SSM_INNER:SSM_INNER + 512]
    dt = _softplus(dt_ref[...] + dtb_ref[...])
    decay = jnp.exp(dt * (-jnp.exp(alog_ref[...])))
    hexp = hexp_ref[...]
    xdt_ref[...] = xs * _dot3_right(dt, hexp)
    drow_ref[...] = _dot3_right(decay, hexp)
    del n


def _dec_prep(proj, dt, conv_state_t, lb, conv_w, conv_b, dt_bias, a_log):
    n = proj.shape[0]
    cq, sq, ck, sk = _rope_tables(np.full((n,), PAST_LEN))
    hexp = np.zeros((LANES, SSM_INNER), np.float32)
    for h in range(SSM_HEADS):
        hexp[h, h * SSM_HEADDIM:(h + 1) * SSM_HEADDIM] = 1.0
    outs = [(n, 1024), (n, 2048), (n, 1024), (n, 1024), (n, 1024), (n, 1024)]
    return pl.pallas_call(
        _dec_prep_kernel,
        out_shape=[jax.ShapeDtypeStruct(s, F32) for s in outs]
        + [jax.ShapeDtypeStruct((SSM_CONV - 1, n, SSM_CONV_CH), F32)],
        compiler_params=pltpu.CompilerParams(vmem_limit_bytes=VMEM_LIMIT),
        name="dec_prep",
    )(proj, dt, conv_state_t, cq, sq, ck, sk, lb, conv_w, conv_b.reshape(1, -1), _pad_lanes(dt_bias),
      _pad_lanes(a_log), jnp.asarray(hexp, BF16))


def _cols(a):
    n, ch = a.shape
    return a.reshape(n // DEC_TOKENS, DEC_TOKENS, ch).transpose(0, 2, 1)


def _ret_dec_kernel(ct_ref, s_ref, v_ref, rg_ref, gain_ref, y_ref, so_ref, o_scr, *, gamma):
    for i in range(DEC_TOKENS):
        for h in range(RET_HEADS):
            qcol = ct_ref[0, h * RET_DK:(h + 1) * RET_DK, i:i + 1]
            kcol = ct_ref[0, (RET_HEADS + h) * RET_DK:(RET_HEADS + h + 1) * RET_DK, i:i + 1]
            vs = slice(h * RET_DV, (h + 1) * RET_DV)
            s_new = gamma[h] * s_ref[i, h] + kcol * v_ref[i:i + 1, vs]
            so_ref[i, h] = s_new
            o_scr[i:i + 1, vs] = jnp.sum(qcol * s_new, axis=0, keepdims=True)
    for h in range(RET_HEADS):
        vs = slice(h * RET_DV, (h + 1) * RET_DV)
        y_ref[:, vs] = _silu(rg_ref[:, vs]) * _head_layernorm(o_scr[:, vs], gain_ref[:, vs])


def _ret_dec(ct, state, proj, gain):
    n = proj.shape[0]
    tb = DEC_TOKENS
    gamma = tuple(float(v) for v in np.exp(_ret_log_gamma()))
    sspec = pl.BlockSpec((tb, RET_HEADS, RET_DK, RET_DV), lambda g: (g, 0, 0, 0))
    return pl.pallas_call(
        functools.partial(_ret_dec_kernel, gamma=gamma),
        grid=(n // tb,),
        in_specs=[pl.BlockSpec((1, 2 * RET_HEADS * RET_DK, tb), lambda g: (g, 0, 0)), sspec,
                  pl.BlockSpec((tb, 1024), lambda g: (g, 1)), pl.BlockSpec((tb, 1024), lambda g: (g, 2)),
                  pl.BlockSpec((1, 1024), lambda g: (0, 0))],
        out_specs=[pl.BlockSpec((tb, 1024), lambda g: (g, 0)), sspec],
        out_shape=[jax.ShapeDtypeStruct((n, 1024), F32), jax.ShapeDtypeStruct(state.shape, F32)],
        scratch_shapes=[pltpu.VMEM((tb, 1024), F32)],
        compiler_params=_cparams(("parallel",)),
        name="ret_dec",
    )(ct, state, proj, proj, gain)


def _hgrn_dec_kernel(ct_ref, s_ref, v_ref, hg_ref, gain_ref, y_ref, so_ref, o_scr):
    for i in range(DEC_TOKENS):
        for h in range(HG_HEADS):
            sl = slice(h * LANES, (h + 1) * LANES)
            qcol = ct_ref[0, h * HG_EXPAND:(h + 1) * HG_EXPAND, i:i + 1]
            fcol = ct_ref[0, (HG_HEADS + h) * HG_EXPAND:(HG_HEADS + h + 1) * HG_EXPAND, i:i + 1]
            s_new = fcol * s_ref[i, h] + (1.0 - fcol) * v_ref[i:i + 1, sl]
            so_ref[i, h] = s_new
            o_scr[i:i + 1, sl] = jnp.sum(qcol * s_new, axis=0, keepdims=True)
    for h in range(HG_HEADS):
        sl = slice(h * LANES, (h + 1) * LANES)
        o = o_scr[:, sl]
        o = o * lax.rsqrt(jnp.mean(o * o, axis=-1, keepdims=True) + EPS) * gain_ref[:, sl]
        y_ref[:, sl] = o * _sigmoid(hg_ref[:, sl])


def _hgrn_dec(ct, state, proj, gain):
    n = proj.shape[0]
    tb = DEC_TOKENS
    sspec = pl.BlockSpec((tb, HG_HEADS, HG_EXPAND, HG_DV), lambda g: (g, 0, 0, 0))
    return pl.pallas_call(
        _hgrn_dec_kernel,
        grid=(n // tb,),
        in_specs=[pl.BlockSpec((1, 2 * HG_HEADS * HG_EXPAND, tb), lambda g: (g, 0, 0)), sspec,
                  pl.BlockSpec((tb, 1024), lambda g: (g, 5)), pl.BlockSpec((tb, 1024), lambda g: (g, 6)),
                  pl.BlockSpec((1, 1024), lambda g: (0, 0))],
        out_specs=[pl.BlockSpec((tb, 1024), lambda g: (g, 0)), sspec],
        out_shape=[jax.ShapeDtypeStruct((n, 1024), F32), jax.ShapeDtypeStruct(state.shape, F32)],
        scratch_shapes=[pltpu.VMEM((tb, 1024), F32)],
        compiler_params=_cparams(("parallel",)),
        name="hgrn_dec",
    )(ct, state, proj, proj, gain)


def _ssd_dec_kernel(ct_ref, s_ref, xs_ref, xdt_ref, drow_ref, z_ref, dsk_ref, norm_ref, y_ref, so_ref, o_scr):
    for i in range(DEC_TOKENS):
        for h in range(SSM_HEADS):
            g = h // HEADS_PER_GROUP
            hs = slice(h * SSM_HEADDIM, (h + 1) * SSM_HEADDIM)
            ccol = ct_ref[0, g * SSM_STATE:(g + 1) * SSM_STATE, i:i + 1]
            bcol = ct_ref[0, (SSM_GROUPS + g) * SSM_STATE:(SSM_GROUPS + g + 1) * SSM_STATE, i:i + 1]
            s_new = drow_ref[i:i + 1, hs] * s_ref[i, h] + bcol * xdt_ref[i:i + 1, hs]
            so_ref[i, h] = s_new
            o_scr[i:i + 1, hs] = jnp.sum(ccol * s_new, axis=0, keepdims=True)
    o = o_scr[...] + dsk_ref[...] * xs_ref[...]
    y_ref[...] = _rms(o * _silu(z_ref[...]), norm_ref[...])


def _ssd_dec(ct, state, xs, xdt, drow, proj, dsk_exp, norm):
    n = proj.shape[0]
    tb = DEC_TOKENS
    sspec = pl.BlockSpec((tb, SSM_HEADS, SSM_STATE, SSM_HEADDIM), lambda g: (g, 0, 0, 0))
    rows = lambda j: pl.BlockSpec((tb, 1024), lambda g: (g, j))
    vec = pl.BlockSpec((1, 1024), lambda g: (0, 0))
    return pl.pallas_call(
        _ssd_dec_kernel,
        grid=(n // tb,),
        in_specs=[pl.BlockSpec((1, 2 * SSM_GROUPS * SSM_STATE, tb), lambda g: (g, 0, 0)), sspec,
                  rows(0), rows(0), rows(0), rows(7), vec, vec],
        out_specs=[rows(0), sspec],
        out_shape=[jax.ShapeDtypeStruct((n, 1024), F32), jax.ShapeDtypeStruct(state.shape, F32)],
        scratch_shapes=[pltpu.VMEM((tb, 1024), F32)],
        compiler_params=_cparams(("parallel",)),
        name="ssd_dec",
    )(ct, state, xs, xdt, drow, proj, dsk_exp, norm)


def kernel(x_prompt, x_sample, state_ret, state_hgrn, state_ssm, state_conv, w_in, ret_norm, hg_norm,
           hg_lb_logits, conv_w, conv_b, dt_bias, a_log, d_skip, ssm_norm, w_branch, w_out, norm_mix_pre,
           norm_mix_post, norm_ffn_pre, norm_ffn_post, w_up, w_down):
    bp, lp, _ = x_prompt.shape
    bs, ls, _ = x_sample.shape
    assert ls == 1 and lp % CHUNK_ROWS == 0 and bs % DEC_TOKENS == 0
    depth = w_in.shape[0]

    lb_w = jax.nn.softmax(hg_lb_logits.astype(F32), axis=0)
    lb_all = jnp.cumsum(lb_w, axis=0) - lb_w[0]

    hp = x_prompt.reshape(bp * lp, D_MODEL)
    hs = x_sample.reshape(bs, D_MODEL)
    outs = [[] for _ in range(8)]
    vec = lambda a: a.reshape(1, -1).astype(F32)
    for l in range(depth):
        w_main = jnp.concatenate([w_in[l, :, :N_MAIN], w_in[l, :, N_MAIN + SSM_HEADS:]], axis=1).astype(BF16)
        w_dt = jnp.pad(w_in[l, :, N_MAIN:N_MAIN + SSM_HEADS], ((0, 0), (0, DT_PAD - SSM_HEADS))).astype(BF16)
        wb, wo = w_branch[l].astype(BF16), w_out[l].astype(BF16)
        wu, wd = w_up[l].astype(BF16), w_down[l].astype(BF16)
        dsk_exp = jnp.repeat(d_skip[l].astype(F32), SSM_HEADDIM).reshape(1, SSM_INNER)
        lb = vec(lb_all[l])

        proj, dt = _in_proj(hp, vec(norm_mix_pre[l]), w_main, w_dt, tm=1024)
        yr, s_ret = _ret_prompt(proj, vec(ret_norm[l]), bp, lp)
        yh, s_hg = _hgrn_prompt(proj, lb, vec(hg_norm[l]), bp, lp)
        ym, s_ssm, s_conv = _ssd_prompt(proj, dt, conv_w[l], conv_b[l], dt_bias[l], a_log[l], dsk_exp,
                                        vec(ssm_norm[l]), bp, lp)
        hp = _merge(hp, yr, yh, ym, proj, wb, wo, vec(norm_mix_post[l]), tm=512)
        hp = _ffn(hp, vec(norm_ffn_pre[l]), vec(norm_ffn_post[l]), wu, wd, tm=1024)
        for lst, val in zip(outs[:4], (s_ret, s_hg, s_ssm, s_conv)):
            lst.append(val)

        proj, dt = _in_proj(hs, vec(norm_mix_pre[l]), w_main, w_dt, tm=bs)
        c_ret, c_hg, c_ssd, xs, xdt, drow, conv_new = _dec_prep(
            proj, dt, state_conv[l].transpose(1, 0, 2), lb, conv_w[l], conv_b[l], dt_bias[l], a_log[l])
        yr, s_ret = _ret_dec(_cols(c_ret), state_ret[l], proj, vec(ret_norm[l]))
        yh, s_hg = _hgrn_dec(_cols(c_hg), state_hgrn[l], proj, vec(hg_norm[l]))
        ym, s_ssm = _ssd_dec(_cols(c_ssd), state_ssm[l], xs, xdt, drow, proj, dsk_exp, vec(ssm_norm[l]))
        hs = _merge(hs, yr, yh, ym, proj, wb, wo, vec(norm_mix_post[l]), tm=bs)
        hs = _ffn(hs, vec(norm_ffn_pre[l]), vec(norm_ffn_post[l]), wu, wd, tm=bs)
        for lst, val in zip(outs[4:], (s_ret, s_hg, s_ssm, conv_new.transpose(1, 0, 2))):
            lst.append(val)

    stacked = [jnp.stack(o) for o in outs]
    return (hp.reshape(bp, lp, D_MODEL), hs.reshape(bs, ls, D_MODEL), *stacked)
```

```python
import functools
import math

import numpy as np
import jax
import jax.numpy as jnp
from jax import lax
from jax.experimental import pallas as pl
from jax.experimental.pallas import tpu as pltpu

F32 = jnp.float32
BF16 = jnp.bfloat16

D_MODEL = 1024
PAST_LEN = 16384
RET_HEADS, RET_DK, RET_DV = 4, 128, 256
ROPE_BASE = 10000.0
HG_HEADS, HG_EXPAND, HG_DV = 8, 128, 128
SSM_HEADS, SSM_HEADDIM, SSM_GROUPS, SSM_STATE, SSM_CONV = 16, 64, 4, 128, 4
SSM_INNER = SSM_HEADS * SSM_HEADDIM
SSM_CONV_CH = SSM_INNER + 2 * SSM_GROUPS * SSM_STATE
D_FF = 4 * D_MODEL
EPS = 1e-6
HG_CLIP_MIN = 1e-6

N_MAIN = 10240
N_PROJ = N_MAIN + 3 * D_MODEL
DT_PAD = 128

LANES = 128
SUBLANES = 8
VMEM_LIMIT = 56 * 1024 * 1024

CHUNK_ROWS = 256
DEC_TOKENS = 8


def _cparams(sem):
    return pltpu.CompilerParams(dimension_semantics=sem, vmem_limit_bytes=VMEM_LIMIT)


def _dot(a, b):
    return jnp.dot(a, b, preferred_element_type=F32)


def _dot_nt(a, b):
    return lax.dot_general(a, b, (((1,), (1,)), ((), ())), preferred_element_type=F32)


def _dot_tn(a, b):
    return lax.dot_general(a, b, (((0,), (0,)), ((), ())), preferred_element_type=F32)


def _split3(a):
    hi = a.astype(BF16)
    r1 = a - hi.astype(F32)
    mid = r1.astype(BF16)
    lo = (r1 - mid.astype(F32)).astype(BF16)
    return hi, mid, lo


def _dot3_left(m01, a):
    hi, mid, lo = _split3(a)
    return _dot(m01, hi) + _dot(m01, mid) + _dot(m01, lo)


def _dot3_right(a, m01):
    hi, mid, lo = _split3(a)
    return _dot(hi, m01) + _dot(mid, m01) + _dot(lo, m01)


def _sigmoid(x):
    return 1.0 / (1.0 + jnp.exp(-x))


def _silu(x):
    return x * _sigmoid(x)


def _softplus(x):
    return jnp.maximum(x, 0.0) + jnp.log1p(jnp.exp(-jnp.abs(x)))


def _rms(x, gain):
    return x * lax.rsqrt(jnp.mean(x * x, axis=-1, keepdims=True) + EPS) * gain


def _in_proj_kernel(x_ref, g_ref, w_ref, wdt_ref, o_ref, dt_ref, xn_scr):
    @pl.when(pl.program_id(1) == 0)
    def _():
        xn_scr[...] = _rms(x_ref[...], g_ref[...]).astype(BF16)
        dt_ref[...] = _dot(xn_scr[...], wdt_ref[...])

    o_ref[...] = _dot(xn_scr[...], w_ref[...]).astype(o_ref.dtype)


def _in_proj(x, gain, w_main, w_dt, tm, out_dtype):
    t = x.shape[0]
    tn = 1024
    return pl.pallas_call(
        _in_proj_kernel,
        grid=(t // tm, N_PROJ // tn),
        in_specs=[
            pl.BlockSpec((tm, D_MODEL), lambda i, j: (i, 0)),
            pl.BlockSpec((1, D_MODEL), lambda i, j: (0, 0)),
            pl.BlockSpec((D_MODEL, tn), lambda i, j: (0, j)),
            pl.BlockSpec((D_MODEL, DT_PAD), lambda i, j: (0, 0)),
        ],
        out_specs=[
            pl.BlockSpec((tm, tn), lambda i, j: (i, j)),
            pl.BlockSpec((tm, DT_PAD), lambda i, j: (i, 0)),
        ],
        out_shape=[jax.ShapeDtypeStruct((t, N_PROJ), out_dtype), jax.ShapeDtypeStruct((t, DT_PAD), F32)],
        scratch_shapes=[pltpu.VMEM((tm, D_MODEL), BF16)],
        compiler_params=_cparams(("parallel", "arbitrary")),
        name="in_proj",
    )(x, gain, w_main, w_dt)


def _ret_log_gamma():
    return np.log1p(-np.exp2(-5.0 - np.arange(RET_HEADS, dtype=np.float64)))


def _ret_consts(r):
    lg = _ret_log_gamma()
    i = np.arange(r)
    diff = (i[:, None] - i[None, :]).astype(np.float64)
    dec = np.where(diff >= 0, np.exp(lg[:, None, None] * np.maximum(diff, 0.0)), 0.0)
    qd = np.exp(lg[:, None] * (i + 1.0))
    kd = np.exp(lg[:, None] * (r - 1.0 - i))
    gl = tuple(float(v) for v in np.exp(lg * r))
    bc = lambda a: jnp.asarray(np.broadcast_to(a[:, :, None], (RET_HEADS, r, LANES)), F32)
    return jnp.asarray(dec, F32), bc(qd), bc(kd), gl


def _rope_tables(pos):
    half = RET_DK // 2
    inv = ROPE_BASE ** (-(np.arange(half, dtype=np.float64) / half))
    ang = pos.astype(np.float64)[:, None] * inv[None, :]
    cos, sin = np.cos(ang), np.sin(ang)
    cos2 = np.concatenate([cos, cos], axis=1)
    sin2 = np.concatenate([-sin, sin], axis=1)
    scale = RET_DK ** -0.5
    return (jnp.asarray(cos2, F32), jnp.asarray(sin2, F32),
            jnp.asarray(cos2 * scale, F32), jnp.asarray(sin2 * scale, F32))


def _rope(t, cos2, sin2):
    return t * cos2 + pltpu.roll(t, RET_DK // 2, 1) * sin2


def _head_layernorm(o, gain):
    oc = o - jnp.mean(o, axis=-1, keepdims=True)
    return oc * lax.rsqrt(jnp.mean(oc * oc, axis=-1, keepdims=True) + EPS) * gain


def _ret_kernel(rq_ref, rk_ref, rv_ref, rg_ref, cq_ref, sq_ref, ck_ref, sk_ref, dec_ref, qd_ref, kd_ref,
                gain_ref, y_ref, sout_ref, s_scr, *, gl):
    c = pl.program_id(1)

    @pl.when(c == 0)
    def _():
        s_scr[...] = jnp.zeros_like(s_scr)

    cq, sq, ck, sk = cq_ref[...], sq_ref[...], ck_ref[...], sk_ref[...]
    for h in range(RET_HEADS):
        ks = slice(h * RET_DK, (h + 1) * RET_DK)
        vs = slice(h * RET_DV, (h + 1) * RET_DV)
        q = _rope(rq_ref[:, ks].astype(F32), cq, sq)
        k = _rope(rk_ref[:, ks].astype(F32), ck, sk)
        v = rv_ref[:, vs].astype(BF16)
        att = (_dot_nt(q.astype(BF16), k.astype(BF16)) * dec_ref[h]).astype(BF16)
        s_old = s_scr[h]
        o = _dot(att, v) + _dot((q * qd_ref[h]).astype(BF16), s_old.astype(BF16))
        s_scr[h] = gl[h] * s_old + _dot_tn((k * kd_ref[h]).astype(BF16), v)
        y = _silu(rg_ref[:, vs].astype(F32)) * _head_layernorm(o, gain_ref[:, vs])
        y_ref[:, vs] = y.astype(y_ref.dtype)

    @pl.when(c == pl.num_programs(1) - 1)
    def _():
        sout_ref[0] = s_scr[...]


def _ret_prompt(proj, gain, batch, seq):
    r = CHUNK_ROWS
    nc = seq // r
    dec, qd, kd, gl = _ret_consts(r)
    cq, sq, ck, sk = _rope_tables(np.arange(seq))
    row = lambda b, c: b * nc + c
    tab = pl.BlockSpec((r, RET_DK), lambda b, c: (c, 0))
    const3 = lambda shape: pl.BlockSpec(shape, lambda b, c: (0, 0, 0))
    return pl.pallas_call(
        functools.partial(_ret_kernel, gl=gl),
        grid=(batch, nc),
        in_specs=[
            pl.BlockSpec((r, 512), lambda b, c: (row(b, c), 0)),
            pl.BlockSpec((r, 512), lambda b, c: (row(b, c), 1)),
            pl.BlockSpec((r, 1024), lambda b, c: (row(b, c), 1)),
            pl.BlockSpec((r, 1024), lambda b, c: (row(b, c), 2)),
            tab, tab, tab, tab,
            const3((RET_HEADS, r, r)), const3((RET_HEADS, r, LANES)), const3((RET_HEADS, r, LANES)),
            pl.BlockSpec((1, 1024), lambda b, c: (0, 0)),
        ],
        out_specs=[
            pl.BlockSpec((r, 1024), lambda b, c: (row(b, c), 0)),
            pl.BlockSpec((1, RET_HEADS, RET_DK, RET_DV), lambda b, c: (b, 0, 0, 0)),
        ],
        out_shape=[jax.ShapeDtypeStruct((batch * seq, 1024), BF16),
                   jax.ShapeDtypeStruct((batch, RET_HEADS, RET_DK, RET_DV), F32)],
        scratch_shapes=[pltpu.VMEM((RET_HEADS, RET_DK, RET_DV), F32)],
        compiler_params=_cparams(("parallel", "arbitrary")),
        name="ret_prompt",
    )(proj, proj, proj, proj, cq, sq, ck, sk, dec, qd, kd, gain)


HG_DIAG = 4


def _hg_levels(r):
    levels, b = [], r // 2
    while b >= HG_DIAG:
        levels.append(b)
        b //= 2
    return levels


def _hg_masks(r):
    i = np.arange(r)
    out = []
    for b in _hg_levels(r):
        same = (i[:, None] // (2 * b)) == (i[None, :] // (2 * b))
        out.append(same & ((i[:, None] & b) != 0) & ((i[None, :] & b) == 0))
    out.append(((i[:, None] // HG_DIAG) == (i[None, :] // HG_DIAG)) & (i[:, None] >= i[None, :]))
    return jnp.asarray(np.stack(out).astype(np.float32))


def _tri_ones(r):
    i = np.arange(r)
    return jnp.asarray((i[:, None] >= i[None, :]).astype(np.float32), BF16)


def _vreg_row_bcast(g3, j):
    return jnp.broadcast_to(g3[:, j:j + 1, :], g3.shape)


def _hgrn_kernel(hq_ref, hf_ref, hi_ref, hg_ref, lb_ref, gain_ref, tri_ref, mask_ref, y_ref, sout_ref,
                 st_scr, *, r):
    c = pl.program_id(1)

    @pl.when(c == 0)
    def _():
        st_scr[...] = jnp.zeros_like(st_scr)

    nt = r // SUBLANES
    row = lax.broadcasted_iota(jnp.int32, (r, LANES), 0)
    tri = tri_ref[...]
    levels = _hg_levels(r)
    for h in range(HG_HEADS):
        sl = slice(h * LANES, (h + 1) * LANES)
        q = _silu(hq_ref[:, sl].astype(F32))
        lb = lb_ref[:, sl]
        f = jnp.clip(lb + (1.0 - lb) * _sigmoid(hf_ref[:, sl].astype(F32)), HG_CLIP_MIN, 1.0)
        kk = 1.0 - f
        g_cum = _dot3_left(tri, jnp.log(f))
        g_last = g_cum[r - 1:r, :]
        g3 = g_cum.reshape(nt, SUBLANES, LANES)
        tile_last = _vreg_row_bcast(g3, SUBLANES - 1)

        att = jnp.zeros((r, r), F32)
        for lvl, b in enumerate(levels):
            if b >= SUBLANES:
                per = 2 * b // SUBLANES
                t4 = tile_last.reshape(nt // per, per, SUBLANES, LANES)
                ref = jnp.broadcast_to(t4[:, per // 2 - 1:per // 2], t4.shape).reshape(r, LANES)
            else:
                ref = _vreg_row_bcast(g3, b - 1).reshape(r, LANES)
            e = jnp.exp(-jnp.abs(g_cum - ref))
            upper = (row & b) != 0
            x = jnp.where(upper, q, kk) * e
            qb = jnp.where(upper, x, 0.0).astype(BF16)
            kb = jnp.where(upper, 0.0, x).astype(BF16)
            att = att + _dot_nt(qb, kb) * mask_ref[lvl]
        ref = jnp.where((row & HG_DIAG) != 0, _vreg_row_bcast(g3, HG_DIAG).reshape(r, LANES),
                        _vreg_row_bcast(g3, 0).reshape(r, LANES))
        qb = (q * jnp.exp(g_cum - ref)).astype(BF16)
        kb = (kk * jnp.exp(ref - g_cum)).astype(BF16)
        att = att + _dot_nt(qb, kb) * mask_ref[len(levels)]

        v = hi_ref[:, sl].astype(BF16)
        st_old = st_scr[h]
        o = _dot(att.astype(BF16), v) + _dot_nt((q * jnp.exp(g_cum)).astype(BF16), st_old.astype(BF16))
        st_scr[h] = st_old * jnp.exp(g_last) + _dot_tn(v, (kk * jnp.exp(g_last - g_cum)).astype(BF16))
        o = o * lax.rsqrt(jnp.mean(o * o, axis=-1, keepdims=True) + EPS) * gain_ref[:, sl]
        y_ref[:, sl] = (o * _sigmoid(hg_ref[:, sl].astype(F32))).astype(y_ref.dtype)

    @pl.when(c == pl.num_programs(1) - 1)
    def _():
        for h in range(HG_HEADS):
            sout_ref[0, h] = st_scr[h].T


def _hgrn_prompt(proj, lb, gain, batch, seq):
    r = CHUNK_ROWS
    nc = seq // r
    masks = _hg_masks(r)
    row = lambda b, c: b * nc + c
    blk = lambda j: pl.BlockSpec((r, 1024), lambda b, c: (row(b, c), j))
    vec = pl.BlockSpec((1, 1024), lambda b, c: (0, 0))
    return pl.pallas_call(
        functools.partial(_hgrn_kernel, r=r),
        grid=(batch, nc),
        in_specs=[blk(3), blk(4), blk(5), blk(6), vec, vec,
                  pl.BlockSpec((r, r), lambda b, c: (0, 0)),
                  pl.BlockSpec(masks.shape, lambda b, c: (0, 0, 0))],
        out_specs=[
            pl.BlockSpec((r, 1024), lambda b, c: (row(b, c), 0)),
            pl.BlockSpec((1, HG_HEADS, HG_EXPAND, HG_DV), lambda b, c: (b, 0, 0, 0)),
        ],
        out_shape=[jax.ShapeDtypeStruct((batch * seq, 1024), BF16),
                   jax.ShapeDtypeStruct((batch, HG_HEADS, HG_EXPAND, HG_DV), F32)],
        scratch_shapes=[pltpu.VMEM((HG_HEADS, HG_DV, HG_EXPAND), F32)],
        compiler_params=_cparams(("parallel", "arbitrary")),
        name="hgrn_prompt",
    )(proj, proj, proj, proj, lb, gain, _tri_ones(r), masks)


HEADS_PER_GROUP = SSM_HEADS // SSM_GROUPS
GROUP_W = HEADS_PER_GROUP * SSM_HEADDIM
CONV_PAD = SUBLANES


def _head_expand(vals, h0, lane_head):
    out = jnp.zeros((vals.shape[0], GROUP_W), F32)
    for j in range(HEADS_PER_GROUP):
        out = jnp.where(lane_head == j, vals[:, h0 + j:h0 + j + 1], out)
    return out


def _ssd_kernel(z_ref, xbc_ref, dt_ref, cw_ref, cb_ref, dtb_ref, alog_ref, dsk_ref, norm_ref, tri_ref,
                y_ref, sout_ref, cout_ref, ext_scr, s_scr, o_scr, *, r):
    c = pl.program_id(1)
    last = pl.num_programs(1) - 1

    @pl.when(c == 0)
    def _():
        ext_scr[0:CONV_PAD, :] = jnp.zeros((CONV_PAD, SSM_CONV_CH), F32)
        s_scr[...] = jnp.zeros_like(s_scr)

    ext_scr[CONV_PAD:CONV_PAD + r, :] = xbc_ref[...].astype(F32)
    conv = cb_ref[...]
    for w in range(SSM_CONV):
        off = CONV_PAD - (SSM_CONV - 1) + w
        conv = conv + cw_ref[w:w + 1, :] * ext_scr[off:off + r, :]

    @pl.when(c == last)
    def _():
        cout_ref[0] = ext_scr[CONV_PAD + r - (SSM_CONV - 1):CONV_PAD + r, :]

    ext_scr[0:CONV_PAD, :] = ext_scr[r:r + CONV_PAD, :]

    xc = _silu(conv)
    xs = xc[:, :SSM_INNER]
    bm = xc[:, SSM_INNER:SSM_INNER + SSM_GROUPS * SSM_STATE]
    cm = xc[:, SSM_INNER + SSM_GROUPS * SSM_STATE:]

    dt = _softplus(dt_ref[...] + dtb_ref[...])
    g_cum = _dot3_left(tri_ref[...], dt * (-jnp.exp(alog_ref[...])))
    g_last = g_cum[r - 1:r, :]
    g_cum_t = g_cum.T
    dt_t = dt.T
    e_cum = jnp.exp(g_cum)
    w_in = dt * jnp.exp(g_last - g_cum)
    e_last = jnp.exp(g_last)

    ri = lax.broadcasted_iota(jnp.int32, (r, r), 0)
    ci = lax.broadcasted_iota(jnp.int32, (r, r), 1)
    causal = ri >= ci
    lane_head = lax.broadcasted_iota(jnp.int32, (r, GROUP_W), 1) // SSM_HEADDIM
    lane_head1 = lax.broadcasted_iota(jnp.int32, (1, GROUP_W), 1) // SSM_HEADDIM

    for g in range(SSM_GROUPS):
        gs = slice(g * SSM_STATE, (g + 1) * SSM_STATE)
        ws = slice(g * GROUP_W, (g + 1) * GROUP_W)
        cg = cm[:, gs].astype(BF16)
        bg = bm[:, gs].astype(BF16)
        cb = _dot_nt(cg, bg)
        xs_g = xs[:, ws]
        s_old = s_scr[g]
        h0 = g * HEADS_PER_GROUP
        o_g = _head_expand(e_cum, h0, lane_head) * _dot(cg, s_old.astype(BF16))
        for j in range(HEADS_PER_GROUP):
            h = h0 + j
            diff = g_cum[:, h:h + 1] - g_cum_t[h:h + 1, :]
            dec = jnp.where(causal, jnp.exp(jnp.where(causal, diff, 0.0)), 0.0)
            att = (cb * dec * dt_t[h:h + 1, :]).astype(BF16)
            o_g = o_g + _dot(att, jnp.where(lane_head == j, xs_g, 0.0).astype(BF16))
        s_scr[g] = (s_old * _head_expand(e_last, h0, lane_head1)
                    + _dot_tn(bg, (xs_g * _head_expand(w_in, h0, lane_head)).astype(BF16)))
        o_scr[:, ws] = o_g + dsk_ref[:, ws] * xs_g

    y_ref[...] = _rms(o_scr[...] * _silu(z_ref[...].astype(F32)), norm_ref[...]).astype(y_ref.dtype)

    @pl.when(c == last)
    def _():
        for h in range(SSM_HEADS):
            j = h % HEADS_PER_GROUP
            sout_ref[0, h] = s_scr[h // HEADS_PER_GROUP][:, j * SSM_HEADDIM:(j + 1) * SSM_HEADDIM]


def _pad_lanes(v):
    return jnp.pad(v.astype(F32), (0, LANES - v.shape[0])).reshape(1, LANES)


def _ssd_prompt(proj, dt, conv_w, conv_b, dt_bias, a_log, dsk_exp, norm, batch, seq):
    r = CHUNK_ROWS
    nc = seq // r
    row = lambda b, c: b * nc + c
    vec = lambda n: pl.BlockSpec((1, n), lambda b, c: (0, 0))
    return pl.pallas_call(
        functools.partial(_ssd_kernel, r=r),
        grid=(batch, nc),
        in_specs=[
            pl.BlockSpec((r, 1024), lambda b, c: (row(b, c), 7)),
            pl.BlockSpec((r, SSM_CONV_CH), lambda b, c: (row(b, c), 4)),
            pl.BlockSpec((r, DT_PAD), lambda b, c: (row(b, c), 0)),
            pl.BlockSpec((SSM_CONV, SSM_CONV_CH), lambda b, c: (0, 0)),
            vec(SSM_CONV_CH), vec(LANES), vec(LANES), vec(SSM_INNER), vec(SSM_INNER),
            pl.BlockSpec((r, r), lambda b, c: (0, 0)),
        ],
        out_specs=[
            pl.BlockSpec((r, 1024), lambda b, c: (row(b, c), 0)),
            pl.BlockSpec((1, SSM_HEADS, SSM_STATE, SSM_HEADDIM), lambda b, c: (b, 0, 0, 0)),
            pl.BlockSpec((1, SSM_CONV - 1, SSM_CONV_CH), lambda b, c: (b, 0, 0)),
        ],
        out_shape=[jax.ShapeDtypeStruct((batch * seq, 1024), BF16),
                   jax.ShapeDtypeStruct((batch, SSM_HEADS, SSM_STATE, SSM_HEADDIM), F32),
                   jax.ShapeDtypeStruct((batch, SSM_CONV - 1, SSM_CONV_CH), F32)],
        scratch_shapes=[pltpu.VMEM((CONV_PAD + r, SSM_CONV_CH), F32),
                        pltpu.VMEM((SSM_GROUPS, SSM_STATE, GROUP_W), F32),
                        pltpu.VMEM((r, SSM_INNER), F32)],
        compiler_params=_cparams(("parallel", "arbitrary")),
        name="ssd_prompt",
    )(proj, proj, dt, conv_w, conv_b.reshape(1, -1), _pad_lanes(dt_bias), _pad_lanes(a_log), dsk_exp, norm,
      _tri_ones(r))


def _merge_kernel(x_ref, yr_ref, yh_ref, ym_ref, g0_ref, g1_ref, g2_ref, wb_ref, wo_ref, npost_ref, o_ref):
    merged = _sigmoid(g0_ref[...].astype(F32)) * _dot(yr_ref[...].astype(BF16), wb_ref[0])
    merged = merged + _sigmoid(g1_ref[...].astype(F32)) * _dot(yh_ref[...].astype(BF16), wb_ref[1])
    merged = merged + _sigmoid(g2_ref[...].astype(F32)) * _dot(ym_ref[...].astype(BF16), wb_ref[2])
    m = _dot(merged.astype(BF16), wo_ref[...])
    o_ref[...] = x_ref[...] + _rms(m, npost_ref[...])


def _merge(x, yr, yh, ym, proj, wb, wo, npost, tm):
    t = x.shape[0]
    rows = lambda j: pl.BlockSpec((tm, D_MODEL), lambda i: (i, j))
    return pl.pallas_call(
        _merge_kernel,
        grid=(t // tm,),
        in_specs=[rows(0), rows(0), rows(0), rows(0), rows(10), rows(11), rows(12),
                  pl.BlockSpec((3, D_MODEL, D_MODEL), lambda i: (0, 0, 0)),
                  pl.BlockSpec((D_MODEL, D_MODEL), lambda i: (0, 0)),
                  pl.BlockSpec((1, D_MODEL), lambda i: (0, 0))],
        out_specs=rows(0),
        out_shape=jax.ShapeDtypeStruct((t, D_MODEL), F32),
        compiler_params=_cparams(("parallel",)),
        name="merge",
    )(x, yr, yh, ym, proj, proj, proj, wb, wo, npost)


def _ffn_kernel(x_ref, npre_ref, npost_ref, wu_ref, wd_ref, o_ref, xn_scr, acc_scr):
    j = pl.program_id(1)

    @pl.when(j == 0)
    def _():
        xn_scr[...] = _rms(x_ref[...], npre_ref[...]).astype(BF16)
        acc_scr[...] = jnp.zeros_like(acc_scr)

    hid = jnp.square(jnp.maximum(_dot(xn_scr[...], wu_ref[...]), 0.0))
    acc_scr[...] += _dot(hid.astype(BF16), wd_ref[...])

    @pl.when(j == pl.num_programs(1) - 1)
    def _():
        o_ref[...] = x_ref[...] + _rms(acc_scr[...], npost_ref[...])


def _ffn(x, npre, npost, wu, wd, tm):
    t = x.shape[0]
    tf = 1024
    vec = pl.BlockSpec((1, D_MODEL), lambda i, j: (0, 0))
    return pl.pallas_call(
        _ffn_kernel,
        grid=(t // tm, D_FF // tf),
        in_specs=[pl.BlockSpec((tm, D_MODEL), lambda i, j: (i, 0)), vec, vec,
                  pl.BlockSpec((D_MODEL, tf), lambda i, j: (0, j)),
                  pl.BlockSpec((tf, D_MODEL), lambda i, j: (j, 0))],
        out_specs=pl.BlockSpec((tm, D_MODEL), lambda i, j: (i, 0)),
        out_shape=jax.ShapeDtypeStruct((t, D_MODEL), F32),
        scratch_shapes=[pltpu.VMEM((tm, D_MODEL), BF16), pltpu.VMEM((tm, D_MODEL), F32)],
        compiler_params=_cparams(("parallel", "arbitrary")),
        name="ffn",
    )(x, npre, npost, wu, wd)


def _dec_prep_kernel(proj_ref, dt_ref, cst_ref, cq_ref, sq_ref, ck_ref, sk_ref, lb_ref, cw_ref, cb_ref,
                     dtb_ref, alog_ref, hexp_ref,
                     cret_ref, chg_ref, cssd_ref, xs_ref, xdt_ref, drow_ref, cnew_ref):
    n = proj_ref.shape[0]
    cq, sq, ck, sk = cq_ref[...], sq_ref[...], ck_ref[...], sk_ref[...]
    for h in range(RET_HEADS):
        ks = slice(h * RET_DK, (h + 1) * RET_DK)
        cret_ref[:, ks] = _rope(proj_ref[:, ks], cq, sq)
        k_off = RET_HEADS * RET_DK
        cret_ref[:, k_off + h * RET_DK:k_off + (h + 1) * RET_DK] = _rope(
            proj_ref[:, k_off + h * RET_DK:k_off + (h + 1) * RET_DK], ck, sk)

    lb = lb_ref[...]
    chg_ref[:, 0:1024] = _silu(proj_ref[:, 3072:4096])
    chg_ref[:, 1024:2048] = jnp.clip(lb + (1.0 - lb) * _sigmoid(proj_ref[:, 4096:5120]), HG_CLIP_MIN, 1.0)

    u = proj_ref[:, 8192:8192 + SSM_CONV_CH]
    conv = cb_ref[...] + cw_ref[SSM_CONV - 1:SSM_CONV, :] * u
    for w in range(SSM_CONV - 1):
        conv = conv + cw_ref[w:w + 1, :] * cst_ref[w]
    cnew_ref[0] = cst_ref[1]
    cnew_ref[1] = cst_ref[2]
    cnew_ref[2] = u
    xc = _silu(conv)
    xs = xc[:, :SSM_INNER]
    xs_ref[...] = xs
    cssd_ref[:, 0:512] = xc[:, SSM_INNER + 512:]
    cssd_ref[:, 512:1024] = xc[:, SSM_INNER:SSM_INNER + 512]
    dt = _softplus(dt_ref[...] + dtb_ref[...])
    decay = jnp.exp(dt * (-jnp.exp(alog_ref[...])))
    hexp = hexp_ref[...]
    xdt_ref[...] = xs * _dot3_right(dt, hexp)
    drow_ref[...] = _dot3_right(decay, hexp)
    del n


def _dec_prep(proj, dt, conv_state_t, lb, conv_w, conv_b, dt_bias, a_log):
    n = proj.shape[0]
    cq, sq, ck, sk = _rope_tables(np.full((n,), PAST_LEN))
    hexp = np.zeros((LANES, SSM_INNER), np.float32)
    for h in range(SSM_HEADS):
        hexp[h, h * SSM_HEADDIM:(h + 1) * SSM_HEADDIM] = 1.0
    outs = [(n, 1024), (n, 2048), (n, 1024), (n, 1024), (n, 1024), (n, 1024)]
    return pl.pallas_call(
        _dec_prep_kernel,
        out_shape=[jax.ShapeDtypeStruct(s, F32) for s in outs]
        + [jax.ShapeDtypeStruct((SSM_CONV - 1, n, SSM_CONV_CH), F32)],
        compiler_params=pltpu.CompilerParams(vmem_limit_bytes=VMEM_LIMIT),
        name="dec_prep",
    )(proj, dt, conv_state_t, cq, sq, ck, sk, lb, conv_w, conv_b.reshape(1, -1), _pad_lanes(dt_bias),
      _pad_lanes(a_log), jnp.asarray(hexp, BF16))


def _cols(a, tb):
    n, ch = a.shape
    return a.reshape(n // tb, tb, ch).transpose(0, 2, 1)


def _state_update_call(body, name, layer, state, carried, n_tokens, heads_per_step, in_specs, operands,
                       scratch):
    depth, _, heads, sk, sv = state.shape
    tb = DEC_TOKENS
    if carried is None:
        src = state
        sspec = pl.BlockSpec((depth, tb, heads_per_step, sk, sv), lambda t, j: (0, t, j, 0, 0))
        own, keep, aliases = layer, tuple(d for d in range(depth) if d != layer), {}
    else:
        src = carried
        sspec = pl.BlockSpec((1, tb, heads_per_step, sk, sv), lambda t, j: (layer, t, j, 0, 0))
        own, keep, aliases = 0, (), {0: 1}
    return pl.pallas_call(
        functools.partial(body, own=own, keep=keep),
        grid=(n_tokens // tb, heads // heads_per_step),
        in_specs=[sspec] + in_specs,
        out_specs=[pl.BlockSpec((tb, 1024), lambda t, j: (t, 0)), sspec],
        out_shape=[jax.ShapeDtypeStruct((n_tokens, 1024), F32), jax.ShapeDtypeStruct(state.shape, F32)],
        input_output_aliases=aliases,
        scratch_shapes=[scratch],
        compiler_params=_cparams(("parallel", "arbitrary")),
        name=name,
    )(src, *operands)


def _ret_dec_kernel(s_ref, ct_ref, v_ref, rg_ref, gain_ref, y_ref, so_ref, o_scr, *, gamma, own, keep):
    for i in range(DEC_TOKENS):
        for h in range(RET_HEADS):
            qcol = ct_ref[0, h * RET_DK:(h + 1) * RET_DK, i:i + 1]
            kcol = ct_ref[0, (RET_HEADS + h) * RET_DK:(RET_HEADS + h + 1) * RET_DK, i:i + 1]
            vs = slice(h * RET_DV, (h + 1) * RET_DV)
            s_new = gamma[h] * s_ref[own, i, h] + kcol * v_ref[i:i + 1, vs]
            so_ref[own, i, h] = s_new
            o_scr[i:i + 1, vs] = jnp.sum(qcol * s_new, axis=0, keepdims=True)
    for j in keep:
        so_ref[j] = s_ref[j]
    for h in range(RET_HEADS):
        vs = slice(h * RET_DV, (h + 1) * RET_DV)
        y_ref[:, vs] = _silu(rg_ref[:, vs]) * _head_layernorm(o_scr[:, vs], gain_ref[:, vs])


def _ret_dec(layer, state, carried, c_ret, proj, gain):
    tb = DEC_TOKENS
    gamma = tuple(float(v) for v in np.exp(_ret_log_gamma()))
    return _state_update_call(
        functools.partial(_ret_dec_kernel, gamma=gamma), "ret_dec", layer, state, carried, proj.shape[0],
        RET_HEADS,
        [pl.BlockSpec((1, 2 * RET_HEADS * RET_DK, tb), lambda t, j: (t, 0, 0)),
         pl.BlockSpec((tb, 1024), lambda t, j: (t, 1)), pl.BlockSpec((tb, 1024), lambda t, j: (t, 2)),
         pl.BlockSpec((1, 1024), lambda t, j: (0, 0))],
        (_cols(c_ret, tb), proj, proj, gain), pltpu.VMEM((tb, 1024), F32))


def _hgrn_dec_kernel(s_ref, ct_ref, v_ref, hg_ref, gain_ref, y_ref, so_ref, o_scr, *, own, keep):
    for i in range(DEC_TOKENS):
        for h in range(HG_HEADS):
            sl = slice(h * LANES, (h + 1) * LANES)
            qcol = ct_ref[0, h * HG_EXPAND:(h + 1) * HG_EXPAND, i:i + 1]
            fcol = ct_ref[0, (HG_HEADS + h) * HG_EXPAND:(HG_HEADS + h + 1) * HG_EXPAND, i:i + 1]
            s_new = fcol * s_ref[own, i, h] + (1.0 - fcol) * v_ref[i:i + 1, sl]
            so_ref[own, i, h] = s_new
            o_scr[i:i + 1, sl] = jnp.sum(qcol * s_new, axis=0, keepdims=True)
    for j in keep:
        so_ref[j] = s_ref[j]
    for h in range(HG_HEADS):
        sl = slice(h * LANES, (h + 1) * LANES)
        o = o_scr[:, sl]
        o = o * lax.rsqrt(jnp.mean(o * o, axis=-1, keepdims=True) + EPS) * gain_ref[:, sl]
        y_ref[:, sl] = o * _sigmoid(hg_ref[:, sl])


def _hgrn_dec(layer, state, carried, c_hg, proj, gain):
    tb = DEC_TOKENS
    return _state_update_call(
        _hgrn_dec_kernel, "hgrn_dec", layer, state, carried, proj.shape[0], HG_HEADS,
        [pl.BlockSpec((1, 2 * HG_HEADS * HG_EXPAND, tb), lambda t, j: (t, 0, 0)),
         pl.BlockSpec((tb, 1024), lambda t, j: (t, 5)), pl.BlockSpec((tb, 1024), lambda t, j: (t, 6)),
         pl.BlockSpec((1, 1024), lambda t, j: (0, 0))],
        (_cols(c_hg, tb), proj, proj, gain), pltpu.VMEM((tb, 1024), F32))


def _ssd_dec_kernel(s_ref, c_ref, b_ref, xdt_ref, drow_ref, xs_ref, z_ref, dsk_ref, norm_ref,
                    y_ref, so_ref, o_scr, *, own, keep):
    g = pl.program_id(1)
    shape = (SSM_STATE, SSM_HEADDIM)
    for i in range(DEC_TOKENS):
        cmat = jnp.broadcast_to(c_ref[0, :, i:i + 1], shape)
        bmat = jnp.broadcast_to(b_ref[0, :, i:i + 1], shape)
        for j in range(HEADS_PER_GROUP):
            hs = slice(j * SSM_HEADDIM, (j + 1) * SSM_HEADDIM)
            s_new = drow_ref[i:i + 1, hs] * s_ref[own, i, j] + bmat * xdt_ref[i:i + 1, hs]
            so_ref[own, i, j] = s_new
            o_scr[g, i:i + 1, hs] = jnp.sum(cmat * s_new, axis=0, keepdims=True)
    for d in keep:
        so_ref[d] = s_ref[d]

    @pl.when(g == pl.num_programs(1) - 1)
    def _():
        o = jnp.concatenate([o_scr[k] for k in range(SSM_GROUPS)], axis=1) + dsk_ref[...] * xs_ref[...]
        y_ref[...] = _rms(o * _silu(z_ref[...]), norm_ref[...])


def _ssd_dec(layer, state, carried, c_ssd, xs, xdt, drow, proj, dsk_exp, norm):
    tb = DEC_TOKENS
    ct = _cols(c_ssd, tb)
    rows = lambda col: pl.BlockSpec((tb, 1024), lambda t, j: (t, col))
    grp = pl.BlockSpec((tb, GROUP_W), lambda t, j: (t, j))
    vec = pl.BlockSpec((1, 1024), lambda t, j: (0, 0))
    return _state_update_call(
        _ssd_dec_kernel, "ssd_dec", layer, state, carried, proj.shape[0], HEADS_PER_GROUP,
        [pl.BlockSpec((1, SSM_STATE, tb), lambda t, j: (t, j, 0)),
         pl.BlockSpec((1, SSM_STATE, tb), lambda t, j: (t, SSM_GROUPS + j, 0)),
         grp, grp, rows(0), rows(7), vec, vec],
        (ct, ct, xdt, drow, xs, proj, dsk_exp, norm), pltpu.VMEM((SSM_GROUPS, tb, GROUP_W), F32))


def kernel(x_prompt, x_sample, state_ret, state_hgrn, state_ssm, state_conv, w_in, ret_norm, hg_norm,
           hg_lb_logits, conv_w, conv_b, dt_bias, a_log, d_skip, ssm_norm, w_branch, w_out, norm_mix_pre,
           norm_mix_post, norm_ffn_pre, norm_ffn_post, w_up, w_down):
    bp, lp, _ = x_prompt.shape
    bs, ls, _ = x_sample.shape
    assert ls == 1 and lp % CHUNK_ROWS == 0 and bs % DEC_TOKENS == 0
    depth = w_in.shape[0]

    lb_w = jax.nn.softmax(hg_lb_logits.astype(F32), axis=0)
    lb_all = jnp.cumsum(lb_w, axis=0) - lb_w[0]

    hp = x_prompt.reshape(bp * lp, D_MODEL)
    hs = x_sample.reshape(bs, D_MODEL)
    outs = [[] for _ in range(5)]
    new_ret = new_hg = new_ssm = None
    vec = lambda a: a.reshape(1, -1).astype(F32)
    for l in range(depth):
        w_main = jnp.concatenate([w_in[l, :, :N_MAIN], w_in[l, :, N_MAIN + SSM_HEADS:]], axis=1).astype(BF16)
        w_dt = jnp.pad(w_in[l, :, N_MAIN:N_MAIN + SSM_HEADS], ((0, 0), (0, DT_PAD - SSM_HEADS))).astype(BF16)
        wb, wo = w_branch[l].astype(BF16), w_out[l].astype(BF16)
        wu, wd = w_up[l].astype(BF16), w_down[l].astype(BF16)
        dsk_exp = jnp.repeat(d_skip[l].astype(F32), SSM_HEADDIM).reshape(1, SSM_INNER)
        lb = vec(lb_all[l])

        proj, dt = _in_proj(hp, vec(norm_mix_pre[l]), w_main, w_dt, tm=min(2048, bp * lp), out_dtype=BF16)
        yr, s_ret = _ret_prompt(proj, vec(ret_norm[l]), bp, lp)
        yh, s_hg = _hgrn_prompt(proj, lb, vec(hg_norm[l]), bp, lp)
        ym, s_ssm, s_conv = _ssd_prompt(proj, dt, conv_w[l], conv_b[l], dt_bias[l], a_log[l], dsk_exp,
                                        vec(ssm_norm[l]), bp, lp)
        hp = _merge(hp, yr, yh, ym, proj, wb, wo, vec(norm_mix_post[l]), tm=512)
        hp = _ffn(hp, vec(norm_ffn_pre[l]), vec(norm_ffn_post[l]), wu, wd, tm=1024)
        for lst, val in zip(outs[:4], (s_ret, s_hg, s_ssm, s_conv)):
            lst.append(val)

        proj, dt = _in_proj(hs, vec(norm_mix_pre[l]), w_main, w_dt, tm=bs, out_dtype=F32)
        c_ret, c_hg, c_ssd, xs, xdt, drow, conv_new = _dec_prep(
            proj, dt, state_conv[l].transpose(1, 0, 2), lb, conv_w[l], conv_b[l], dt_bias[l], a_log[l])
        yr, new_ret = _ret_dec(l, state_ret, new_ret, c_ret, proj, vec(ret_norm[l]))
        yh, new_hg = _hgrn_dec(l, state_hgrn, new_hg, c_hg, proj, vec(hg_norm[l]))
        ym, new_ssm = _ssd_dec(l, state_ssm, new_ssm, c_ssd, xs, xdt, drow, proj, dsk_exp, vec(ssm_norm[l]))
        hs = _merge(hs, yr, yh, ym, proj, wb, wo, vec(norm_mix_post[l]), tm=bs)
        hs = _ffn(hs, vec(norm_ffn_pre[l]), vec(norm_ffn_post[l]), wu, wd, tm=bs)
        outs[4].append(conv_new.transpose(1, 0, 2))

    p_ret, p_hg, p_ssm, p_conv, s_conv = [jnp.stack(o) for o in outs]
    return (hp.reshape(bp, lp, D_MODEL), hs.reshape(bs, ls, D_MODEL), p_ret, p_hg, p_ssm, p_conv,
            new_ret, new_hg, new_ssm, s_conv)
```

```python
import functools
import math

import numpy as np
import jax
import jax.numpy as jnp
from jax import lax
from jax.experimental import pallas as pl
from jax.experimental.pallas import tpu as pltpu

F32 = jnp.float32
BF16 = jnp.bfloat16

D_MODEL = 1024
PAST_LEN = 16384
RET_HEADS, RET_DK, RET_DV = 4, 128, 256
ROPE_BASE = 10000.0
HG_HEADS, HG_EXPAND, HG_DV = 8, 128, 128
SSM_HEADS, SSM_HEADDIM, SSM_GROUPS, SSM_STATE, SSM_CONV = 16, 64, 4, 128, 4
SSM_INNER = SSM_HEADS * SSM_HEADDIM
SSM_CONV_CH = SSM_INNER + 2 * SSM_GROUPS * SSM_STATE
D_FF = 4 * D_MODEL
EPS = 1e-6
HG_CLIP_MIN = 1e-6

N_MAIN = 10240
N_PROJ = N_MAIN + 3 * D_MODEL
DT_PAD = 128

LANES = 128
SUBLANES = 8
VMEM_LIMIT = 56 * 1024 * 1024

CHUNK_ROWS = 256
DEC_TOKENS = 8


def _cparams(sem):
    return pltpu.CompilerParams(dimension_semantics=sem, vmem_limit_bytes=VMEM_LIMIT)


def _dot(a, b):
    return jnp.dot(a, b, preferred_element_type=F32)


def _dot_nt(a, b):
    return lax.dot_general(a, b, (((1,), (1,)), ((), ())), preferred_element_type=F32)


def _dot_tn(a, b):
    return lax.dot_general(a, b, (((0,), (0,)), ((), ())), preferred_element_type=F32)


def _split3(a):
    hi = a.astype(BF16)
    r1 = a - hi.astype(F32)
    mid = r1.astype(BF16)
    lo = (r1 - mid.astype(F32)).astype(BF16)
    return hi, mid, lo


def _dot3_left(m01, a):
    hi, mid, lo = _split3(a)
    return _dot(m01, hi) + _dot(m01, mid) + _dot(m01, lo)


def _dot3_right(a, m01):
    hi, mid, lo = _split3(a)
    return _dot(hi, m01) + _dot(mid, m01) + _dot(lo, m01)


def _sigmoid(x):
    return 1.0 / (1.0 + jnp.exp(-x))


def _silu(x):
    return x * _sigmoid(x)


def _softplus(x):
    return jnp.maximum(x, 0.0) + jnp.log1p(jnp.exp(-jnp.abs(x)))


def _rms(x, gain):
    return x * lax.rsqrt(jnp.mean(x * x, axis=-1, keepdims=True) + EPS) * gain


def _in_proj_kernel(x_ref, g_ref, w_ref, wdt_ref, o_ref, dt_ref, xn_scr):
    @pl.when(pl.program_id(1) == 0)
    def _():
        xn_scr[...] = _rms(x_ref[...], g_ref[...]).astype(BF16)
        dt_ref[...] = _dot(xn_scr[...], wdt_ref[...])

    o_ref[...] = _dot(xn_scr[...], w_ref[...]).astype(o_ref.dtype)


def _in_proj(x, gain, w_main, w_dt, tm, out_dtype):
    t = x.shape[0]
    tn = 1024
    return pl.pallas_call(
        _in_proj_kernel,
        grid=(t // tm, N_PROJ // tn),
        in_specs=[
            pl.BlockSpec((tm, D_MODEL), lambda i, j: (i, 0)),
            pl.BlockSpec((1, D_MODEL), lambda i, j: (0, 0)),
            pl.BlockSpec((D_MODEL, tn), lambda i, j: (0, j)),
            pl.BlockSpec((D_MODEL, DT_PAD), lambda i, j: (0, 0)),
        ],
        out_specs=[
            pl.BlockSpec((tm, tn), lambda i, j: (i, j)),
            pl.BlockSpec((tm, DT_PAD), lambda i, j: (i, 0)),
        ],
        out_shape=[jax.ShapeDtypeStruct((t, N_PROJ), out_dtype), jax.ShapeDtypeStruct((t, DT_PAD), F32)],
        scratch_shapes=[pltpu.VMEM((tm, D_MODEL), BF16)],
        compiler_params=_cparams(("parallel", "arbitrary")),
        name="in_proj",
    )(x, gain, w_main, w_dt)


def _ret_log_gamma():
    return np.log1p(-np.exp2(-5.0 - np.arange(RET_HEADS, dtype=np.float64)))


def _ret_consts(r):
    lg = _ret_log_gamma()
    i = np.arange(r)
    diff = (i[:, None] - i[None, :]).astype(np.float64)
    dec = np.where(diff >= 0, np.exp(lg[:, None, None] * np.maximum(diff, 0.0)), 0.0)
    qd = np.exp(lg[:, None] * (i + 1.0))
    kd = np.exp(lg[:, None] * (r - 1.0 - i))
    gl = tuple(float(v) for v in np.exp(lg * r))
    bc = lambda a: jnp.asarray(np.broadcast_to(a[:, :, None], (RET_HEADS, r, LANES)), F32)
    return jnp.asarray(dec, F32), bc(qd), bc(kd), gl


def _rope_tables(pos):
    half = RET_DK // 2
    inv = ROPE_BASE ** (-(np.arange(half, dtype=np.float64) / half))
    ang = pos.astype(np.float64)[:, None] * inv[None, :]
    cos, sin = np.cos(ang), np.sin(ang)
    cos2 = np.concatenate([cos, cos], axis=1)
    sin2 = np.concatenate([-sin, sin], axis=1)
    scale = RET_DK ** -0.5
    return (jnp.asarray(cos2, F32), jnp.asarray(sin2, F32),
            jnp.asarray(cos2 * scale, F32), jnp.asarray(sin2 * scale, F32))


def _rope(t, cos2, sin2):
    return t * cos2 + pltpu.roll(t, RET_DK // 2, 1) * sin2


def _head_layernorm(o, gain):
    oc = o - jnp.mean(o, axis=-1, keepdims=True)
    return oc * lax.rsqrt(jnp.mean(oc * oc, axis=-1, keepdims=True) + EPS) * gain


def _ret_kernel(rq_ref, rk_ref, rv_ref, rg_ref, cq_ref, sq_ref, ck_ref, sk_ref, dec_ref, qd_ref, kd_ref,
                gain_ref, y_ref, sout_ref, s_scr, *, gl):
    c = pl.program_id(1)

    @pl.when(c == 0)
    def _():
        s_scr[...] = jnp.zeros_like(s_scr)

    cq, sq, ck, sk = cq_ref[...], sq_ref[...], ck_ref[...], sk_ref[...]
    for h in range(RET_HEADS):
        ks = slice(h * RET_DK, (h + 1) * RET_DK)
        vs = slice(h * RET_DV, (h + 1) * RET_DV)
        q = _rope(rq_ref[:, ks].astype(F32), cq, sq)
        k = _rope(rk_ref[:, ks].astype(F32), ck, sk)
        v = rv_ref[:, vs].astype(BF16)
        att = (_dot_nt(q.astype(BF16), k.astype(BF16)) * dec_ref[h]).astype(BF16)
        s_old = s_scr[h]
        o = _dot(att, v) + _dot((q * qd_ref[h]).astype(BF16), s_old.astype(BF16))
        s_scr[h] = gl[h] * s_old + _dot_tn((k * kd_ref[h]).astype(BF16), v)
        y = _silu(rg_ref[:, vs].astype(F32)) * _head_layernorm(o, gain_ref[:, vs])
        y_ref[:, vs] = y.astype(y_ref.dtype)

    @pl.when(c == pl.num_programs(1) - 1)
    def _():
        sout_ref[0] = s_scr[...]


def _ret_prompt(proj, gain, batch, seq):
    r = CHUNK_ROWS
    nc = seq // r
    dec, qd, kd, gl = _ret_consts(r)
    cq, sq, ck, sk = _rope_tables(np.arange(seq))
    row = lambda b, c: b * nc + c
    tab = pl.BlockSpec((r, RET_DK), lambda b, c: (c, 0))
    const3 = lambda shape: pl.BlockSpec(shape, lambda b, c: (0, 0, 0))
    return pl.pallas_call(
        functools.partial(_ret_kernel, gl=gl),
        grid=(batch, nc),
        in_specs=[
            pl.BlockSpec((r, 512), lambda b, c: (row(b, c), 0)),
            pl.BlockSpec((r, 512), lambda b, c: (row(b, c), 1)),
            pl.BlockSpec((r, 1024), lambda b, c: (row(b, c), 1)),
            pl.BlockSpec((r, 1024), lambda b, c: (row(b, c), 2)),
            tab, tab, tab, tab,
            const3((RET_HEADS, r, r)), const3((RET_HEADS, r, LANES)), const3((RET_HEADS, r, LANES)),
            pl.BlockSpec((1, 1024), lambda b, c: (0, 0)),
        ],
        out_specs=[
            pl.BlockSpec((r, 1024), lambda b, c: (row(b, c), 0)),
            pl.BlockSpec((1, RET_HEADS, RET_DK, RET_DV), lambda b, c: (b, 0, 0, 0)),
        ],
        out_shape=[jax.ShapeDtypeStruct((batch * seq, 1024), BF16),
                   jax.ShapeDtypeStruct((batch, RET_HEADS, RET_DK, RET_DV), F32)],
        scratch_shapes=[pltpu.VMEM((RET_HEADS, RET_DK, RET_DV), F32)],
        compiler_params=_cparams(("parallel", "arbitrary")),
        name="ret_prompt",
    )(proj, proj, proj, proj, cq, sq, ck, sk, dec, qd, kd, gain)


HG_CHUNK = 128
HG_LEVELS = (64, 32, 16, 8)
HG_TILE_REF = SUBLANES // 2 - 1
LOG2E = 1.4426950408889634


def _hg_masks():
    i = np.arange(HG_CHUNK)
    out = [(i[:, None] // (2 * b)) == (i[None, :] // (2 * b)) for b in HG_LEVELS[1:]]
    out.append(((i[:, None] // SUBLANES) == (i[None, :] // SUBLANES)) & (i[:, None] >= i[None, :]))
    return jnp.asarray(np.stack(out).astype(np.float32))


def _tri_ones(r):
    i = np.arange(r)
    return jnp.asarray((i[:, None] >= i[None, :]).astype(np.float32), BF16)


def _halves(a, nb):
    a5 = a.reshape(HG_CHUNK // (2 * nb * SUBLANES), 2, nb, SUBLANES, a.shape[-1])
    return a5[:, 0], a5[:, 1]


def _join_halves(lower, upper):
    return jnp.stack([lower, upper], axis=1).reshape(HG_CHUNK, lower.shape[-1])


def _tile_row_bcast(a, j):
    a3 = a.reshape(HG_CHUNK // SUBLANES, SUBLANES, a.shape[-1])
    return jnp.broadcast_to(a3[:, j:j + 1, :], a3.shape).reshape(a.shape)


def _per_head_nt(a, b):
    return [_dot_nt(a[:, h * LANES:(h + 1) * LANES], b[:, h * LANES:(h + 1) * LANES]) for h in range(HG_HEADS)]


def _hg_chunk_att(q, kk, g2, mask_ref):
    tile_last = _tile_row_bcast(g2, SUBLANES - 1)
    att = None
    for lvl, b in enumerate(HG_LEVELS):
        nb = b // SUBLANES
        _, q_up = _halves(q, nb)
        k_lo, _ = _halves(kk, nb)
        g_lo, g_up = _halves(g2, nb)
        ref = _halves(tile_last, nb)[0][:, nb - 1:nb]
        qe = q_up * jnp.exp2(g_up - ref)
        ke = k_lo * jnp.exp2(ref - g_lo)
        zeros = jnp.zeros_like(qe)
        prods = _per_head_nt(_join_halves(zeros, qe).astype(BF16), _join_halves(ke, zeros).astype(BF16))
        att = prods if lvl == 0 else [a + p * mask_ref[lvl - 1] for a, p in zip(att, prods)]
    d = g2 - _tile_row_bcast(g2, HG_TILE_REF)
    prods = _per_head_nt((q * jnp.exp2(d)).astype(BF16), (kk * jnp.exp2(-d)).astype(BF16))
    same_tile = mask_ref[len(HG_LEVELS) - 1] > 0.5
    return [a + jnp.where(same_tile, p, 0.0) for a, p in zip(att, prods)]


def _hgrn_kernel(hq_ref, hf_ref, hi_ref, hg_ref, lb_ref, gain_ref, tri_ref, mask_ref, y_ref, sout_ref,
                 st_scr, *, r):
    c = pl.program_id(1)

    @pl.when(c == 0)
    def _():
        st_scr[...] = jnp.zeros_like(st_scr)

    lb = lb_ref[...]
    for ck in range(r // HG_CHUNK):
        rows = slice(ck * HG_CHUNK, (ck + 1) * HG_CHUNK)
        q = _silu(hq_ref[rows, :].astype(F32))
        f = jnp.clip(lb + (1.0 - lb) * _sigmoid(hf_ref[rows, :].astype(F32)), HG_CLIP_MIN, 1.0)
        kk = 1.0 - f
        g2 = _dot3_left(tri_ref[...], jnp.log(f)) * LOG2E
        g2_last = g2[HG_CHUNK - 1:HG_CHUNK, :]
        att = _hg_chunk_att(q, kk, g2, mask_ref)
        q_in = (q * jnp.exp2(g2)).astype(BF16)
        k_out = (kk * jnp.exp2(g2_last - g2)).astype(BF16)
        e_last = jnp.exp2(g2_last)
        gate = _sigmoid(hg_ref[rows, :].astype(F32))
        for h in range(HG_HEADS):
            sl = slice(h * LANES, (h + 1) * LANES)
            v = hi_ref[rows, sl]
            st_old = st_scr[h]
            o = _dot(att[h].astype(BF16), v) + _dot_nt(q_in[:, sl], st_old.astype(BF16))
            st_scr[h] = st_old * e_last[:, sl] + _dot_tn(v, k_out[:, sl])
            o = o * lax.rsqrt(jnp.mean(o * o, axis=-1, keepdims=True) + EPS) * gain_ref[:, sl]
            y_ref[rows, sl] = (o * gate[:, sl]).astype(y_ref.dtype)

    @pl.when(c == pl.num_programs(1) - 1)
    def _():
        for h in range(HG_HEADS):
            sout_ref[0, h] = st_scr[h].T


def _hgrn_prompt(proj, lb, gain, batch, seq):
    r = CHUNK_ROWS
    nc = seq // r
    masks = _hg_masks()
    row = lambda b, c: b * nc + c
    blk = lambda j: pl.BlockSpec((r, 1024), lambda b, c: (row(b, c), j))
    vec = pl.BlockSpec((1, 1024), lambda b, c: (0, 0))
    return pl.pallas_call(
        functools.partial(_hgrn_kernel, r=r),
        grid=(batch, nc),
        in_specs=[blk(3), blk(4), blk(5), blk(6), vec, vec,
                  pl.BlockSpec((HG_CHUNK, HG_CHUNK), lambda b, c: (0, 0)),
                  pl.BlockSpec(masks.shape, lambda b, c: (0, 0, 0))],
        out_specs=[
            pl.BlockSpec((r, 1024), lambda b, c: (row(b, c), 0)),
            pl.BlockSpec((1, HG_HEADS, HG_EXPAND, HG_DV), lambda b, c: (b, 0, 0, 0)),
        ],
        out_shape=[jax.ShapeDtypeStruct((batch * seq, 1024), BF16),
                   jax.ShapeDtypeStruct((batch, HG_HEADS, HG_EXPAND, HG_DV), F32)],
        scratch_shapes=[pltpu.VMEM((HG_HEADS, HG_DV, HG_EXPAND), F32)],
        compiler_params=_cparams(("parallel", "arbitrary")),
        name="hgrn_prompt",
    )(proj, proj, proj, proj, lb, gain, _tri_ones(HG_CHUNK), masks)


HEADS_PER_GROUP = SSM_HEADS // SSM_GROUPS
GROUP_W = HEADS_PER_GROUP * SSM_HEADDIM
CONV_PAD = SUBLANES
SSD_CHUNK = 128


def _head_expand(vals, h0, lane_head):
    out = jnp.zeros((vals.shape[0], GROUP_W), F32)
    for j in range(HEADS_PER_GROUP):
        out = jnp.where(lane_head == j, vals[:, h0 + j:h0 + j + 1], out)
    return out


def _ssd_kernel(z_ref, xbc_ref, dt_ref, cw_ref, cb_ref, dtb_ref, alog_ref, dsk_ref, norm_ref, tri_ref,
                shift_ref, y_ref, sout_ref, cout_ref, hist_scr, s_scr, o_scr, *, r):
    c = pl.program_id(1)
    last = pl.num_programs(1) - 1

    @pl.when(c == 0)
    def _():
        hist_scr[0:CONV_PAD, :] = jnp.zeros((CONV_PAD, SSM_CONV_CH), F32)
        s_scr[...] = jnp.zeros_like(s_scr)

    u_bf = xbc_ref[...]
    u = u_bf.astype(F32)
    hist_scr[CONV_PAD:2 * CONV_PAD, :] = u[0:CONV_PAD]
    conv = cb_ref[...] + cw_ref[SSM_CONV - 1:SSM_CONV, :] * u
    head = cb_ref[...]
    for w in range(SSM_CONV):
        off = CONV_PAD - (SSM_CONV - 1) + w
        head = head + cw_ref[w:w + 1, :] * hist_scr[off:off + CONV_PAD, :]
    for back in range(1, SSM_CONV):
        conv = conv + cw_ref[SSM_CONV - 1 - back:SSM_CONV - back, :] * _dot(shift_ref[back - 1], u_bf)
    conv = jnp.concatenate([head, conv[CONV_PAD:]], axis=0)

    @pl.when(c == last)
    def _():
        cout_ref[0] = u[r - (SSM_CONV - 1):r, :]

    hist_scr[0:CONV_PAD, :] = u[r - CONV_PAD:r]

    xc = _silu(conv)
    xs = xc[:, :SSM_INNER]
    bm = xc[:, SSM_INNER:SSM_INNER + SSM_GROUPS * SSM_STATE]
    cm = xc[:, SSM_INNER + SSM_GROUPS * SSM_STATE:]

    dt_all = _softplus(dt_ref[...] + dtb_ref[...])
    rate = -jnp.exp(alog_ref[...]) * LOG2E
    rc = SSD_CHUNK
    ri = lax.broadcasted_iota(jnp.int32, (rc, rc), 0)
    ci = lax.broadcasted_iota(jnp.int32, (rc, rc), 1)
    causal = ri >= ci
    lane_head = lax.broadcasted_iota(jnp.int32, (rc, GROUP_W), 1) // SSM_HEADDIM
    lane_head1 = lax.broadcasted_iota(jnp.int32, (1, GROUP_W), 1) // SSM_HEADDIM
    groups = range(SSM_GROUPS)

    for ck in range(r // rc):
        rows = slice(ck * rc, (ck + 1) * rc)
        dt = dt_all[rows]
        g2 = _dot3_left(tri_ref[...], dt * rate)
        g2_last = g2[rc - 1:rc, :]
        g2_t = g2.T
        dt_t = dt.T
        e_cum = jnp.exp2(g2)
        w_in = dt * jnp.exp2(g2_last - g2)
        e_last = jnp.exp2(g2_last)

        cg = [cm[rows, g * SSM_STATE:(g + 1) * SSM_STATE].astype(BF16) for g in groups]
        bg = [bm[rows, g * SSM_STATE:(g + 1) * SSM_STATE].astype(BF16) for g in groups]
        xs_g = [xs[rows, g * GROUP_W:(g + 1) * GROUP_W] for g in groups]
        xs_b = [x.astype(BF16) for x in xs_g]
        s_old = [s_scr[g] for g in groups]
        cb = [jnp.where(causal, _dot_nt(cg[g], bg[g]), 0.0) for g in groups]
        carried = [_dot(cg[g], s_old[g].astype(BF16)) for g in groups]
        for g in groups:
            h0 = g * HEADS_PER_GROUP
            o_g = _head_expand(e_cum, h0, lane_head) * carried[g]
            for j in range(HEADS_PER_GROUP):
                h = h0 + j
                dec = jnp.exp2(jnp.minimum(g2[:, h:h + 1] - g2_t[h:h + 1, :], 0.0))
                att = (cb[g] * dec * dt_t[h:h + 1, :]).astype(BF16)
                o_g = o_g + jnp.where(lane_head == j, _dot(att, xs_b[g]), 0.0)
            s_scr[g] = (s_old[g] * _head_expand(e_last, h0, lane_head1)
                        + _dot_tn(bg[g], (xs_g[g] * _head_expand(w_in, h0, lane_head)).astype(BF16)))
            o_scr[rows, g * GROUP_W:(g + 1) * GROUP_W] = o_g + dsk_ref[:, g * GROUP_W:(g + 1) * GROUP_W] * xs_g[g]

    y_ref[...] = _rms(o_scr[...] * _silu(z_ref[...].astype(F32)), norm_ref[...]).astype(y_ref.dtype)

    @pl.when(c == last)
    def _():
        for h in range(SSM_HEADS):
            j = h % HEADS_PER_GROUP
            sout_ref[0, h] = s_scr[h // HEADS_PER_GROUP][:, j * SSM_HEADDIM:(j + 1) * SSM_HEADDIM]


def _shift_ones(r):
    i = np.arange(r)
    mats = [(i[:, None] - i[None, :]) == k for k in range(1, SSM_CONV)]
    return jnp.asarray(np.stack(mats).astype(np.float32), BF16)


def _pad_lanes(v):
    return jnp.pad(v.astype(F32), (0, LANES - v.shape[0])).reshape(1, LANES)


def _ssd_prompt(proj, dt, conv_w, conv_b, dt_bias, a_log, dsk_exp, norm, batch, seq):
    r = CHUNK_ROWS
    nc = seq // r
    row = lambda b, c: b * nc + c
    vec = lambda n: pl.BlockSpec((1, n), lambda b, c: (0, 0))
    return pl.pallas_call(
        functools.partial(_ssd_kernel, r=r),
        grid=(batch, nc),
        in_specs=[
            pl.BlockSpec((r, 1024), lambda b, c: (row(b, c), 7)),
            pl.BlockSpec((r, SSM_CONV_CH), lambda b, c: (row(b, c), 4)),
            pl.BlockSpec((r, DT_PAD), lambda b, c: (row(b, c), 0)),
            pl.BlockSpec((SSM_CONV, SSM_CONV_CH), lambda b, c: (0, 0)),
            vec(SSM_CONV_CH), vec(LANES), vec(LANES), vec(SSM_INNER), vec(SSM_INNER),
            pl.BlockSpec((SSD_CHUNK, SSD_CHUNK), lambda b, c: (0, 0)),
            pl.BlockSpec((SSM_CONV - 1, r, r), lambda b, c: (0, 0, 0)),
        ],
        out_specs=[
            pl.BlockSpec((r, 1024), lambda b, c: (row(b, c), 0)),
            pl.BlockSpec((1, SSM_HEADS, SSM_STATE, SSM_HEADDIM), lambda b, c: (b, 0, 0, 0)),
            pl.BlockSpec((1, SSM_CONV - 1, SSM_CONV_CH), lambda b, c: (b, 0, 0)),
        ],
        out_shape=[jax.ShapeDtypeStruct((batch * seq, 1024), BF16),
                   jax.ShapeDtypeStruct((batch, SSM_HEADS, SSM_STATE, SSM_HEADDIM), F32),
                   jax.ShapeDtypeStruct((batch, SSM_CONV - 1, SSM_CONV_CH), F32)],
        scratch_shapes=[pltpu.VMEM((2 * CONV_PAD, SSM_CONV_CH), F32),
                        pltpu.VMEM((SSM_GROUPS, SSM_STATE, GROUP_W), F32),
                        pltpu.VMEM((r, SSM_INNER), F32)],
        compiler_params=_cparams(("parallel", "arbitrary")),
        name="ssd_prompt",
    )(proj, proj, dt, conv_w, conv_b.reshape(1, -1), _pad_lanes(dt_bias), _pad_lanes(a_log), dsk_exp, norm,
      _tri_ones(SSD_CHUNK), _shift_ones(r))


def _merge_kernel(x_ref, yr_ref, yh_ref, ym_ref, g0_ref, g1_ref, g2_ref, wb_ref, wo_ref, npost_ref, o_ref):
    merged = _sigmoid(g0_ref[...].astype(F32)) * _dot(yr_ref[...].astype(BF16), wb_ref[0])
    merged = merged + _sigmoid(g1_ref[...].astype(F32)) * _dot(yh_ref[...].astype(BF16), wb_ref[1])
    merged = merged + _sigmoid(g2_ref[...].astype(F32)) * _dot(ym_ref[...].astype(BF16), wb_ref[2])
    m = _dot(merged.astype(BF16), wo_ref[...])
    o_ref[...] = x_ref[...] + _rms(m, npost_ref[...])


def _merge(x, yr, yh, ym, proj, wb, wo, npost, tm):
    t = x.shape[0]
    rows = lambda j: pl.BlockSpec((tm, D_MODEL), lambda i: (i, j))
    return pl.pallas_call(
        _merge_kernel,
        grid=(t // tm,),
        in_specs=[rows(0), rows(0), rows(0), rows(0), rows(10), rows(11), rows(12),
                  pl.BlockSpec((3, D_MODEL, D_MODEL), lambda i: (0, 0, 0)),
                  pl.BlockSpec((D_MODEL, D_MODEL), lambda i: (0, 0)),
                  pl.BlockSpec((1, D_MODEL), lambda i: (0, 0))],
        out_specs=rows(0),
        out_shape=jax.ShapeDtypeStruct((t, D_MODEL), F32),
        compiler_params=_cparams(("parallel",)),
        name="merge",
    )(x, yr, yh, ym, proj, proj, proj, wb, wo, npost)


def _ffn_kernel(x_ref, npre_ref, npost_ref, wu_ref, wd_ref, o_ref, xn_scr, acc_scr):
    j = pl.program_id(1)

    @pl.when(j == 0)
    def _():
        xn_scr[...] = _rms(x_ref[...], npre_ref[...]).astype(BF16)
        acc_scr[...] = jnp.zeros_like(acc_scr)

    hid = jnp.square(jnp.maximum(_dot(xn_scr[...], wu_ref[...]), 0.0))
    acc_scr[...] += _dot(hid.astype(BF16), wd_ref[...])

    @pl.when(j == pl.num_programs(1) - 1)
    def _():
        o_ref[...] = x_ref[...] + _rms(acc_scr[...], npost_ref[...])


def _ffn(x, npre, npost, wu, wd, tm):
    t = x.shape[0]
    tf = 1024
    vec = pl.BlockSpec((1, D_MODEL), lambda i, j: (0, 0))
    return pl.pallas_call(
        _ffn_kernel,
        grid=(t // tm, D_FF // tf),
        in_specs=[pl.BlockSpec((tm, D_MODEL), lambda i, j: (i, 0)), vec, vec,
                  pl.BlockSpec((D_MODEL, tf), lambda i, j: (0, j)),
                  pl.BlockSpec((tf, D_MODEL), lambda i, j: (j, 0))],
        out_specs=pl.BlockSpec((tm, D_MODEL), lambda i, j: (i, 0)),
        out_shape=jax.ShapeDtypeStruct((t, D_MODEL), F32),
        scratch_shapes=[pltpu.VMEM((tm, D_MODEL), BF16), pltpu.VMEM((tm, D_MODEL), F32)],
        compiler_params=_cparams(("parallel", "arbitrary")),
        name="ffn",
    )(x, npre, npost, wu, wd)


def _dec_prep_kernel(proj_ref, dt_ref, cst_ref, cq_ref, sq_ref, ck_ref, sk_ref, lb_ref, cw_ref, cb_ref,
                     dtb_ref, alog_ref, hexp_ref,
                     cret_ref, chg_ref, cssd_ref, xs_ref, xdt_ref, drow_ref, cnew_ref):
    n = proj_ref.shape[0]
    cq, sq, ck, sk = cq_ref[...], sq_ref[...], ck_ref[...], sk_ref[...]
    for h in range(RET_HEADS):
        ks = slice(h * RET_DK, (h + 1) * RET_DK)
        cret_ref[:, ks] = _rope(proj_ref[:, ks], cq, sq)
        k_off = RET_HEADS * RET_DK
        cret_ref[:, k_off + h * RET_DK:k_off + (h + 1) * RET_DK] = _rope(
            proj_ref[:, k_off + h * RET_DK:k_off + (h + 1) * RET_DK], ck, sk)

    lb = lb_ref[...]
    chg_ref[:, 0:1024] = _silu(proj_ref[:, 3072:4096])
    chg_ref[:, 1024:2048] = jnp.clip(lb + (1.0 - lb) * _sigmoid(proj_ref[:, 4096:5120]), HG_CLIP_MIN, 1.0)

    u = proj_ref[:, 8192:8192 + SSM_CONV_CH]
    conv = cb_ref[...] + cw_ref[SSM_CONV - 1:SSM_CONV, :] * u
    for w in range(SSM_CONV - 1):
        conv = conv + cw_ref[w:w + 1, :] * cst_ref[w]
    cnew_ref[0] = cst_ref[1]
    cnew_ref[1] = cst_ref[2]
    cnew_ref[2] = u
    xc = _silu(conv)
    xs = xc[:, :SSM_INNER]
    xs_ref[...] = xs
    cssd_ref[:, 0:512] = xc[:, SSM_INNER + 512:]
    cssd_ref[:, 512:1024] = xc[:, SSM_INNER:SSM_INNER + 512]
    dt = _softplus(dt_ref[...] + dtb_ref[...])
    decay = jnp.exp(dt * (-jnp.exp(alog_ref[...])))
    hexp = hexp_ref[...]
    xdt_ref[...] = xs * _dot3_right(dt, hexp)
    drow_ref[...] = _dot3_right(decay, hexp)
    del n


def _dec_prep(proj, dt, conv_state_t, lb, conv_w, conv_b, dt_bias, a_log):
    n = proj.shape[0]
    cq, sq, ck, sk = _rope_tables(np.full((n,), PAST_LEN))
    hexp = np.zeros((LANES, SSM_INNER), np.float32)
    for h in range(SSM_HEADS):
        hexp[h, h * SSM_HEADDIM:(h + 1) * SSM_HEADDIM] = 1.0
    outs = [(n, 1024), (n, 2048), (n, 1024), (n, 1024), (n, 1024), (n, 1024)]
    return pl.pallas_call(
        _dec_prep_kernel,
        out_shape=[jax.ShapeDtypeStruct(s, F32) for s in outs]
        + [jax.ShapeDtypeStruct((SSM_CONV - 1, n, SSM_CONV_CH), F32)],
        compiler_params=pltpu.CompilerParams(vmem_limit_bytes=VMEM_LIMIT),
        name="dec_prep",
    )(proj, dt, conv_state_t, cq, sq, ck, sk, lb, conv_w, conv_b.reshape(1, -1), _pad_lanes(dt_bias),
      _pad_lanes(a_log), jnp.asarray(hexp, BF16))


def _cols(a, tb):
    n, ch = a.shape
    return a.reshape(n // tb, tb, ch).transpose(0, 2, 1)


def _state_update_call(body, name, layer, state, carried, n_tokens, heads_per_step, in_specs, operands,
                       scratch):
    depth, _, heads, sk, sv = state.shape
    tb = DEC_TOKENS
    if carried is None:
        src = state
        sspec = pl.BlockSpec((depth, tb, heads_per_step, sk, sv), lambda t, j: (0, t, j, 0, 0))
        own, keep, aliases = layer, tuple(d for d in range(depth) if d != layer), {}
    else:
        src = carried
        sspec = pl.BlockSpec((1, tb, heads_per_step, sk, sv), lambda t, j: (layer, t, j, 0, 0))
        own, keep, aliases = 0, (), {0: 1}
    return pl.pallas_call(
        functools.partial(body, own=own, keep=keep),
        grid=(n_tokens // tb, heads // heads_per_step),
        in_specs=[sspec] + in_specs,
        out_specs=[pl.BlockSpec((tb, 1024), lambda t, j: (t, 0)), sspec],
        out_shape=[jax.ShapeDtypeStruct((n_tokens, 1024), F32), jax.ShapeDtypeStruct(state.shape, F32)],
        input_output_aliases=aliases,
        scratch_shapes=[scratch],
        compiler_params=_cparams(("parallel", "arbitrary")),
        name=name,
    )(src, *operands)


def _ret_dec_kernel(s_ref, ct_ref, v_ref, rg_ref, gain_ref, y_ref, so_ref, o_scr, *, gamma, own, keep):
    for i in range(DEC_TOKENS):
        for h in range(RET_HEADS):
            qcol = ct_ref[0, h * RET_DK:(h + 1) * RET_DK, i:i + 1]
            kcol = ct_ref[0, (RET_HEADS + h) * RET_DK:(RET_HEADS + h + 1) * RET_DK, i:i + 1]
            vs = slice(h * RET_DV, (h + 1) * RET_DV)
            s_new = gamma[h] * s_ref[own, i, h] + kcol * v_ref[i:i + 1, vs]
            so_ref[own, i, h] = s_new
            o_scr[i:i + 1, vs] = jnp.sum(qcol * s_new, axis=0, keepdims=True)
    for j in keep:
        so_ref[j] = s_ref[j]
    for h in range(RET_HEADS):
        vs = slice(h * RET_DV, (h + 1) * RET_DV)
        y_ref[:, vs] = _silu(rg_ref[:, vs]) * _head_layernorm(o_scr[:, vs], gain_ref[:, vs])


def _ret_dec(layer, state, carried, c_ret, proj, gain):
    tb = DEC_TOKENS
    gamma = tuple(float(v) for v in np.exp(_ret_log_gamma()))
    return _state_update_call(
        functools.partial(_ret_dec_kernel, gamma=gamma), "ret_dec", layer, state, carried, proj.shape[0],
        RET_HEADS,
        [pl.BlockSpec((1, 2 * RET_HEADS * RET_DK, tb), lambda t, j: (t, 0, 0)),
         pl.BlockSpec((tb, 1024), lambda t, j: (t, 1)), pl.BlockSpec((tb, 1024), lambda t, j: (t, 2)),
         pl.BlockSpec((1, 1024), lambda t, j: (0, 0))],
        (_cols(c_ret, tb), proj, proj, gain), pltpu.VMEM((tb, 1024), F32))


def _hgrn_dec_kernel(s_ref, ct_ref, v_ref, hg_ref, gain_ref, y_ref, so_ref, o_scr, *, own, keep):
    for i in range(DEC_TOKENS):
        for h in range(HG_HEADS):
            sl = slice(h * LANES, (h + 1) * LANES)
            qcol = ct_ref[0, h * HG_EXPAND:(h + 1) * HG_EXPAND, i:i + 1]
            fcol = ct_ref[0, (HG_HEADS + h) * HG_EXPAND:(HG_HEADS + h + 1) * HG_EXPAND, i:i + 1]
            s_new = fcol * s_ref[own, i, h] + (1.0 - fcol) * v_ref[i:i + 1, sl]
            so_ref[own, i, h] = s_new
            o_scr[i:i + 1, sl] = jnp.sum(qcol * s_new, axis=0, keepdims=True)
    for j in keep:
        so_ref[j] = s_ref[j]
    for h in range(HG_HEADS):
        sl = slice(h * LANES, (h + 1) * LANES)
        o = o_scr[:, sl]
        o = o * lax.rsqrt(jnp.mean(o * o, axis=-1, keepdims=True) + EPS) * gain_ref[:, sl]
        y_ref[:, sl] = o * _sigmoid(hg_ref[:, sl])


def _hgrn_dec(layer, state, carried, c_hg, proj, gain):
    tb = DEC_TOKENS
    return _state_update_call(
        _hgrn_dec_kernel, "hgrn_dec", layer, state, carried, proj.shape[0], HG_HEADS,
        [pl.BlockSpec((1, 2 * HG_HEADS * HG_EXPAND, tb), lambda t, j: (t, 0, 0)),
         pl.BlockSpec((tb, 1024), lambda t, j: (t, 5)), pl.BlockSpec((tb, 1024), lambda t, j: (t, 6)),
         pl.BlockSpec((1, 1024), lambda t, j: (0, 0))],
        (_cols(c_hg, tb), proj, proj, gain), pltpu.VMEM((tb, 1024), F32))


def _ssd_dec_kernel(s_ref, c_ref, b_ref, xdt_ref, drow_ref, xs_ref, z_ref, dsk_ref, norm_ref,
                    y_ref, so_ref, o_scr, *, own, keep):
    g = pl.program_id(1)
    c_rows = c_ref[...].astype(BF16)
    for i in range(DEC_TOKENS):
        b_row = b_ref[i:i + 1, :]
        new = []
        for j in range(HEADS_PER_GROUP):
            hs = slice(j * SSM_HEADDIM, (j + 1) * SSM_HEADDIM)
            decay = drow_ref[i:i + 1, j * SSM_HEADDIM:j * SSM_HEADDIM + 1]
            s_new = decay * s_ref[own, i, j] + xdt_ref[0, hs, i:i + 1] * b_row
            so_ref[own, i, j] = s_new
            new.append(s_new)
        s_grp = jnp.concatenate(new, axis=0).astype(BF16)
        o_scr[g, i:i + 1, :] = _dot_nt(c_rows, s_grp)[i:i + 1, :]
    for d in keep:
        so_ref[d] = s_ref[d]

    @pl.when(g == pl.num_programs(1) - 1)
    def _():
        o = jnp.concatenate([o_scr[k] for k in range(SSM_GROUPS)], axis=1) + dsk_ref[...] * xs_ref[...]
        y_ref[...] = _rms(o * _silu(z_ref[...]), norm_ref[...])


def _ssd_dec(layer, state, carried, c_ssd, xs, xdt, drow, proj, dsk_exp, norm):
    tb = DEC_TOKENS
    rows = lambda col: pl.BlockSpec((tb, 1024), lambda t, j: (t, col))
    vec = pl.BlockSpec((1, 1024), lambda t, j: (0, 0))
    return _state_update_call(
        _ssd_dec_kernel, "ssd_dec", layer, state, carried, proj.shape[0], HEADS_PER_GROUP,
        [pl.BlockSpec((tb, SSM_STATE), lambda t, j: (t, j)),
         pl.BlockSpec((tb, SSM_STATE), lambda t, j: (t, SSM_GROUPS + j)),
         pl.BlockSpec((1, GROUP_W, tb), lambda t, j: (t, j, 0)),
         pl.BlockSpec((tb, GROUP_W), lambda t, j: (t, j)), rows(0), rows(7), vec, vec],
        (c_ssd, c_ssd, _cols(xdt, tb), drow, xs, proj, dsk_exp, norm),
        pltpu.VMEM((SSM_GROUPS, tb, GROUP_W), F32))


def kernel(x_prompt, x_sample, state_ret, state_hgrn, state_ssm, state_conv, w_in, ret_norm, hg_norm,
           hg_lb_logits, conv_w, conv_b, dt_bias, a_log, d_skip, ssm_norm, w_branch, w_out, norm_mix_pre,
           norm_mix_post, norm_ffn_pre, norm_ffn_post, w_up, w_down):
    bp, lp, _ = x_prompt.shape
    bs, ls, _ = x_sample.shape
    assert ls == 1 and lp % CHUNK_ROWS == 0 and bs % DEC_TOKENS == 0
    depth = w_in.shape[0]

    lb_w = jax.nn.softmax(hg_lb_logits.astype(F32), axis=0)
    lb_all = jnp.cumsum(lb_w, axis=0) - lb_w[0]

    hp = x_prompt.reshape(bp * lp, D_MODEL)
    hs = x_sample.reshape(bs, D_MODEL)
    outs = [[] for _ in range(5)]
    new_ret = new_hg = new_ssm = None
    ssm_t = jnp.swapaxes(state_ssm, -1, -2)
    vec = lambda a: a.reshape(1, -1).astype(F32)
    for l in range(depth):
        w_main = jnp.concatenate([w_in[l, :, :N_MAIN], w_in[l, :, N_MAIN + SSM_HEADS:]], axis=1).astype(BF16)
        w_dt = jnp.pad(w_in[l, :, N_MAIN:N_MAIN + SSM_HEADS], ((0, 0), (0, DT_PAD - SSM_HEADS))).astype(BF16)
        wb, wo = w_branch[l].astype(BF16), w_out[l].astype(BF16)
        wu, wd = w_up[l].astype(BF16), w_down[l].astype(BF16)
        dsk_exp = jnp.repeat(d_skip[l].astype(F32), SSM_HEADDIM).reshape(1, SSM_INNER)
        lb = vec(lb_all[l])

        proj, dt = _in_proj(hp, vec(norm_mix_pre[l]), w_main, w_dt, tm=min(2048, bp * lp), out_dtype=BF16)
        yr, s_ret = _ret_prompt(proj, vec(ret_norm[l]), bp, lp)
        yh, s_hg = _hgrn_prompt(proj, lb, vec(hg_norm[l]), bp, lp)
        ym, s_ssm, s_conv = _ssd_prompt(proj, dt, conv_w[l], conv_b[l], dt_bias[l], a_log[l], dsk_exp,
                                        vec(ssm_norm[l]), bp, lp)
        hp = _merge(hp, yr, yh, ym, proj, wb, wo, vec(norm_mix_post[l]), tm=512)
        hp = _ffn(hp, vec(norm_ffn_pre[l]), vec(norm_ffn_post[l]), wu, wd, tm=1024)
        for lst, val in zip(outs[:4], (s_ret, s_hg, s_ssm, s_conv)):
            lst.append(val)

        proj, dt = _in_proj(hs, vec(norm_mix_pre[l]), w_main, w_dt, tm=bs, out_dtype=F32)
        c_ret, c_hg, c_ssd, xs, xdt, drow, conv_new = _dec_prep(
            proj, dt, state_conv[l].transpose(1, 0, 2), lb, conv_w[l], conv_b[l], dt_bias[l], a_log[l])
        yr, new_ret = _ret_dec(l, state_ret, new_ret, c_ret, proj, vec(ret_norm[l]))
        yh, new_hg = _hgrn_dec(l, state_hgrn, new_hg, c_hg, proj, vec(hg_norm[l]))
        ym, new_ssm = _ssd_dec(l, ssm_t, new_ssm, c_ssd, xs, xdt, drow, proj, dsk_exp, vec(ssm_norm[l]))
        hs = _merge(hs, yr, yh, ym, proj, wb, wo, vec(norm_mix_post[l]), tm=bs)
        hs = _ffn(hs, vec(norm_ffn_pre[l]), vec(norm_ffn_post[l]), wu, wd, tm=bs)
        outs[4].append(conv_new.transpose(1, 0, 2))

    p_ret, p_hg, p_ssm, p_conv, s_conv = [jnp.stack(o) for o in outs]
    return (hp.reshape(bp, lp, D_MODEL), hs.reshape(bs, ls, D_MODEL), p_ret, p_hg, p_ssm, p_conv,
            new_ret, new_hg, jnp.swapaxes(new_ssm, -1, -2), s_conv)
```

```python
import functools
from typing import NamedTuple

import numpy as np
import jax
import jax.numpy as jnp
from jax import lax
from jax.experimental import pallas as pl
from jax.experimental.pallas import tpu as pltpu

F32 = jnp.float32
BF16 = jnp.bfloat16

D_MODEL = 1024
PAST_LEN = 16384
RET_HEADS, RET_DK, RET_DV = 4, 128, 256
ROPE_BASE = 10000.0
HG_HEADS, HG_EXPAND, HG_DV = 8, 128, 128
SSM_HEADS, SSM_HEADDIM, SSM_GROUPS, SSM_STATE, SSM_CONV = 16, 64, 4, 128, 4
SSM_INNER = SSM_HEADS * SSM_HEADDIM
SSM_CONV_CH = SSM_INNER + 2 * SSM_GROUPS * SSM_STATE
D_FF = 4 * D_MODEL
EPS = 1e-6
HG_CLIP_MIN = 1e-6

N_MAIN = 10240
N_PROJ = N_MAIN + 3 * D_MODEL
DT_PAD = 128

LANES = 128
SUBLANES = 8
VMEM_LIMIT = 56 * 1024 * 1024

CHUNK_ROWS = 256
DEC_TOKENS = 8


def _cparams(sem):
    return pltpu.CompilerParams(dimension_semantics=sem, vmem_limit_bytes=VMEM_LIMIT)


def _dot(a, b):
    return jnp.dot(a, b, preferred_element_type=F32)


def _dot_nt(a, b):
    return lax.dot_general(a, b, (((1,), (1,)), ((), ())), preferred_element_type=F32)


def _dot_tn(a, b):
    return lax.dot_general(a, b, (((0,), (0,)), ((), ())), preferred_element_type=F32)


def _split3(a):
    hi = a.astype(BF16)
    r1 = a - hi.astype(F32)
    mid = r1.astype(BF16)
    lo = (r1 - mid.astype(F32)).astype(BF16)
    return hi, mid, lo


def _dot3_left(m01, a):
    hi, mid, lo = _split3(a)
    return _dot(m01, hi) + _dot(m01, mid) + _dot(m01, lo)


def _dot3_right(a, m01):
    hi, mid, lo = _split3(a)
    return _dot(hi, m01) + _dot(mid, m01) + _dot(lo, m01)


def _sigmoid(x):
    return 1.0 / (1.0 + jnp.exp(-x))


def _silu(x):
    return x * _sigmoid(x)


def _softplus(x):
    return jnp.maximum(x, 0.0) + jnp.log1p(jnp.exp(-jnp.abs(x)))


def _rms(x, gain):
    return x * lax.rsqrt(jnp.mean(x * x, axis=-1, keepdims=True) + EPS) * gain


PROJ_TILE = 1024
HG_TILE0, HG_TILES = 3, 4


class _Cols(NamedTuple):
    z: int
    xbc: int
    gates: int


COLS_FULL = _Cols(z=7, xbc=8, gates=10)
COLS_NO_HGRN = _Cols(z=7 - HG_TILES, xbc=8 - HG_TILES, gates=10 - HG_TILES)


def _in_proj_kernel(x_ref, g_ref, w_ref, wdt_ref, o_ref, dt_ref, xn_ref):
    @pl.when(pl.program_id(1) == 0)
    def _():
        xn_ref[...] = _rms(x_ref[...], g_ref[...]).astype(BF16)
        dt_ref[...] = _dot(xn_ref[...], wdt_ref[...])

    o_ref[...] = _dot(xn_ref[...], w_ref[...]).astype(o_ref.dtype)


def _in_proj_normed_kernel(xn_ref, w_ref, wdt_ref, o_ref, dt_ref):
    @pl.when(pl.program_id(1) == 0)
    def _():
        dt_ref[...] = _dot(xn_ref[...], wdt_ref[...])

    o_ref[...] = _dot(xn_ref[...], w_ref[...]).astype(o_ref.dtype)


def _in_proj(x, gain, w_main, w_dt, tm, out_dtype, skip_hgrn):
    t = x.shape[0]
    tn = PROJ_TILE
    n_tiles = N_PROJ // tn - (HG_TILES if skip_hgrn else 0)
    w_tile = (lambda j: j + jnp.where(j >= HG_TILE0, HG_TILES, 0)) if skip_hgrn else (lambda j: j)
    rows = pl.BlockSpec((tm, D_MODEL), lambda i, j: (i, 0))
    w_specs = [pl.BlockSpec((D_MODEL, tn), lambda i, j: (0, w_tile(j))),
               pl.BlockSpec((D_MODEL, DT_PAD), lambda i, j: (0, 0))]
    out_specs = [pl.BlockSpec((tm, tn), lambda i, j: (i, j)), pl.BlockSpec((tm, DT_PAD), lambda i, j: (i, 0))]
    out_shape = [jax.ShapeDtypeStruct((t, n_tiles * tn), out_dtype), jax.ShapeDtypeStruct((t, DT_PAD), F32)]
    if gain is None:
        proj, dt = pl.pallas_call(
            _in_proj_normed_kernel, grid=(t // tm, n_tiles), in_specs=[rows] + w_specs,
            out_specs=out_specs, out_shape=out_shape,
            compiler_params=_cparams(("parallel", "arbitrary")), name="in_proj",
        )(x, w_main, w_dt)
        return proj, dt, x
    return pl.pallas_call(
        _in_proj_kernel, grid=(t // tm, n_tiles),
        in_specs=[rows, pl.BlockSpec((1, D_MODEL), lambda i, j: (0, 0))] + w_specs,
        out_specs=out_specs + [rows], out_shape=out_shape + [jax.ShapeDtypeStruct((t, D_MODEL), BF16)],
        compiler_params=_cparams(("parallel", "arbitrary")), name="in_proj",
    )(x, gain, w_main, w_dt)


def _ret_log_gamma():
    return np.log1p(-np.exp2(-5.0 - np.arange(RET_HEADS, dtype=np.float64)))


def _ret_consts(r):
    lg = _ret_log_gamma()
    i = np.arange(r)
    diff = (i[:, None] - i[None, :]).astype(np.float64)
    dec = np.where(diff >= 0, np.exp(lg[:, None, None] * np.maximum(diff, 0.0)), 0.0)
    qd = np.exp(lg[:, None] * (i + 1.0))
    kd = np.exp(lg[:, None] * (r - 1.0 - i))
    gl = tuple(float(v) for v in np.exp(lg * r))
    bc = lambda a: jnp.asarray(np.broadcast_to(a[:, :, None], (RET_HEADS, r, LANES)), F32)
    return jnp.asarray(dec, F32), bc(qd), bc(kd), gl


def _rope_tables(pos):
    half = RET_DK // 2
    inv = ROPE_BASE ** (-(np.arange(half, dtype=np.float64) / half))
    ang = pos.astype(np.float64)[:, None] * inv[None, :]
    cos, sin = np.cos(ang), np.sin(ang)
    cos2 = np.concatenate([cos, cos], axis=1)
    sin2 = np.concatenate([-sin, sin], axis=1)
    scale = RET_DK ** -0.5
    return (jnp.asarray(cos2, F32), jnp.asarray(sin2, F32),
            jnp.asarray(cos2 * scale, F32), jnp.asarray(sin2 * scale, F32))


def _rope(t, cos2, sin2):
    return t * cos2 + pltpu.roll(t, RET_DK // 2, 1) * sin2


def _head_layernorm(o, gain):
    oc = o - jnp.mean(o, axis=-1, keepdims=True)
    return oc * lax.rsqrt(jnp.mean(oc * oc, axis=-1, keepdims=True) + EPS) * gain


def _ret_kernel(rq_ref, rk_ref, rv_ref, rg_ref, cq_ref, sq_ref, ck_ref, sk_ref, dec_ref, qd_ref, kd_ref,
                gain_ref, y_ref, sout_ref, s_scr, *, gl):
    c = pl.program_id(1)

    @pl.when(c == 0)
    def _():
        s_scr[...] = jnp.zeros_like(s_scr)

    cq, sq, ck, sk = cq_ref[...], sq_ref[...], ck_ref[...], sk_ref[...]
    for h in range(RET_HEADS):
        ks = slice(h * RET_DK, (h + 1) * RET_DK)
        vs = slice(h * RET_DV, (h + 1) * RET_DV)
        q = _rope(rq_ref[:, ks].astype(F32), cq, sq)
        k = _rope(rk_ref[:, ks].astype(F32), ck, sk)
        v = rv_ref[:, vs].astype(BF16)
        att = (_dot_nt(q.astype(BF16), k.astype(BF16)) * dec_ref[h]).astype(BF16)
        s_old = s_scr[h]
        o = _dot(att, v) + _dot((q * qd_ref[h]).astype(BF16), s_old.astype(BF16))
        s_scr[h] = gl[h] * s_old + _dot_tn((k * kd_ref[h]).astype(BF16), v)
        y = _silu(rg_ref[:, vs].astype(F32)) * _head_layernorm(o, gain_ref[:, vs])
        y_ref[:, vs] = y.astype(y_ref.dtype)

    @pl.when(c == pl.num_programs(1) - 1)
    def _():
        sout_ref[0] = s_scr[...]


def _ret_prompt(proj, gain, batch, seq):
    r = CHUNK_ROWS
    nc = seq // r
    dec, qd, kd, gl = _ret_consts(r)
    cq, sq, ck, sk = _rope_tables(np.arange(seq))
    row = lambda b, c: b * nc + c
    tab = pl.BlockSpec((r, RET_DK), lambda b, c: (c, 0))
    const3 = lambda shape: pl.BlockSpec(shape, lambda b, c: (0, 0, 0))
    return pl.pallas_call(
        functools.partial(_ret_kernel, gl=gl),
        grid=(batch, nc),
        in_specs=[
            pl.BlockSpec((r, 512), lambda b, c: (row(b, c), 0)),
            pl.BlockSpec((r, 512), lambda b, c: (row(b, c), 1)),
            pl.BlockSpec((r, 1024), lambda b, c: (row(b, c), 1)),
            pl.BlockSpec((r, 1024), lambda b, c: (row(b, c), 2)),
            tab, tab, tab, tab,
            const3((RET_HEADS, r, r)), const3((RET_HEADS, r, LANES)), const3((RET_HEADS, r, LANES)),
            pl.BlockSpec((1, 1024), lambda b, c: (0, 0)),
        ],
        out_specs=[
            pl.BlockSpec((r, 1024), lambda b, c: (row(b, c), 0)),
            pl.BlockSpec((1, RET_HEADS, RET_DK, RET_DV), lambda b, c: (b, 0, 0, 0)),
        ],
        out_shape=[jax.ShapeDtypeStruct((batch * seq, 1024), BF16),
                   jax.ShapeDtypeStruct((batch, RET_HEADS, RET_DK, RET_DV), F32)],
        scratch_shapes=[pltpu.VMEM((RET_HEADS, RET_DK, RET_DV), F32)],
        compiler_params=_cparams(("parallel", "arbitrary")),
        name="ret_prompt",
    )(proj, proj, proj, proj, cq, sq, ck, sk, dec, qd, kd, gain)


HG_CHUNK = 128
HG_LEVELS = (64, 32, 16, 8)
HG_TILE_REF = SUBLANES // 2 - 1
LOG2E = 1.4426950408889634


def _hg_masks():
    i = np.arange(HG_CHUNK)
    out = [(i[:, None] // (2 * b)) == (i[None, :] // (2 * b)) for b in HG_LEVELS[1:]]
    out.append(((i[:, None] // SUBLANES) == (i[None, :] // SUBLANES)) & (i[:, None] >= i[None, :]))
    return jnp.asarray(np.stack(out).astype(np.float32))


def _tri_ones(r):
    i = np.arange(r)
    return jnp.asarray((i[:, None] >= i[None, :]).astype(np.float32), BF16)


def _halves(a, nb):
    a5 = a.reshape(HG_CHUNK // (2 * nb * SUBLANES), 2, nb, SUBLANES, a.shape[-1])
    return a5[:, 0], a5[:, 1]


def _join_halves(lower, upper):
    return jnp.stack([lower, upper], axis=1).reshape(HG_CHUNK, lower.shape[-1])


def _tile_row_bcast(a, j):
    a3 = a.reshape(HG_CHUNK // SUBLANES, SUBLANES, a.shape[-1])
    return jnp.broadcast_to(a3[:, j:j + 1, :], a3.shape).reshape(a.shape)


def _per_head_nt(a, b):
    return [_dot_nt(a[:, h * LANES:(h + 1) * LANES], b[:, h * LANES:(h + 1) * LANES]) for h in range(HG_HEADS)]


def _hg_chunk_att(q, kk, g2, mask_ref):
    tile_last = _tile_row_bcast(g2, SUBLANES - 1)
    att = None
    for lvl, b in enumerate(HG_LEVELS):
        nb = b // SUBLANES
        _, q_up = _halves(q, nb)
        k_lo, _ = _halves(kk, nb)
        g_lo, g_up = _halves(g2, nb)
        ref = _halves(tile_last, nb)[0][:, nb - 1:nb]
        qe = q_up * jnp.exp2(g_up - ref)
        ke = k_lo * jnp.exp2(ref - g_lo)
        zeros = jnp.zeros_like(qe)
        prods = _per_head_nt(_join_halves(zeros, qe).astype(BF16), _join_halves(ke, zeros).astype(BF16))
        att = prods if lvl == 0 else [a + p * mask_ref[lvl - 1] for a, p in zip(att, prods)]
    d = g2 - _tile_row_bcast(g2, HG_TILE_REF)
    prods = _per_head_nt((q * jnp.exp2(d)).astype(BF16), (kk * jnp.exp2(-d)).astype(BF16))
    same_tile = mask_ref[len(HG_LEVELS) - 1] > 0.5
    return [a + jnp.where(same_tile, p, 0.0) for a, p in zip(att, prods)]


def _hgrn_kernel(xc_ref, xnx_ref, w0_ref, w1_ref, w2_ref, w3_ref, lb_ref, gain_ref, tri_ref, mask_ref,
                 y_ref, sout_ref, st_scr, pj_scr, *, r):
    c = pl.program_id(1)
    step = pl.program_id(0) * pl.num_programs(1) + c
    w_refs = (w0_ref, w1_ref, w2_ref, w3_ref)

    def project(xn, slot):
        for t, w_ref in enumerate(w_refs):
            pj_scr[slot, :, t * PROJ_TILE:(t + 1) * PROJ_TILE] = _dot(xn, w_ref[...]).astype(BF16)

    @pl.when(step == 0)
    def _():
        project(xc_ref[...], 0)

    @pl.when(c == 0)
    def _():
        st_scr[...] = jnp.zeros_like(st_scr)

    def block(cur):
        project(xnx_ref[...], 1 - cur)
        hq_ref, hf_ref, hi_ref, hg_ref = (
            pj_scr.at[cur, :, t * PROJ_TILE:(t + 1) * PROJ_TILE] for t in range(HG_TILES))
        lb = lb_ref[...]
        for ck in range(r // HG_CHUNK):
            rows = slice(ck * HG_CHUNK, (ck + 1) * HG_CHUNK)
            q = _silu(hq_ref[rows, :].astype(F32))
            f = jnp.clip(lb + (1.0 - lb) * _sigmoid(hf_ref[rows, :].astype(F32)), HG_CLIP_MIN, 1.0)
            kk = 1.0 - f
            g2 = _dot3_left(tri_ref[...], jnp.log(f)) * LOG2E
            g2_last = g2[HG_CHUNK - 1:HG_CHUNK, :]
            att = _hg_chunk_att(q, kk, g2, mask_ref)
            q_in = (q * jnp.exp2(g2)).astype(BF16)
            k_out = (kk * jnp.exp2(g2_last - g2)).astype(BF16)
            e_last = jnp.exp2(g2_last)
            gate = _sigmoid(hg_ref[rows, :].astype(F32))
            for h in range(HG_HEADS):
                sl = slice(h * LANES, (h + 1) * LANES)
                v = hi_ref[rows, sl]
                st_old = st_scr[h]
                o = _dot(att[h].astype(BF16), v) + _dot_nt(q_in[:, sl], st_old.astype(BF16))
                st_scr[h] = st_old * e_last[:, sl] + _dot_tn(v, k_out[:, sl])
                o = o * lax.rsqrt(jnp.mean(o * o, axis=-1, keepdims=True) + EPS) * gain_ref[:, sl]
                y_ref[rows, sl] = (o * gate[:, sl]).astype(y_ref.dtype)

    for parity in range(2):
        pl.when(lax.rem(step, 2) == parity)(functools.partial(block, parity))

    @pl.when(c == pl.num_programs(1) - 1)
    def _():
        for h in range(HG_HEADS):
            sout_ref[0, h] = st_scr[h].T


def _hgrn_prompt(xn, w_main, lb, gain, batch, seq):
    r = CHUNK_ROWS
    nc = seq // r
    masks = _hg_masks()
    row = lambda b, c: b * nc + c
    last_block = batch * nc - 1
    vec = pl.BlockSpec((1, 1024), lambda b, c: (0, 0))
    w_tile = lambda t: pl.BlockSpec((D_MODEL, PROJ_TILE), lambda b, c: (0, HG_TILE0 + t))
    return pl.pallas_call(
        functools.partial(_hgrn_kernel, r=r),
        grid=(batch, nc),
        in_specs=[pl.BlockSpec((r, D_MODEL), lambda b, c: (row(b, c), 0)),
                  pl.BlockSpec((r, D_MODEL), lambda b, c: (jnp.minimum(row(b, c) + 1, last_block), 0)),
                  w_tile(0), w_tile(1), w_tile(2), w_tile(3), vec, vec,
                  pl.BlockSpec((HG_CHUNK, HG_CHUNK), lambda b, c: (0, 0)),
                  pl.BlockSpec(masks.shape, lambda b, c: (0, 0, 0))],
        out_specs=[
            pl.BlockSpec((r, 1024), lambda b, c: (row(b, c), 0)),
            pl.BlockSpec((1, HG_HEADS, HG_EXPAND, HG_DV), lambda b, c: (b, 0, 0, 0)),
        ],
        out_shape=[jax.ShapeDtypeStruct((batch * seq, 1024), BF16),
                   jax.ShapeDtypeStruct((batch, HG_HEADS, HG_EXPAND, HG_DV), F32)],
        scratch_shapes=[pltpu.VMEM((HG_HEADS, HG_DV, HG_EXPAND), F32),
                        pltpu.VMEM((2, r, HG_TILES * PROJ_TILE), BF16)],
        compiler_params=_cparams(("arbitrary", "arbitrary")),
        name="hgrn_prompt",
    )(xn, xn, w_main, w_main, w_main, w_main, lb, gain, _tri_ones(HG_CHUNK), masks)


HEADS_PER_GROUP = SSM_HEADS // SSM_GROUPS
GROUP_W = HEADS_PER_GROUP * SSM_HEADDIM
CONV_PAD = SUBLANES
SSD_CHUNK = 128


def _head_expand(vals, h0, lane_head):
    out = jnp.zeros((vals.shape[0], GROUP_W), F32)
    for j in range(HEADS_PER_GROUP):
        out = jnp.where(lane_head == j, vals[:, h0 + j:h0 + j + 1], out)
    return out


def _ssd_kernel(z_ref, xbc_ref, dt_ref, cw_ref, cb_ref, dtb_ref, alog_ref, dsk_ref, norm_ref, tri_ref,
                shift_ref, y_ref, sout_ref, cout_ref, hist_scr, s_scr, o_scr, *, r):
    c = pl.program_id(1)
    last = pl.num_programs(1) - 1

    @pl.when(c == 0)
    def _():
        hist_scr[0:CONV_PAD, :] = jnp.zeros((CONV_PAD, SSM_CONV_CH), F32)
        s_scr[...] = jnp.zeros_like(s_scr)

    u_bf = xbc_ref[...]
    u = u_bf.astype(F32)
    hist_scr[CONV_PAD:2 * CONV_PAD, :] = u[0:CONV_PAD]
    conv = cb_ref[...] + cw_ref[SSM_CONV - 1:SSM_CONV, :] * u
    head = cb_ref[...]
    for w in range(SSM_CONV):
        off = CONV_PAD - (SSM_CONV - 1) + w
        head = head + cw_ref[w:w + 1, :] * hist_scr[off:off + CONV_PAD, :]
    for back in range(1, SSM_CONV):
        conv = conv + cw_ref[SSM_CONV - 1 - back:SSM_CONV - back, :] * _dot(shift_ref[back - 1], u_bf)
    conv = jnp.concatenate([head, conv[CONV_PAD:]], axis=0)

    @pl.when(c == last)
    def _():
        cout_ref[0] = u[r - (SSM_CONV - 1):r, :]

    hist_scr[0:CONV_PAD, :] = u[r - CONV_PAD:r]

    xc = _silu(conv)
    xs = xc[:, :SSM_INNER]
    bm = xc[:, SSM_INNER:SSM_INNER + SSM_GROUPS * SSM_STATE]
    cm = xc[:, SSM_INNER + SSM_GROUPS * SSM_STATE:]

    dt_all = _softplus(dt_ref[...] + dtb_ref[...])
    rate = -jnp.exp(alog_ref[...]) * LOG2E
    rc = SSD_CHUNK
    ri = lax.broadcasted_iota(jnp.int32, (rc, rc), 0)
    ci = lax.broadcasted_iota(jnp.int32, (rc, rc), 1)
    causal = ri >= ci
    lane_head = lax.broadcasted_iota(jnp.int32, (rc, GROUP_W), 1) // SSM_HEADDIM
    lane_head1 = lax.broadcasted_iota(jnp.int32, (1, GROUP_W), 1) // SSM_HEADDIM
    groups = range(SSM_GROUPS)

    for ck in range(r // rc):
        rows = slice(ck * rc, (ck + 1) * rc)
        dt = dt_all[rows]
        g2 = _dot3_left(tri_ref[...], dt * rate)
        g2_last = g2[rc - 1:rc, :]
        g2_t = g2.T
        dt_t = dt.T
        e_cum = jnp.exp2(g2)
        w_in = dt * jnp.exp2(g2_last - g2)
        e_last = jnp.exp2(g2_last)

        cg = [cm[rows, g * SSM_STATE:(g + 1) * SSM_STATE].astype(BF16) for g in groups]
        bg = [bm[rows, g * SSM_STATE:(g + 1) * SSM_STATE].astype(BF16) for g in groups]
        xs_g = [xs[rows, g * GROUP_W:(g + 1) * GROUP_W] for g in groups]
        xs_b = [x.astype(BF16) for x in xs_g]
        s_old = [s_scr[g] for g in groups]
        cb = [jnp.where(causal, _dot_nt(cg[g], bg[g]), 0.0) for g in groups]
        carried = [_dot(cg[g], s_old[g].astype(BF16)) for g in groups]
        for g in groups:
            h0 = g * HEADS_PER_GROUP
            o_g = _head_expand(e_cum, h0, lane_head) * carried[g]
            for j in range(HEADS_PER_GROUP):
                h = h0 + j
                dec = jnp.exp2(jnp.minimum(g2[:, h:h + 1] - g2_t[h:h + 1, :], 0.0))
                att = (cb[g] * dec * dt_t[h:h + 1, :]).astype(BF16)
                o_g = o_g + jnp.where(lane_head == j, _dot(att, xs_b[g]), 0.0)
            s_scr[g] = (s_old[g] * _head_expand(e_last, h0, lane_head1)
                        + _dot_tn(bg[g], (xs_g[g] * _head_expand(w_in, h0, lane_head)).astype(BF16)))
            o_scr[rows, g * GROUP_W:(g + 1) * GROUP_W] = o_g + dsk_ref[:, g * GROUP_W:(g + 1) * GROUP_W] * xs_g[g]

    y_ref[...] = _rms(o_scr[...] * _silu(z_ref[...].astype(F32)), norm_ref[...]).astype(y_ref.dtype)

    @pl.when(c == last)
    def _():
        for h in range(SSM_HEADS):
            j = h % HEADS_PER_GROUP
            sout_ref[0, h] = s_scr[h // HEADS_PER_GROUP][:, j * SSM_HEADDIM:(j + 1) * SSM_HEADDIM]


def _shift_ones(r):
    i = np.arange(r)
    mats = [(i[:, None] - i[None, :]) == k for k in range(1, SSM_CONV)]
    return jnp.asarray(np.stack(mats).astype(np.float32), BF16)


def _pad_lanes(v):
    return jnp.pad(v.astype(F32), (0, LANES - v.shape[0])).reshape(1, LANES)


def _ssd_prompt(proj, cols, dt, conv_w, conv_b, dt_bias, a_log, dsk_exp, norm, batch, seq):
    r = CHUNK_ROWS
    nc = seq // r
    row = lambda b, c: b * nc + c
    vec = lambda n: pl.BlockSpec((1, n), lambda b, c: (0, 0))
    return pl.pallas_call(
        functools.partial(_ssd_kernel, r=r),
        grid=(batch, nc),
        in_specs=[
            pl.BlockSpec((r, 1024), lambda b, c: (row(b, c), cols.z)),
            pl.BlockSpec((r, SSM_CONV_CH), lambda b, c: (row(b, c), cols.xbc // 2)),
            pl.BlockSpec((r, DT_PAD), lambda b, c: (row(b, c), 0)),
            pl.BlockSpec((SSM_CONV, SSM_CONV_CH), lambda b, c: (0, 0)),
            vec(SSM_CONV_CH), vec(LANES), vec(LANES), vec(SSM_INNER), vec(SSM_INNER),
            pl.BlockSpec((SSD_CHUNK, SSD_CHUNK), lambda b, c: (0, 0)),
            pl.BlockSpec((SSM_CONV - 1, r, r), lambda b, c: (0, 0, 0)),
        ],
        out_specs=[
            pl.BlockSpec((r, 1024), lambda b, c: (row(b, c), 0)),
            pl.BlockSpec((1, SSM_HEADS, SSM_STATE, SSM_HEADDIM), lambda b, c: (b, 0, 0, 0)),
            pl.BlockSpec((1, SSM_CONV - 1, SSM_CONV_CH), lambda b, c: (b, 0, 0)),
        ],
        out_shape=[jax.ShapeDtypeStruct((batch * seq, 1024), BF16),
                   jax.ShapeDtypeStruct((batch, SSM_HEADS, SSM_STATE, SSM_HEADDIM), F32),
                   jax.ShapeDtypeStruct((batch, SSM_CONV - 1, SSM_CONV_CH), F32)],
        scratch_shapes=[pltpu.VMEM((2 * CONV_PAD, SSM_CONV_CH), F32),
                        pltpu.VMEM((SSM_GROUPS, SSM_STATE, GROUP_W), F32),
                        pltpu.VMEM((r, SSM_INNER), F32)],
        compiler_params=_cparams(("parallel", "arbitrary")),
        name="ssd_prompt",
    )(proj, proj, dt, conv_w, conv_b.reshape(1, -1), _pad_lanes(dt_bias), _pad_lanes(a_log), dsk_exp, norm,
      _tri_ones(SSD_CHUNK), _shift_ones(r))


def _merge_kernel(x_ref, yr_ref, yh_ref, ym_ref, g0_ref, g1_ref, g2_ref, wb_ref, wo_ref, npost_ref, nnext_ref,
                  o_ref, xn_ref):
    merged = _sigmoid(g0_ref[...].astype(F32)) * _dot(yr_ref[...].astype(BF16), wb_ref[0])
    merged = merged + _sigmoid(g1_ref[...].astype(F32)) * _dot(yh_ref[...].astype(BF16), wb_ref[1])
    merged = merged + _sigmoid(g2_ref[...].astype(F32)) * _dot(ym_ref[...].astype(BF16), wb_ref[2])
    m = _dot(merged.astype(BF16), wo_ref[...])
    out = x_ref[...] + _rms(m, npost_ref[...])
    o_ref[...] = out
    xn_ref[...] = _rms(out, nnext_ref[...]).astype(BF16)


def _merge(x, yr, yh, ym, proj, cols, wb, wo, npost, nnext, tm):
    t = x.shape[0]
    rows = lambda j: pl.BlockSpec((tm, D_MODEL), lambda i: (i, j))
    vec = pl.BlockSpec((1, D_MODEL), lambda i: (0, 0))
    return pl.pallas_call(
        _merge_kernel,
        grid=(t // tm,),
        in_specs=[rows(0), rows(0), rows(0), rows(0), rows(cols.gates), rows(cols.gates + 1), rows(cols.gates + 2),
                  pl.BlockSpec((3, D_MODEL, D_MODEL), lambda i: (0, 0, 0)),
                  pl.BlockSpec((D_MODEL, D_MODEL), lambda i: (0, 0)), vec, vec],
        out_specs=[rows(0), rows(0)],
        out_shape=[jax.ShapeDtypeStruct((t, D_MODEL), F32), jax.ShapeDtypeStruct((t, D_MODEL), BF16)],
        compiler_params=_cparams(("parallel",)),
        name="merge",
    )(x, yr, yh, ym, proj, proj, proj, wb, wo, npost, nnext)


def _ffn_kernel(x_ref, xn_ref, wu_ref, wd_ref, npost_ref, *rest):
    (nnext_ref, o_ref, xnn_ref, acc_scr) = rest if len(rest) == 4 else (None, rest[0], None, rest[1])
    j = pl.program_id(1)

    @pl.when(j == 0)
    def _():
        acc_scr[...] = jnp.zeros_like(acc_scr)

    hid = jnp.square(jnp.maximum(_dot(xn_ref[...], wu_ref[...]), 0.0))
    acc_scr[...] += _dot(hid.astype(BF16), wd_ref[...])

    @pl.when(j == pl.num_programs(1) - 1)
    def _():
        out = x_ref[...] + _rms(acc_scr[...], npost_ref[...])
        o_ref[...] = out
        if xnn_ref is not None:
            xnn_ref[...] = _rms(out, nnext_ref[...]).astype(BF16)


def _ffn(x, xn, npost, nnext, wu, wd, tm):
    t = x.shape[0]
    tf = 1024
    vec = pl.BlockSpec((1, D_MODEL), lambda i, j: (0, 0))
    rows = pl.BlockSpec((tm, D_MODEL), lambda i, j: (i, 0))
    chain = nnext is not None
    outs = pl.pallas_call(
        _ffn_kernel,
        grid=(t // tm, D_FF // tf),
        in_specs=[rows, rows,
                  pl.BlockSpec((D_MODEL, tf), lambda i, j: (0, j)),
                  pl.BlockSpec((tf, D_MODEL), lambda i, j: (j, 0)), vec] + ([vec] if chain else []),
        out_specs=[rows, rows] if chain else [rows],
        out_shape=[jax.ShapeDtypeStruct((t, D_MODEL), F32)]
        + ([jax.ShapeDtypeStruct((t, D_MODEL), BF16)] if chain else []),
        scratch_shapes=[pltpu.VMEM((tm, D_MODEL), F32)],
        compiler_params=_cparams(("parallel", "arbitrary")),
        name="ffn",
    )(x, xn, wu, wd, npost, *([nnext] if chain else []))
    return (outs[0], outs[1]) if chain else (outs[0], None)


def _dec_prep_kernel(proj_ref, dt_ref, cst_ref, cq_ref, sq_ref, ck_ref, sk_ref, lb_ref, cw_ref, cb_ref,
                     dtb_ref, alog_ref, hexp_ref,
                     cret_ref, chg_ref, cssd_ref, xs_ref, xdt_ref, drow_ref, cnew_ref):
    n = proj_ref.shape[0]
    cq, sq, ck, sk = cq_ref[...], sq_ref[...], ck_ref[...], sk_ref[...]
    for h in range(RET_HEADS):
        ks = slice(h * RET_DK, (h + 1) * RET_DK)
        cret_ref[:, ks] = _rope(proj_ref[:, ks], cq, sq)
        k_off = RET_HEADS * RET_DK
        cret_ref[:, k_off + h * RET_DK:k_off + (h + 1) * RET_DK] = _rope(
            proj_ref[:, k_off + h * RET_DK:k_off + (h + 1) * RET_DK], ck, sk)

    lb = lb_ref[...]
    chg_ref[:, 0:1024] = _silu(proj_ref[:, 3072:4096])
    chg_ref[:, 1024:2048] = jnp.clip(lb + (1.0 - lb) * _sigmoid(proj_ref[:, 4096:5120]), HG_CLIP_MIN, 1.0)

    u = proj_ref[:, 8192:8192 + SSM_CONV_CH]
    conv = cb_ref[...] + cw_ref[SSM_CONV - 1:SSM_CONV, :] * u
    for w in range(SSM_CONV - 1):
        conv = conv + cw_ref[w:w + 1, :] * cst_ref[w]
    cnew_ref[0] = cst_ref[1]
    cnew_ref[1] = cst_ref[2]
    cnew_ref[2] = u
    xc = _silu(conv)
    xs = xc[:, :SSM_INNER]
    xs_ref[...] = xs
    cssd_ref[:, 0:512] = xc[:, SSM_INNER + 512:]
    cssd_ref[:, 512:1024] = xc[:, SSM_INNER:SSM_INNER + 512]
    dt = _softplus(dt_ref[...] + dtb_ref[...])
    decay = jnp.exp(dt * (-jnp.exp(alog_ref[...])))
    hexp = hexp_ref[...]
    xdt_ref[...] = xs * _dot3_right(dt, hexp)
    drow_ref[...] = _dot3_right(decay, hexp)
    del n


def _dec_prep(proj, dt, conv_state_t, lb, conv_w, conv_b, dt_bias, a_log):
    n = proj.shape[0]
    cq, sq, ck, sk = _rope_tables(np.full((n,), PAST_LEN))
    hexp = np.zeros((LANES, SSM_INNER), np.float32)
    for h in range(SSM_HEADS):
        hexp[h, h * SSM_HEADDIM:(h + 1) * SSM_HEADDIM] = 1.0
    outs = [(n, 1024), (n, 2048), (n, 1024), (n, 1024), (n, 1024), (n, 1024)]
    return pl.pallas_call(
        _dec_prep_kernel,
        out_shape=[jax.ShapeDtypeStruct(s, F32) for s in outs]
        + [jax.ShapeDtypeStruct((SSM_CONV - 1, n, SSM_CONV_CH), F32)],
        compiler_params=pltpu.CompilerParams(vmem_limit_bytes=VMEM_LIMIT),
        name="dec_prep",
    )(proj, dt, conv_state_t, cq, sq, ck, sk, lb, conv_w, conv_b.reshape(1, -1), _pad_lanes(dt_bias),
      _pad_lanes(a_log), jnp.asarray(hexp, BF16))


def _cols(a, tb):
    n, ch = a.shape
    return a.reshape(n // tb, tb, ch).transpose(0, 2, 1)


def _state_update_call(body, name, layer, state, carried, n_tokens, heads_per_step, in_specs, operands,
                       scratch):
    depth, _, heads, sk, sv = state.shape
    tb = DEC_TOKENS
    if carried is None:
        src = state
        sspec = pl.BlockSpec((depth, tb, heads_per_step, sk, sv), lambda t, j: (0, t, j, 0, 0))
        own, keep, aliases = layer, tuple(d for d in range(depth) if d != layer), {}
    else:
        src = carried
        sspec = pl.BlockSpec((1, tb, heads_per_step, sk, sv), lambda t, j: (layer, t, j, 0, 0))
        own, keep, aliases = 0, (), {0: 1}
    return pl.pallas_call(
        functools.partial(body, own=own, keep=keep),
        grid=(n_tokens // tb, heads // heads_per_step),
        in_specs=[sspec] + in_specs,
        out_specs=[pl.BlockSpec((tb, 1024), lambda t, j: (t, 0)), sspec],
        out_shape=[jax.ShapeDtypeStruct((n_tokens, 1024), F32), jax.ShapeDtypeStruct(state.shape, F32)],
        input_output_aliases=aliases,
        scratch_shapes=[scratch],
        compiler_params=_cparams(("parallel", "arbitrary")),
        name=name,
    )(src, *operands)


def _ret_dec_kernel(s_ref, ct_ref, v_ref, rg_ref, gain_ref, y_ref, so_ref, o_scr, *, gamma, own, keep):
    for i in range(DEC_TOKENS):
        for h in range(RET_HEADS):
            qcol = ct_ref[0, h * RET_DK:(h + 1) * RET_DK, i:i + 1]
            kcol = ct_ref[0, (RET_HEADS + h) * RET_DK:(RET_HEADS + h + 1) * RET_DK, i:i + 1]
            vs = slice(h * RET_DV, (h + 1) * RET_DV)
            s_new = gamma[h] * s_ref[own, i, h] + kcol * v_ref[i:i + 1, vs]
            so_ref[own, i, h] = s_new
            o_scr[i:i + 1, vs] = jnp.sum(qcol * s_new, axis=0, keepdims=True)
    for j in keep:
        so_ref[j] = s_ref[j]
    for h in range(RET_HEADS):
        vs = slice(h * RET_DV, (h + 1) * RET_DV)
        y_ref[:, vs] = _silu(rg_ref[:, vs]) * _head_layernorm(o_scr[:, vs], gain_ref[:, vs])


def _ret_dec(layer, state, carried, c_ret, proj, gain):
    tb = DEC_TOKENS
    gamma = tuple(float(v) for v in np.exp(_ret_log_gamma()))
    return _state_update_call(
        functools.partial(_ret_dec_kernel, gamma=gamma), "ret_dec", layer, state, carried, proj.shape[0],
        RET_HEADS,
        [pl.BlockSpec((1, 2 * RET_HEADS * RET_DK, tb), lambda t, j: (t, 0, 0)),
         pl.BlockSpec((tb, 1024), lambda t, j: (t, 1)), pl.BlockSpec((tb, 1024), lambda t, j: (t, 2)),
         pl.BlockSpec((1, 1024), lambda t, j: (0, 0))],
        (_cols(c_ret, tb), proj, proj, gain), pltpu.VMEM((tb, 1024), F32))


def _hgrn_dec_kernel(s_ref, ct_ref, v_ref, hg_ref, gain_ref, y_ref, so_ref, o_scr, *, own, keep):
    for i in range(DEC_TOKENS):
        for h in range(HG_HEADS):
            sl = slice(h * LANES, (h + 1) * LANES)
            qcol = ct_ref[0, h * HG_EXPAND:(h + 1) * HG_EXPAND, i:i + 1]
            fcol = ct_ref[0, (HG_HEADS + h) * HG_EXPAND:(HG_HEADS + h + 1) * HG_EXPAND, i:i + 1]
            s_new = fcol * s_ref[own, i, h] + (1.0 - fcol) * v_ref[i:i + 1, sl]
            so_ref[own, i, h] = s_new
            o_scr[i:i + 1, sl] = jnp.sum(qcol * s_new, axis=0, keepdims=True)
    for j in keep:
        so_ref[j] = s_ref[j]
    for h in range(HG_HEADS):
        sl = slice(h * LANES, (h + 1) * LANES)
        o = o_scr[:, sl]
        o = o * lax.rsqrt(jnp.mean(o * o, axis=-1, keepdims=True) + EPS) * gain_ref[:, sl]
        y_ref[:, sl] = o * _sigmoid(hg_ref[:, sl])


def _hgrn_dec(layer, state, carried, c_hg, proj, gain):
    tb = DEC_TOKENS
    return _state_update_call(
        _hgrn_dec_kernel, "hgrn_dec", layer, state, carried, proj.shape[0], HG_HEADS,
        [pl.BlockSpec((1, 2 * HG_HEADS * HG_EXPAND, tb), lambda t, j: (t, 0, 0)),
         pl.BlockSpec((tb, 1024), lambda t, j: (t, 5)), pl.BlockSpec((tb, 1024), lambda t, j: (t, 6)),
         pl.BlockSpec((1, 1024), lambda t, j: (0, 0))],
        (_cols(c_hg, tb), proj, proj, gain), pltpu.VMEM((tb, 1024), F32))


def _ssd_dec_kernel(s_ref, c_ref, b_ref, xdt_ref, drow_ref, xs_ref, z_ref, dsk_ref, norm_ref,
                    y_ref, so_ref, o_scr, *, own, keep):
    for g in range(SSM_GROUPS):
        c_rows = c_ref[:, g * SSM_STATE:(g + 1) * SSM_STATE].astype(BF16)
        for i in range(DEC_TOKENS):
            b_row = b_ref[i:i + 1, g * SSM_STATE:(g + 1) * SSM_STATE]
            new = []
            for h in range(g * HEADS_PER_GROUP, (g + 1) * HEADS_PER_GROUP):
                hs = slice(h * SSM_HEADDIM, (h + 1) * SSM_HEADDIM)
                decay = drow_ref[i:i + 1, h * SSM_HEADDIM:h * SSM_HEADDIM + 1]
                s_new = decay * s_ref[own, i, h] + xdt_ref[0, hs, i:i + 1] * b_row
                so_ref[own, i, h] = s_new
                new.append(s_new)
            s_grp = jnp.concatenate(new, axis=0).astype(BF16)
            o_scr[i:i + 1, g * GROUP_W:(g + 1) * GROUP_W] = _dot_nt(c_rows, s_grp)[i:i + 1, :]
    for d in keep:
        so_ref[d] = s_ref[d]
    o = o_scr[...] + dsk_ref[...] * xs_ref[...]
    y_ref[...] = _rms(o * _silu(z_ref[...]), norm_ref[...])


def _ssd_dec(layer, state, carried, c_ssd, xs, xdt, drow, proj, dsk_exp, norm):
    tb = DEC_TOKENS
    rows = lambda col: pl.BlockSpec((tb, 1024), lambda t, j: (t, col))
    half = lambda col: pl.BlockSpec((tb, SSM_GROUPS * SSM_STATE), lambda t, j: (t, col))
    vec = pl.BlockSpec((1, 1024), lambda t, j: (0, 0))
    return _state_update_call(
        _ssd_dec_kernel, "ssd_dec", layer, state, carried, proj.shape[0], SSM_HEADS,
        [half(0), half(1), pl.BlockSpec((1, SSM_INNER, tb), lambda t, j: (t, 0, 0)),
         rows(0), rows(0), rows(7), vec, vec],
        (c_ssd, c_ssd, _cols(xdt, tb), drow, xs, proj, dsk_exp, norm), pltpu.VMEM((tb, 1024), F32))


def kernel(x_prompt, x_sample, state_ret, state_hgrn, state_ssm, state_conv, w_in, ret_norm, hg_norm,
           hg_lb_logits, conv_w, conv_b, dt_bias, a_log, d_skip, ssm_norm, w_branch, w_out, norm_mix_pre,
           norm_mix_post, norm_ffn_pre, norm_ffn_post, w_up, w_down):
    bp, lp, _ = x_prompt.shape
    bs, ls, _ = x_sample.shape
    assert ls == 1 and lp % CHUNK_ROWS == 0 and bs % DEC_TOKENS == 0
    depth = w_in.shape[0]

    lb_w = jax.nn.softmax(hg_lb_logits.astype(F32), axis=0)
    lb_all = jnp.cumsum(lb_w, axis=0) - lb_w[0]

    hp = x_prompt.reshape(bp * lp, D_MODEL)
    hs = x_sample.reshape(bs, D_MODEL)
    outs = [[] for _ in range(5)]
    new_ret = new_hg = new_ssm = None
    ssm_t = jnp.swapaxes(state_ssm, -1, -2)
    vec = lambda a: a.reshape(1, -1).astype(F32)
    hp_normed = hs_normed = None
    for l in range(depth):
        next_gain = vec(norm_mix_pre[l + 1]) if l + 1 < depth else None
        w_main = jnp.concatenate([w_in[l, :, :N_MAIN], w_in[l, :, N_MAIN + SSM_HEADS:]], axis=1).astype(BF16)
        w_dt = jnp.pad(w_in[l, :, N_MAIN:N_MAIN + SSM_HEADS], ((0, 0), (0, DT_PAD - SSM_HEADS))).astype(BF16)
        wb, wo = w_branch[l].astype(BF16), w_out[l].astype(BF16)
        wu, wd = w_up[l].astype(BF16), w_down[l].astype(BF16)
        dsk_exp = jnp.repeat(d_skip[l].astype(F32), SSM_HEADDIM).reshape(1, SSM_INNER)
        lb = vec(lb_all[l])

        proj, dt, xn = _in_proj(hp if hp_normed is None else hp_normed,
                                vec(norm_mix_pre[l]) if hp_normed is None else None,
                                w_main, w_dt, tm=min(2048, bp * lp), out_dtype=BF16, skip_hgrn=True)
        yr, s_ret = _ret_prompt(proj, vec(ret_norm[l]), bp, lp)
        yh, s_hg = _hgrn_prompt(xn, w_main, lb, vec(hg_norm[l]), bp, lp)
        ym, s_ssm, s_conv = _ssd_prompt(proj, COLS_NO_HGRN, dt, conv_w[l], conv_b[l], dt_bias[l], a_log[l],
                                        dsk_exp, vec(ssm_norm[l]), bp, lp)
        hp, xn = _merge(hp, yr, yh, ym, proj, COLS_NO_HGRN, wb, wo, vec(norm_mix_post[l]),
                        vec(norm_ffn_pre[l]), tm=512)
        hp, hp_normed = _ffn(hp, xn, vec(norm_ffn_post[l]), next_gain, wu, wd, tm=1024)
        for lst, val in zip(outs[:4], (s_ret, s_hg, s_ssm, s_conv)):
            lst.append(val)

        proj, dt, _ = _in_proj(hs if hs_normed is None else hs_normed,
                               vec(norm_mix_pre[l]) if hs_normed is None else None,
                               w_main, w_dt, tm=bs, out_dtype=F32, skip_hgrn=False)
        c_ret, c_hg, c_ssd, xs, xdt, drow, conv_new = _dec_prep(
            proj, dt, state_conv[l].transpose(1, 0, 2), lb, conv_w[l], conv_b[l], dt_bias[l], a_log[l])
        yr, new_ret = _ret_dec(l, state_ret, new_ret, c_ret, proj, vec(ret_norm[l]))
        yh, new_hg = _hgrn_dec(l, state_hgrn, new_hg, c_hg, proj, vec(hg_norm[l]))
        ym, new_ssm = _ssd_dec(l, ssm_t, new_ssm, c_ssd, xs, xdt, drow, proj, dsk_exp, vec(ssm_norm[l]))
        hs, xn = _merge(hs, yr, yh, ym, proj, COLS_FULL, wb, wo, vec(norm_mix_post[l]), vec(norm_ffn_pre[l]),
                        tm=bs)
        hs, hs_normed = _ffn(hs, xn, vec(norm_ffn_post[l]), next_gain, wu, wd, tm=bs)
        outs[4].append(conv_new.transpose(1, 0, 2))

    p_ret, p_hg, p_ssm, p_conv, s_conv = [jnp.stack(o) for o in outs]
    return (hp.reshape(bp, lp, D_MODEL), hs.reshape(bs, ls, D_MODEL), p_ret, p_hg, p_ssm, p_conv,
            new_ret, new_hg, jnp.swapaxes(new_ssm, -1, -2), s_conv)
```

```python
import functools
from typing import NamedTuple

import numpy as np
import jax
import jax.numpy as jnp
from jax import lax
from jax.experimental import pallas as pl
from jax.experimental.pallas import tpu as pltpu

F32 = jnp.float32
BF16 = jnp.bfloat16

D_MODEL = 1024
PAST_LEN = 16384
RET_HEADS, RET_DK, RET_DV = 4, 128, 256
ROPE_BASE = 10000.0
HG_HEADS, HG_EXPAND, HG_DV = 8, 128, 128
SSM_HEADS, SSM_HEADDIM, SSM_GROUPS, SSM_STATE, SSM_CONV = 16, 64, 4, 128, 4
SSM_INNER = SSM_HEADS * SSM_HEADDIM
SSM_CONV_CH = SSM_INNER + 2 * SSM_GROUPS * SSM_STATE
N_BRANCH = 3
D_FF = 4 * D_MODEL
EPS = 1e-6
HG_CLIP_MIN = 1e-6

N_MAIN = 10240
N_PROJ = N_MAIN + 3 * D_MODEL
DT_PAD = 128

LANES = 128
SUBLANES = 8
VMEM_LIMIT = 56 * 1024 * 1024

CHUNK_ROWS = 256
DEC_TOKENS = 8


def _cparams(sem):
    return pltpu.CompilerParams(dimension_semantics=sem, vmem_limit_bytes=VMEM_LIMIT)


def _dot(a, b):
    return jnp.dot(a, b, preferred_element_type=F32)


def _dot_nt(a, b):
    return lax.dot_general(a, b, (((1,), (1,)), ((), ())), preferred_element_type=F32)


def _dot_tn(a, b):
    return lax.dot_general(a, b, (((0,), (0,)), ((), ())), preferred_element_type=F32)


def _split3(a):
    hi = a.astype(BF16)
    r1 = a - hi.astype(F32)
    mid = r1.astype(BF16)
    lo = (r1 - mid.astype(F32)).astype(BF16)
    return hi, mid, lo


def _dot3_left(m01, a):
    hi, mid, lo = _split3(a)
    return _dot(m01, hi) + _dot(m01, mid) + _dot(m01, lo)


def _dot3_right(a, m01):
    hi, mid, lo = _split3(a)
    return _dot(hi, m01) + _dot(mid, m01) + _dot(lo, m01)


def _sigmoid(x):
    return 1.0 / (1.0 + jnp.exp(-x))


def _silu(x):
    return x * _sigmoid(x)


def _softplus(x):
    return jnp.maximum(x, 0.0) + jnp.log1p(jnp.exp(-jnp.abs(x)))


def _rms(x, gain):
    return x * lax.rsqrt(jnp.mean(x * x, axis=-1, keepdims=True) + EPS) * gain


PROJ_TILE = 1024
HG_TILE0, HG_TILES = 3, 4


class _Cols(NamedTuple):
    z: int
    xbc: int
    gates: int


COLS_FULL = _Cols(z=7, xbc=8, gates=10)
COLS_NO_HGRN = _Cols(z=7 - HG_TILES, xbc=8 - HG_TILES, gates=10 - HG_TILES)


MAIN_TILES = N_MAIN // PROJ_TILE


def _project_tile(xn_ref, w_ref, wg_ref, o_ref, n_main):
    j = pl.program_id(1)

    @pl.when(j < n_main)
    def _():
        o_ref[...] = _dot_nt(xn_ref[...], w_ref[0].astype(BF16)).astype(o_ref.dtype)

    @pl.when(j >= n_main)
    def _():
        o_ref[...] = _dot_nt(xn_ref[...], wg_ref[...]).astype(o_ref.dtype)


def _in_proj_kernel(x_ref, g_ref, w_ref, wg_ref, wdt_ref, o_ref, dt_ref, xn_ref, *, n_main):
    @pl.when(pl.program_id(1) == 0)
    def _():
        xn_ref[...] = _rms(x_ref[...], g_ref[...]).astype(BF16)
        dt_ref[...] = _dot_nt(xn_ref[...], wdt_ref[...])

    _project_tile(xn_ref, w_ref, wg_ref, o_ref, n_main)


def _in_proj_normed_kernel(xn_ref, w_ref, wg_ref, wdt_ref, o_ref, dt_ref, *, n_main):
    @pl.when(pl.program_id(1) == 0)
    def _():
        dt_ref[...] = _dot_nt(xn_ref[...], wdt_ref[...])

    _project_tile(xn_ref, w_ref, wg_ref, o_ref, n_main)


def _in_proj(x, gain, w_in, layer, w_gates, w_dt, tm, out_dtype, skip_hgrn):
    t = x.shape[0]
    tn = PROJ_TILE
    n_main = MAIN_TILES - (HG_TILES if skip_hgrn else 0)
    n_tiles = n_main + N_BRANCH
    skip = (lambda j: j + jnp.where(j >= HG_TILE0, HG_TILES, 0)) if skip_hgrn else (lambda j: j)
    rows = pl.BlockSpec((tm, D_MODEL), lambda i, j: (i, 0))
    w_specs = [pl.BlockSpec((1, tn, D_MODEL), lambda i, j: (layer, jnp.minimum(skip(j), MAIN_TILES - 1), 0)),
               pl.BlockSpec((tn, D_MODEL), lambda i, j: (jnp.maximum(j - n_main, 0), 0)),
               pl.BlockSpec((DT_PAD, D_MODEL), lambda i, j: (0, 0))]
    out_specs = [pl.BlockSpec((tm, tn), lambda i, j: (i, j)), pl.BlockSpec((tm, DT_PAD), lambda i, j: (i, 0))]
    out_shape = [jax.ShapeDtypeStruct((t, n_tiles * tn), out_dtype), jax.ShapeDtypeStruct((t, DT_PAD), F32)]
    if gain is None:
        proj, dt = pl.pallas_call(
            functools.partial(_in_proj_normed_kernel, n_main=n_main), grid=(t // tm, n_tiles),
            in_specs=[rows] + w_specs, out_specs=out_specs, out_shape=out_shape,
            compiler_params=_cparams(("parallel", "arbitrary")), name="in_proj",
        )(x, w_in, w_gates, w_dt)
        return proj, dt, x
    return pl.pallas_call(
        functools.partial(_in_proj_kernel, n_main=n_main), grid=(t // tm, n_tiles),
        in_specs=[rows, pl.BlockSpec((1, D_MODEL), lambda i, j: (0, 0))] + w_specs,
        out_specs=out_specs + [rows], out_shape=out_shape + [jax.ShapeDtypeStruct((t, D_MODEL), BF16)],
        compiler_params=_cparams(("parallel", "arbitrary")), name="in_proj",
    )(x, gain, w_in, w_gates, w_dt)


def _ret_log_gamma():
    return np.log1p(-np.exp2(-5.0 - np.arange(RET_HEADS, dtype=np.float64)))


def _ret_consts(r):
    lg = _ret_log_gamma()
    i = np.arange(r)
    diff = (i[:, None] - i[None, :]).astype(np.float64)
    dec = np.where(diff >= 0, np.exp(lg[:, None, None] * np.maximum(diff, 0.0)), 0.0)
    qd = np.exp(lg[:, None] * (i + 1.0))
    kd = np.exp(lg[:, None] * (r - 1.0 - i))
    gl = tuple(float(v) for v in np.exp(lg * r))
    bc = lambda a: jnp.asarray(np.broadcast_to(a[:, :, None], (RET_HEADS, r, LANES)), F32)
    return jnp.asarray(dec, F32), bc(qd), bc(kd), gl


def _rope_tables(pos):
    half = RET_DK // 2
    inv = ROPE_BASE ** (-(np.arange(half, dtype=np.float64) / half))
    ang = pos.astype(np.float64)[:, None] * inv[None, :]
    cos, sin = np.cos(ang), np.sin(ang)
    cos2 = np.concatenate([cos, cos], axis=1)
    sin2 = np.concatenate([-sin, sin], axis=1)
    scale = RET_DK ** -0.5
    return (jnp.asarray(cos2, F32), jnp.asarray(sin2, F32),
            jnp.asarray(cos2 * scale, F32), jnp.asarray(sin2 * scale, F32))


def _rope(t, cos2, sin2):
    return t * cos2 + pltpu.roll(t, RET_DK // 2, 1) * sin2


def _head_layernorm(o, gain):
    oc = o - jnp.mean(o, axis=-1, keepdims=True)
    return oc * lax.rsqrt(jnp.mean(oc * oc, axis=-1, keepdims=True) + EPS) * gain


def _ret_kernel(rq_ref, rk_ref, rv_ref, rg_ref, cq_ref, sq_ref, ck_ref, sk_ref, dec_ref, qd_ref, kd_ref,
                gain_ref, y_ref, sout_ref, s_scr, *, gl):
    c = pl.program_id(1)

    @pl.when(c == 0)
    def _():
        s_scr[...] = jnp.zeros_like(s_scr)

    cq, sq, ck, sk = cq_ref[...], sq_ref[...], ck_ref[...], sk_ref[...]
    for h in range(RET_HEADS):
        ks = slice(h * RET_DK, (h + 1) * RET_DK)
        vs = slice(h * RET_DV, (h + 1) * RET_DV)
        q = _rope(rq_ref[:, ks].astype(F32), cq, sq)
        k = _rope(rk_ref[:, ks].astype(F32), ck, sk)
        v = rv_ref[:, vs].astype(BF16)
        att = (_dot_nt(q.astype(BF16), k.astype(BF16)) * dec_ref[h]).astype(BF16)
        s_old = s_scr[h]
        o = _dot(att, v) + _dot((q * qd_ref[h]).astype(BF16), s_old.astype(BF16))
        s_scr[h] = gl[h] * s_old + _dot_tn((k * kd_ref[h]).astype(BF16), v)
        y = _silu(rg_ref[:, vs].astype(F32)) * _head_layernorm(o, gain_ref[:, vs])
        y_ref[:, vs] = y.astype(y_ref.dtype)

    @pl.when(c == pl.num_programs(1) - 1)
    def _():
        sout_ref[0] = s_scr[...]


def _ret_prompt(proj, gain, batch, seq):
    r = CHUNK_ROWS
    nc = seq // r
    dec, qd, kd, gl = _ret_consts(r)
    cq, sq, ck, sk = _rope_tables(np.arange(seq))
    row = lambda b, c: b * nc + c
    tab = pl.BlockSpec((r, RET_DK), lambda b, c: (c, 0))
    const3 = lambda shape: pl.BlockSpec(shape, lambda b, c: (0, 0, 0))
    return pl.pallas_call(
        functools.partial(_ret_kernel, gl=gl),
        grid=(batch, nc),
        in_specs=[
            pl.BlockSpec((r, 512), lambda b, c: (row(b, c), 0)),
            pl.BlockSpec((r, 512), lambda b, c: (row(b, c), 1)),
            pl.BlockSpec((r, 1024), lambda b, c: (row(b, c), 1)),
            pl.BlockSpec((r, 1024), lambda b, c: (row(b, c), 2)),
            tab, tab, tab, tab,
            const3((RET_HEADS, r, r)), const3((RET_HEADS, r, LANES)), const3((RET_HEADS, r, LANES)),
            pl.BlockSpec((1, 1024), lambda b, c: (0, 0)),
        ],
        out_specs=[
            pl.BlockSpec((r, 1024), lambda b, c: (row(b, c), 0)),
            pl.BlockSpec((1, RET_HEADS, RET_DK, RET_DV), lambda b, c: (b, 0, 0, 0)),
        ],
        out_shape=[jax.ShapeDtypeStruct((batch * seq, 1024), BF16),
                   jax.ShapeDtypeStruct((batch, RET_HEADS, RET_DK, RET_DV), F32)],
        scratch_shapes=[pltpu.VMEM((RET_HEADS, RET_DK, RET_DV), F32)],
        compiler_params=_cparams(("parallel", "arbitrary")),
        name="ret_prompt",
    )(proj, proj, proj, proj, cq, sq, ck, sk, dec, qd, kd, gain)


HG_CHUNK = 128
HG_LEVELS = (64, 32, 16, 8)
HG_TILE_REF = SUBLANES // 2 - 1
LOG2E = 1.4426950408889634


def _hg_masks():
    i = np.arange(HG_CHUNK)
    out = [(i[:, None] // (2 * b)) == (i[None, :] // (2 * b)) for b in HG_LEVELS[1:]]
    out.append(((i[:, None] // SUBLANES) == (i[None, :] // SUBLANES)) & (i[:, None] >= i[None, :]))
    return jnp.asarray(np.stack(out).astype(np.float32))


def _tri_ones(r):
    i = np.arange(r)
    return jnp.asarray((i[:, None] >= i[None, :]).astype(np.float32), BF16)


def _halves(a, nb):
    a5 = a.reshape(HG_CHUNK // (2 * nb * SUBLANES), 2, nb, SUBLANES, a.shape[-1])
    return a5[:, 0], a5[:, 1]


def _join_halves(lower, upper):
    return jnp.stack([lower, upper], axis=1).reshape(HG_CHUNK, lower.shape[-1])


def _tile_row_bcast(a, j):
    a3 = a.reshape(HG_CHUNK // SUBLANES, SUBLANES, a.shape[-1])
    return jnp.broadcast_to(a3[:, j:j + 1, :], a3.shape).reshape(a.shape)


def _per_head_nt(a, b):
    return [_dot_nt(a[:, h * LANES:(h + 1) * LANES], b[:, h * LANES:(h + 1) * LANES]) for h in range(HG_HEADS)]


def _hg_chunk_att(q, kk, g2, mask_ref):
    tile_last = _tile_row_bcast(g2, SUBLANES - 1)
    att = None
    for lvl, b in enumerate(HG_LEVELS):
        nb = b // SUBLANES
        _, q_up = _halves(q, nb)
        k_lo, _ = _halves(kk, nb)
        g_lo, g_up = _halves(g2, nb)
        ref = _halves(tile_last, nb)[0][:, nb - 1:nb]
        qe = q_up * jnp.exp2(g_up - ref)
        ke = k_lo * jnp.exp2(ref - g_lo)
        zeros = jnp.zeros_like(qe)
        prods = _per_head_nt(_join_halves(zeros, qe).astype(BF16), _join_halves(ke, zeros).astype(BF16))
        att = prods if lvl == 0 else [a + p * mask_ref[lvl - 1] for a, p in zip(att, prods)]
    d = g2 - _tile_row_bcast(g2, HG_TILE_REF)
    prods = _per_head_nt((q * jnp.exp2(d)).astype(BF16), (kk * jnp.exp2(-d)).astype(BF16))
    same_tile = mask_ref[len(HG_LEVELS) - 1] > 0.5
    return [a + jnp.where(same_tile, p, 0.0) for a, p in zip(att, prods)]


def _hgrn_kernel(xc_ref, xnx_ref, w0_ref, w1_ref, w2_ref, w3_ref, lb_ref, gain_ref, tri_ref, mask_ref,
                 y_ref, sout_ref, st_scr, pj_scr, *, r):
    c = pl.program_id(1)
    step = pl.program_id(0) * pl.num_programs(1) + c
    w_refs = (w0_ref, w1_ref, w2_ref, w3_ref)

    def project(xn, slot):
        for t, w_ref in enumerate(w_refs):
            pj_scr[slot, :, t * PROJ_TILE:(t + 1) * PROJ_TILE] = _dot_nt(xn, w_ref[...]).astype(BF16)

    @pl.when(step == 0)
    def _():
        project(xc_ref[...], 0)

    @pl.when(c == 0)
    def _():
        st_scr[...] = jnp.zeros_like(st_scr)

    def block(cur):
        project(xnx_ref[...], 1 - cur)
        hq_ref, hf_ref, hi_ref, hg_ref = (
            pj_scr.at[cur, :, t * PROJ_TILE:(t + 1) * PROJ_TILE] for t in range(HG_TILES))
        lb = lb_ref[...]
        for ck in range(r // HG_CHUNK):
            rows = slice(ck * HG_CHUNK, (ck + 1) * HG_CHUNK)
            q = _silu(hq_ref[rows, :].astype(F32))
            f = jnp.clip(lb + (1.0 - lb) * _sigmoid(hf_ref[rows, :].astype(F32)), HG_CLIP_MIN, 1.0)
            kk = 1.0 - f
            g2 = _dot3_left(tri_ref[...], jnp.log(f)) * LOG2E
            g2_last = g2[HG_CHUNK - 1:HG_CHUNK, :]
            att = _hg_chunk_att(q, kk, g2, mask_ref)
            q_in = (q * jnp.exp2(g2)).astype(BF16)
            k_out = (kk * jnp.exp2(g2_last - g2)).astype(BF16)
            e_last = jnp.exp2(g2_last)
            gate = _sigmoid(hg_ref[rows, :].astype(F32))
            for h in range(HG_HEADS):
                sl = slice(h * LANES, (h + 1) * LANES)
                v = hi_ref[rows, sl]
                st_old = st_scr[h]
                o = _dot(att[h].astype(BF16), v) + _dot_nt(q_in[:, sl], st_old.astype(BF16))
                st_scr[h] = st_old * e_last[:, sl] + _dot_tn(v, k_out[:, sl])
                o = o * lax.rsqrt(jnp.mean(o * o, axis=-1, keepdims=True) + EPS) * gain_ref[:, sl]
                y_ref[rows, sl] = (o * gate[:, sl]).astype(y_ref.dtype)

    for parity in range(2):
        pl.when(lax.rem(step, 2) == parity)(functools.partial(block, parity))

    @pl.when(c == pl.num_programs(1) - 1)
    def _():
        for h in range(HG_HEADS):
            sout_ref[0, h] = st_scr[h].T


def _hgrn_prompt(xn, w_hg, lb, gain, batch, seq):
    r = CHUNK_ROWS
    nc = seq // r
    masks = _hg_masks()
    row = lambda b, c: b * nc + c
    last_block = batch * nc - 1
    vec = pl.BlockSpec((1, 1024), lambda b, c: (0, 0))
    w_tile = lambda t: pl.BlockSpec((PROJ_TILE, D_MODEL), lambda b, c: (t, 0))
    return pl.pallas_call(
        functools.partial(_hgrn_kernel, r=r),
        grid=(batch, nc),
        in_specs=[pl.BlockSpec((r, D_MODEL), lambda b, c: (row(b, c), 0)),
                  pl.BlockSpec((r, D_MODEL), lambda b, c: (jnp.minimum(row(b, c) + 1, last_block), 0)),
                  w_tile(0), w_tile(1), w_tile(2), w_tile(3), vec, vec,
                  pl.BlockSpec((HG_CHUNK, HG_CHUNK), lambda b, c: (0, 0)),
                  pl.BlockSpec(masks.shape, lambda b, c: (0, 0, 0))],
        out_specs=[
            pl.BlockSpec((r, 1024), lambda b, c: (row(b, c), 0)),
            pl.BlockSpec((1, HG_HEADS, HG_EXPAND, HG_DV), lambda b, c: (b, 0, 0, 0)),
        ],
        out_shape=[jax.ShapeDtypeStruct((batch * seq, 1024), BF16),
                   jax.ShapeDtypeStruct((batch, HG_HEADS, HG_EXPAND, HG_DV), F32)],
        scratch_shapes=[pltpu.VMEM((HG_HEADS, HG_DV, HG_EXPAND), F32),
                        pltpu.VMEM((2, r, HG_TILES * PROJ_TILE), BF16)],
        compiler_params=_cparams(("arbitrary", "arbitrary")),
        name="hgrn_prompt",
    )(xn, xn, w_hg, w_hg, w_hg, w_hg, lb, gain, _tri_ones(HG_CHUNK), masks)


HEADS_PER_GROUP = SSM_HEADS // SSM_GROUPS
GROUP_W = HEADS_PER_GROUP * SSM_HEADDIM
CONV_PAD = SUBLANES
SSD_CHUNK = 128


def _head_expand(vals, h0, lane_head):
    out = jnp.zeros((vals.shape[0], GROUP_W), F32)
    for j in range(HEADS_PER_GROUP):
        out = jnp.where(lane_head == j, vals[:, h0 + j:h0 + j + 1], out)
    return out


def _ssd_kernel(z_ref, xbc_ref, dt_ref, cw_ref, cb_ref, dtb_ref, alog_ref, dsk_ref, norm_ref, tri_ref,
                shift_ref, y_ref, sout_ref, cout_ref, hist_scr, s_scr, o_scr, *, r):
    c = pl.program_id(1)
    last = pl.num_programs(1) - 1

    @pl.when(c == 0)
    def _():
        hist_scr[0:CONV_PAD, :] = jnp.zeros((CONV_PAD, SSM_CONV_CH), F32)
        s_scr[...] = jnp.zeros_like(s_scr)

    u_bf = xbc_ref[...]
    u = u_bf.astype(F32)
    hist_scr[CONV_PAD:2 * CONV_PAD, :] = u[0:CONV_PAD]
    conv = cb_ref[...] + cw_ref[SSM_CONV - 1:SSM_CONV, :] * u
    head = cb_ref[...]
    for w in range(SSM_CONV):
        off = CONV_PAD - (SSM_CONV - 1) + w
        head = head + cw_ref[w:w + 1, :] * hist_scr[off:off + CONV_PAD, :]
    for back in range(1, SSM_CONV):
        conv = conv + cw_ref[SSM_CONV - 1 - back:SSM_CONV - back, :] * _dot(shift_ref[back - 1], u_bf)
    conv = jnp.concatenate([head, conv[CONV_PAD:]], axis=0)

    @pl.when(c == last)
    def _():
        cout_ref[0] = u[r - (SSM_CONV - 1):r, :]

    hist_scr[0:CONV_PAD, :] = u[r - CONV_PAD:r]

    xc = _silu(conv)
    xs = xc[:, :SSM_INNER]
    bm = xc[:, SSM_INNER:SSM_INNER + SSM_GROUPS * SSM_STATE]
    cm = xc[:, SSM_INNER + SSM_GROUPS * SSM_STATE:]

    dt_all = _softplus(dt_ref[...] + dtb_ref[...])
    rate = -jnp.exp(alog_ref[...]) * LOG2E
    rc = SSD_CHUNK
    ri = lax.broadcasted_iota(jnp.int32, (rc, rc), 0)
    ci = lax.broadcasted_iota(jnp.int32, (rc, rc), 1)
    causal = ri >= ci
    lane_head = lax.broadcasted_iota(jnp.int32, (rc, GROUP_W), 1) // SSM_HEADDIM
    lane_head1 = lax.broadcasted_iota(jnp.int32, (1, GROUP_W), 1) // SSM_HEADDIM
    groups = range(SSM_GROUPS)

    for ck in range(r // rc):
        rows = slice(ck * rc, (ck + 1) * rc)
        dt = dt_all[rows]
        g2 = _dot3_left(tri_ref[...], dt * rate)
        g2_last = g2[rc - 1:rc, :]
        g2_t = g2.T
        dt_t = dt.T
        e_cum = jnp.exp2(g2)
        w_in = dt * jnp.exp2(g2_last - g2)
        e_last = jnp.exp2(g2_last)

        cg = [cm[rows, g * SSM_STATE:(g + 1) * SSM_STATE].astype(BF16) for g in groups]
        bg = [bm[rows, g * SSM_STATE:(g + 1) * SSM_STATE].astype(BF16) for g in groups]
        xs_g = [xs[rows, g * GROUP_W:(g + 1) * GROUP_W] for g in groups]
        xs_b = [x.astype(BF16) for x in xs_g]
        s_old = [s_scr[g] for g in groups]
        cb = [jnp.where(causal, _dot_nt(cg[g], bg[g]), 0.0) for g in groups]
        carried = [_dot(cg[g], s_old[g].astype(BF16)) for g in groups]
        for g in groups:
            h0 = g * HEADS_PER_GROUP
            o_g = _head_expand(e_cum, h0, lane_head) * carried[g]
            for j in range(HEADS_PER_GROUP):
                h = h0 + j
                dec = jnp.exp2(jnp.minimum(g2[:, h:h + 1] - g2_t[h:h + 1, :], 0.0))
                att = (cb[g] * dec * dt_t[h:h + 1, :]).astype(BF16)
                o_g = o_g + jnp.where(lane_head == j, _dot(att, xs_b[g]), 0.0)
            s_scr[g] = (s_old[g] * _head_expand(e_last, h0, lane_head1)
                        + _dot_tn(bg[g], (xs_g[g] * _head_expand(w_in, h0, lane_head)).astype(BF16)))
            o_scr[rows, g * GROUP_W:(g + 1) * GROUP_W] = o_g + dsk_ref[:, g * GROUP_W:(g + 1) * GROUP_W] * xs_g[g]

    y_ref[...] = _rms(o_scr[...] * _silu(z_ref[...].astype(F32)), norm_ref[...]).astype(y_ref.dtype)

    @pl.when(c == last)
    def _():
        for h in range(SSM_HEADS):
            j = h % HEADS_PER_GROUP
            sout_ref[0, h] = s_scr[h // HEADS_PER_GROUP][:, j * SSM_HEADDIM:(j + 1) * SSM_HEADDIM]


def _shift_ones(r):
    i = np.arange(r)
    mats = [(i[:, None] - i[None, :]) == k for k in range(1, SSM_CONV)]
    return jnp.asarray(np.stack(mats).astype(np.float32), BF16)


def _pad_lanes(v):
    return jnp.pad(v.astype(F32), (0, LANES - v.shape[0])).reshape(1, LANES)


def _ssd_prompt(proj, cols, dt, conv_w, conv_b, dt_bias, a_log, dsk_exp, norm, batch, seq):
    r = CHUNK_ROWS
    nc = seq // r
    row = lambda b, c: b * nc + c
    vec = lambda n: pl.BlockSpec((1, n), lambda b, c: (0, 0))
    return pl.pallas_call(
        functools.partial(_ssd_kernel, r=r),
        grid=(batch, nc),
        in_specs=[
            pl.BlockSpec((r, 1024), lambda b, c: (row(b, c), cols.z)),
            pl.BlockSpec((r, SSM_CONV_CH), lambda b, c: (row(b, c), cols.xbc // 2)),
            pl.BlockSpec((r, DT_PAD), lambda b, c: (row(b, c), 0)),
            pl.BlockSpec((SSM_CONV, SSM_CONV_CH), lambda b, c: (0, 0)),
            vec(SSM_CONV_CH), vec(LANES), vec(LANES), vec(SSM_INNER), vec(SSM_INNER),
            pl.BlockSpec((SSD_CHUNK, SSD_CHUNK), lambda b, c: (0, 0)),
            pl.BlockSpec((SSM_CONV - 1, r, r), lambda b, c: (0, 0, 0)),
        ],
        out_specs=[
            pl.BlockSpec((r, 1024), lambda b, c: (row(b, c), 0)),
            pl.BlockSpec((1, SSM_HEADS, SSM_STATE, SSM_HEADDIM), lambda b, c: (b, 0, 0, 0)),
            pl.BlockSpec((1, SSM_CONV - 1, SSM_CONV_CH), lambda b, c: (b, 0, 0)),
        ],
        out_shape=[jax.ShapeDtypeStruct((batch * seq, 1024), BF16),
                   jax.ShapeDtypeStruct((batch, SSM_HEADS, SSM_STATE, SSM_HEADDIM), F32),
                   jax.ShapeDtypeStruct((batch, SSM_CONV - 1, SSM_CONV_CH), F32)],
        scratch_shapes=[pltpu.VMEM((2 * CONV_PAD, SSM_CONV_CH), F32),
                        pltpu.VMEM((SSM_GROUPS, SSM_STATE, GROUP_W), F32),
                        pltpu.VMEM((r, SSM_INNER), F32)],
        compiler_params=_cparams(("parallel", "arbitrary")),
        name="ssd_prompt",
    )(proj, proj, dt, conv_w, conv_b.reshape(1, -1), _pad_lanes(dt_bias), _pad_lanes(a_log), dsk_exp, norm,
      _tri_ones(SSD_CHUNK), _shift_ones(r))


def _merge_kernel(x_ref, yr_ref, yh_ref, ym_ref, g0_ref, g1_ref, g2_ref, wb_ref, wo_ref, npost_ref, nnext_ref,
                  o_ref, xn_ref):
    merged = _sigmoid(g0_ref[...].astype(F32)) * _dot(yr_ref[...].astype(BF16), wb_ref[0])
    merged = merged + _sigmoid(g1_ref[...].astype(F32)) * _dot(yh_ref[...].astype(BF16), wb_ref[1])
    merged = merged + _sigmoid(g2_ref[...].astype(F32)) * _dot(ym_ref[...].astype(BF16), wb_ref[2])
    m = _dot(merged.astype(BF16), wo_ref[...])
    out = x_ref[...] + _rms(m, npost_ref[...])
    o_ref[...] = out
    xn_ref[...] = _rms(out, nnext_ref[...]).astype(BF16)


def _merge(x, yr, yh, ym, proj, cols, wb, wo, npost, nnext, tm):
    t = x.shape[0]
    rows = lambda j: pl.BlockSpec((tm, D_MODEL), lambda i: (i, j))
    vec = pl.BlockSpec((1, D_MODEL), lambda i: (0, 0))
    return pl.pallas_call(
        _merge_kernel,
        grid=(t // tm,),
        in_specs=[rows(0), rows(0), rows(0), rows(0), rows(cols.gates), rows(cols.gates + 1), rows(cols.gates + 2),
                  pl.BlockSpec((3, D_MODEL, D_MODEL), lambda i: (0, 0, 0)),
                  pl.BlockSpec((D_MODEL, D_MODEL), lambda i: (0, 0)), vec, vec],
        out_specs=[rows(0), rows(0)],
        out_shape=[jax.ShapeDtypeStruct((t, D_MODEL), F32), jax.ShapeDtypeStruct((t, D_MODEL), BF16)],
        compiler_params=_cparams(("parallel",)),
        name="merge",
    )(x, yr, yh, ym, proj, proj, proj, wb, wo, npost, nnext)


def _ffn_kernel(x_ref, xn_ref, wu_ref, wd_ref, npost_ref, *rest):
    (nnext_ref, o_ref, xnn_ref, acc_scr) = rest if len(rest) == 4 else (None, rest[0], None, rest[1])
    j = pl.program_id(1)

    @pl.when(j == 0)
    def _():
        acc_scr[...] = jnp.zeros_like(acc_scr)

    hid = jnp.square(jnp.maximum(_dot(xn_ref[...], wu_ref[0].astype(BF16)), 0.0))
    acc_scr[...] += _dot(hid.astype(BF16), wd_ref[0].astype(BF16))

    @pl.when(j == pl.num_programs(1) - 1)
    def _():
        out = x_ref[...] + _rms(acc_scr[...], npost_ref[...])
        o_ref[...] = out
        if xnn_ref is not None:
            xnn_ref[...] = _rms(out, nnext_ref[...]).astype(BF16)


def _ffn(x, xn, npost, nnext, w_up, w_down, layer, tm):
    t = x.shape[0]
    tf = 1024
    vec = pl.BlockSpec((1, D_MODEL), lambda i, j: (0, 0))
    rows = pl.BlockSpec((tm, D_MODEL), lambda i, j: (i, 0))
    chain = nnext is not None
    wu, wd = w_up, w_down
    outs = pl.pallas_call(
        _ffn_kernel,
        grid=(t // tm, D_FF // tf),
        in_specs=[rows, rows,
                  pl.BlockSpec((1, D_MODEL, tf), lambda i, j: (layer, 0, j)),
                  pl.BlockSpec((1, tf, D_MODEL), lambda i, j: (layer, j, 0)), vec] + ([vec] if chain else []),
        out_specs=[rows, rows] if chain else [rows],
        out_shape=[jax.ShapeDtypeStruct((t, D_MODEL), F32)]
        + ([jax.ShapeDtypeStruct((t, D_MODEL), BF16)] if chain else []),
        scratch_shapes=[pltpu.VMEM((tm, D_MODEL), F32)],
        compiler_params=_cparams(("parallel", "arbitrary")),
        name="ffn",
    )(x, xn, wu, wd, npost, *([nnext] if chain else []))
    return (outs[0], outs[1]) if chain else (outs[0], None)


def _dec_prep_kernel(proj_ref, dt_ref, cst_ref, cq_ref, sq_ref, ck_ref, sk_ref, lb_ref, cw_ref, cb_ref,
                     dtb_ref, alog_ref, hexp_ref,
                     cret_ref, chg_ref, cssd_ref, xs_ref, xdt_ref, drow_ref, cnew_ref):
    n = proj_ref.shape[0]
    cq, sq, ck, sk = cq_ref[...], sq_ref[...], ck_ref[...], sk_ref[...]
    for h in range(RET_HEADS):
        ks = slice(h * RET_DK, (h + 1) * RET_DK)
        cret_ref[:, ks] = _rope(proj_ref[:, ks], cq, sq)
        k_off = RET_HEADS * RET_DK
        cret_ref[:, k_off + h * RET_DK:k_off + (h + 1) * RET_DK] = _rope(
            proj_ref[:, k_off + h * RET_DK:k_off + (h + 1) * RET_DK], ck, sk)

    lb = lb_ref[...]
    chg_ref[:, 0:1024] = _silu(proj_ref[:, 3072:4096])
    chg_ref[:, 1024:2048] = jnp.clip(lb + (1.0 - lb) * _sigmoid(proj_ref[:, 4096:5120]), HG_CLIP_MIN, 1.0)

    u = proj_ref[:, 8192:8192 + SSM_CONV_CH]
    conv = cb_ref[...] + cw_ref[SSM_CONV - 1:SSM_CONV, :] * u
    for w in range(SSM_CONV - 1):
        conv = conv + cw_ref[w:w + 1, :] * cst_ref[w]
    cnew_ref[0] = cst_ref[1]
    cnew_ref[1] = cst_ref[2]
    cnew_ref[2] = u
    xc = _silu(conv)
    xs = xc[:, :SSM_INNER]
    xs_ref[...] = xs
    cssd_ref[:, 0:512] = xc[:, SSM_INNER + 512:]
    cssd_ref[:, 512:1024] = xc[:, SSM_INNER:SSM_INNER + 512]
    dt = _softplus(dt_ref[...] + dtb_ref[...])
    decay = jnp.exp(dt * (-jnp.exp(alog_ref[...])))
    hexp = hexp_ref[...]
    xdt_ref[...] = xs * _dot3_right(dt, hexp)
    drow_ref[...] = _dot3_right(decay, hexp)
    del n


def _dec_prep(proj, dt, conv_state_t, lb, conv_w, conv_b, dt_bias, a_log):
    n = proj.shape[0]
    cq, sq, ck, sk = _rope_tables(np.full((n,), PAST_LEN))
    hexp = np.zeros((LANES, SSM_INNER), np.float32)
    for h in range(SSM_HEADS):
        hexp[h, h * SSM_HEADDIM:(h + 1) * SSM_HEADDIM] = 1.0
    outs = [(n, 1024), (n, 2048), (n, 1024), (n, 1024), (n, 1024), (n, 1024)]
    return pl.pallas_call(
        _dec_prep_kernel,
        out_shape=[jax.ShapeDtypeStruct(s, F32) for s in outs]
        + [jax.ShapeDtypeStruct((SSM_CONV - 1, n, SSM_CONV_CH), F32)],
        compiler_params=pltpu.CompilerParams(vmem_limit_bytes=VMEM_LIMIT),
        name="dec_prep",
    )(proj, dt, conv_state_t, cq, sq, ck, sk, lb, conv_w, conv_b.reshape(1, -1), _pad_lanes(dt_bias),
      _pad_lanes(a_log), jnp.asarray(hexp, BF16))


def _cols(a, tb):
    n, ch = a.shape
    return a.reshape(n // tb, tb, ch).transpose(0, 2, 1)


def _state_update_call(body, name, layer, state, carried, n_tokens, heads_per_step, in_specs, operands,
                       scratch):
    depth, _, heads, sk, sv = state.shape
    tb = DEC_TOKENS
    if carried is None:
        src = state
        sspec = pl.BlockSpec((depth, tb, heads_per_step, sk, sv), lambda t, j: (0, t, j, 0, 0))
        own, keep, aliases = layer, tuple(d for d in range(depth) if d != layer), {}
    else:
        src = carried
        sspec = pl.BlockSpec((1, tb, heads_per_step, sk, sv), lambda t, j: (layer, t, j, 0, 0))
        own, keep, aliases = 0, (), {0: 1}
    return pl.pallas_call(
        functools.partial(body, own=own, keep=keep),
        grid=(n_tokens // tb, heads // heads_per_step),
        in_specs=[sspec] + in_specs,
        out_specs=[pl.BlockSpec((tb, 1024), lambda t, j: (t, 0)), sspec],
        out_shape=[jax.ShapeDtypeStruct((n_tokens, 1024), F32), jax.ShapeDtypeStruct(state.shape, F32)],
        input_output_aliases=aliases,
        scratch_shapes=[scratch],
        compiler_params=_cparams(("parallel", "arbitrary")),
        name=name,
    )(src, *operands)


def _ret_dec_kernel(s_ref, ct_ref, v_ref, rg_ref, gain_ref, y_ref, so_ref, o_scr, *, gamma, own, keep):
    for i in range(DEC_TOKENS):
        for h in range(RET_HEADS):
            qcol = ct_ref[0, h * RET_DK:(h + 1) * RET_DK, i:i + 1]
            kcol = ct_ref[0, (RET_HEADS + h) * RET_DK:(RET_HEADS + h + 1) * RET_DK, i:i + 1]
            vs = slice(h * RET_DV, (h + 1) * RET_DV)
            s_new = gamma[h] * s_ref[own, i, h] + kcol * v_ref[i:i + 1, vs]
            so_ref[own, i, h] = s_new
            o_scr[i:i + 1, vs] = jnp.sum(qcol * s_new, axis=0, keepdims=True)
    for j in keep:
        so_ref[j] = s_ref[j]
    for h in range(RET_HEADS):
        vs = slice(h * RET_DV, (h + 1) * RET_DV)
        y_ref[:, vs] = _silu(rg_ref[:, vs]) * _head_layernorm(o_scr[:, vs], gain_ref[:, vs])


def _ret_dec(layer, state, carried, c_ret, proj, gain):
    tb = DEC_TOKENS
    gamma = tuple(float(v) for v in np.exp(_ret_log_gamma()))
    return _state_update_call(
        functools.partial(_ret_dec_kernel, gamma=gamma), "ret_dec", layer, state, carried, proj.shape[0],
        RET_HEADS,
        [pl.BlockSpec((1, 2 * RET_HEADS * RET_DK, tb), lambda t, j: (t, 0, 0)),
         pl.BlockSpec((tb, 1024), lambda t, j: (t, 1)), pl.BlockSpec((tb, 1024), lambda t, j: (t, 2)),
         pl.BlockSpec((1, 1024), lambda t, j: (0, 0))],
        (_cols(c_ret, tb), proj, proj, gain), pltpu.VMEM((tb, 1024), F32))


def _hgrn_dec_kernel(s_ref, ct_ref, v_ref, hg_ref, gain_ref, y_ref, so_ref, o_scr, *, own, keep):
    for i in range(DEC_TOKENS):
        for h in range(HG_HEADS):
            sl = slice(h * LANES, (h + 1) * LANES)
            qcol = ct_ref[0, h * HG_EXPAND:(h + 1) * HG_EXPAND, i:i + 1]
            fcol = ct_ref[0, (HG_HEADS + h) * HG_EXPAND:(HG_HEADS + h + 1) * HG_EXPAND, i:i + 1]
            s_new = fcol * s_ref[own, i, h] + (1.0 - fcol) * v_ref[i:i + 1, sl]
            so_ref[own, i, h] = s_new
            o_scr[i:i + 1, sl] = jnp.sum(qcol * s_new, axis=0, keepdims=True)
    for j in keep:
        so_ref[j] = s_ref[j]
    for h in range(HG_HEADS):
        sl = slice(h * LANES, (h + 1) * LANES)
        o = o_scr[:, sl]
        o = o * lax.rsqrt(jnp.mean(o * o, axis=-1, keepdims=True) + EPS) * gain_ref[:, sl]
        y_ref[:, sl] = o * _sigmoid(hg_ref[:, sl])


def _hgrn_dec(layer, state, carried, c_hg, proj, gain):
    tb = DEC_TOKENS
    return _state_update_call(
        _hgrn_dec_kernel, "hgrn_dec", layer, state, carried, proj.shape[0], HG_HEADS,
        [pl.BlockSpec((1, 2 * HG_HEADS * HG_EXPAND, tb), lambda t, j: (t, 0, 0)),
         pl.BlockSpec((tb, 1024), lambda t, j: (t, 5)), pl.BlockSpec((tb, 1024), lambda t, j: (t, 6)),
         pl.BlockSpec((1, 1024), lambda t, j: (0, 0))],
        (_cols(c_hg, tb), proj, proj, gain), pltpu.VMEM((tb, 1024), F32))


def _ssd_dec_kernel(s_ref, c_ref, b_ref, xdt_ref, drow_ref, xs_ref, z_ref, dsk_ref, norm_ref,
                    y_ref, so_ref, o_scr, *, own, keep):
    for g in range(SSM_GROUPS):
        c_rows = c_ref[:, g * SSM_STATE:(g + 1) * SSM_STATE].astype(BF16)
        for i in range(DEC_TOKENS):
            b_row = b_ref[i:i + 1, g * SSM_STATE:(g + 1) * SSM_STATE]
            new = []
            for h in range(g * HEADS_PER_GROUP, (g + 1) * HEADS_PER_GROUP):
                hs = slice(h * SSM_HEADDIM, (h + 1) * SSM_HEADDIM)
                decay = drow_ref[i:i + 1, h * SSM_HEADDIM:h * SSM_HEADDIM + 1]
                s_new = decay * s_ref[own, i, h] + xdt_ref[0, hs, i:i + 1] * b_row
                so_ref[own, i, h] = s_new
                new.append(s_new)
            s_grp = jnp.concatenate(new, axis=0).astype(BF16)
            o_scr[i:i + 1, g * GROUP_W:(g + 1) * GROUP_W] = _dot_nt(c_rows, s_grp)[i:i + 1, :]
    for d in keep:
        so_ref[d] = s_ref[d]
    o = o_scr[...] + dsk_ref[...] * xs_ref[...]
    y_ref[...] = _rms(o * _silu(z_ref[...]), norm_ref[...])


def _ssd_dec(layer, state, carried, c_ssd, xs, xdt, drow, proj, dsk_exp, norm):
    tb = DEC_TOKENS
    rows = lambda col: pl.BlockSpec((tb, 1024), lambda t, j: (t, col))
    half = lambda col: pl.BlockSpec((tb, SSM_GROUPS * SSM_STATE), lambda t, j: (t, col))
    vec = pl.BlockSpec((1, 1024), lambda t, j: (0, 0))
    return _state_update_call(
        _ssd_dec_kernel, "ssd_dec", layer, state, carried, proj.shape[0], SSM_HEADS,
        [half(0), half(1), pl.BlockSpec((1, SSM_INNER, tb), lambda t, j: (t, 0, 0)),
         rows(0), rows(0), rows(7), vec, vec],
        (c_ssd, c_ssd, _cols(xdt, tb), drow, xs, proj, dsk_exp, norm), pltpu.VMEM((tb, 1024), F32))


def kernel(x_prompt, x_sample, state_ret, state_hgrn, state_ssm, state_conv, w_in, ret_norm, hg_norm,
           hg_lb_logits, conv_w, conv_b, dt_bias, a_log, d_skip, ssm_norm, w_branch, w_out, norm_mix_pre,
           norm_mix_post, norm_ffn_pre, norm_ffn_post, w_up, w_down):
    bp, lp, _ = x_prompt.shape
    bs, ls, _ = x_sample.shape
    assert ls == 1 and lp % CHUNK_ROWS == 0 and bs % DEC_TOKENS == 0
    depth = w_in.shape[0]

    lb_w = jax.nn.softmax(hg_lb_logits.astype(F32), axis=0)
    lb_all = jnp.cumsum(lb_w, axis=0) - lb_w[0]

    hp = x_prompt.reshape(bp * lp, D_MODEL)
    hs = x_sample.reshape(bs, D_MODEL)
    outs = [[] for _ in range(5)]
    new_ret = new_hg = new_ssm = None
    ssm_t = jnp.swapaxes(state_ssm, -1, -2)
    w_in_t = jnp.swapaxes(w_in, 1, 2)
    vec = lambda a: a.reshape(1, -1).astype(F32)
    hp_normed = hs_normed = None
    for l in range(depth):
        next_gain = vec(norm_mix_pre[l + 1]) if l + 1 < depth else None
        w_gates = w_in_t[l, N_MAIN + SSM_HEADS:].astype(BF16)
        w_hg = w_in_t[l, HG_TILE0 * PROJ_TILE:(HG_TILE0 + HG_TILES) * PROJ_TILE].astype(BF16)
        w_dt = jnp.pad(w_in_t[l, N_MAIN:N_MAIN + SSM_HEADS], ((0, DT_PAD - SSM_HEADS), (0, 0))).astype(BF16)
        wb, wo = w_branch[l].astype(BF16), w_out[l].astype(BF16)
        dsk_exp = jnp.repeat(d_skip[l].astype(F32), SSM_HEADDIM).reshape(1, SSM_INNER)
        lb = vec(lb_all[l])

        proj, dt, xn = _in_proj(hp if hp_normed is None else hp_normed,
                                vec(norm_mix_pre[l]) if hp_normed is None else None,
                                w_in_t, l, w_gates, w_dt, tm=min(2048, bp * lp), out_dtype=BF16, skip_hgrn=True)
        yr, s_ret = _ret_prompt(proj, vec(ret_norm[l]), bp, lp)
        yh, s_hg = _hgrn_prompt(xn, w_hg, lb, vec(hg_norm[l]), bp, lp)
        ym, s_ssm, s_conv = _ssd_prompt(proj, COLS_NO_HGRN, dt, conv_w[l], conv_b[l], dt_bias[l], a_log[l],
                                        dsk_exp, vec(ssm_norm[l]), bp, lp)
        hp, xn = _merge(hp, yr, yh, ym, proj, COLS_NO_HGRN, wb, wo, vec(norm_mix_post[l]),
                        vec(norm_ffn_pre[l]), tm=512)
        hp, hp_normed = _ffn(hp, xn, vec(norm_ffn_post[l]), next_gain, w_up, w_down, l, tm=1024)
        for lst, val in zip(outs[:4], (s_ret, s_hg, s_ssm, s_conv)):
            lst.append(val)

        proj, dt, _ = _in_proj(hs if hs_normed is None else hs_normed,
                               vec(norm_mix_pre[l]) if hs_normed is None else None,
                               w_in_t, l, w_gates, w_dt, tm=bs, out_dtype=F32, skip_hgrn=False)
        c_ret, c_hg, c_ssd, xs, xdt, drow, conv_new = _dec_prep(
            proj, dt, state_conv[l].transpose(1, 0, 2), lb, conv_w[l], conv_b[l], dt_bias[l], a_log[l])
        yr, new_ret = _ret_dec(l, state_ret, new_ret, c_ret, proj, vec(ret_norm[l]))
        yh, new_hg = _hgrn_dec(l, state_hgrn, new_hg, c_hg, proj, vec(hg_norm[l]))
        ym, new_ssm = _ssd_dec(l, ssm_t, new_ssm, c_ssd, xs, xdt, drow, proj, dsk_exp, vec(ssm_norm[l]))
        hs, xn = _merge(hs, yr, yh, ym, proj, COLS_FULL, wb, wo, vec(norm_mix_post[l]), vec(norm_ffn_pre[l]),
                        tm=bs)
        hs, hs_normed = _ffn(hs, xn, vec(norm_ffn_post[l]), next_gain, w_up, w_down, l, tm=bs)
        outs[4].append(conv_new.transpose(1, 0, 2))

    p_ret, p_hg, p_ssm, p_conv, s_conv = [jnp.stack(o) for o in outs]
    return (hp.reshape(bp, lp, D_MODEL), hs.reshape(bs, ls, D_MODEL), p_ret, p_hg, p_ssm, p_conv,
            new_ret, new_hg, jnp.swapaxes(new_ssm, -1, -2), s_conv)
```

```python
import functools

import numpy as np
import jax
import jax.numpy as jnp
from jax import lax
from jax.experimental import pallas as pl
from jax.experimental.pallas import tpu as pltpu

F32 = jnp.float32
BF16 = jnp.bfloat16

D_MODEL = 1024
PAST_LEN = 16384
RET_HEADS, RET_DK, RET_DV = 4, 128, 256
ROPE_BASE = 10000.0
HG_HEADS, HG_EXPAND, HG_DV = 8, 128, 128
SSM_HEADS, SSM_HEADDIM, SSM_GROUPS, SSM_STATE, SSM_CONV = 16, 64, 4, 128, 4
SSM_INNER = SSM_HEADS * SSM_HEADDIM
SSM_CONV_CH = SSM_INNER + 2 * SSM_GROUPS * SSM_STATE
N_BRANCH = 3
D_FF = 4 * D_MODEL
EPS = 1e-6
HG_CLIP_MIN = 1e-6

N_MAIN = 10240
N_PROJ = N_MAIN + 3 * D_MODEL
DT_PAD = 128

LANES = 128
SUBLANES = 8
VMEM_LIMIT = 56 * 1024 * 1024

CHUNK_ROWS = 256
DEC_TOKENS = 8


def _cparams(sem):
    return pltpu.CompilerParams(dimension_semantics=sem, vmem_limit_bytes=VMEM_LIMIT)


def _dot(a, b):
    return jnp.dot(a, b, preferred_element_type=F32)


def _dot_nt(a, b):
    return lax.dot_general(a, b, (((1,), (1,)), ((), ())), preferred_element_type=F32)


def _dot_tn(a, b):
    return lax.dot_general(a, b, (((0,), (0,)), ((), ())), preferred_element_type=F32)


def _split3(a):
    hi = a.astype(BF16)
    r1 = a - hi.astype(F32)
    mid = r1.astype(BF16)
    lo = (r1 - mid.astype(F32)).astype(BF16)
    return hi, mid, lo


def _dot3_left(m01, a):
    hi, mid, lo = _split3(a)
    return _dot(m01, hi) + _dot(m01, mid) + _dot(m01, lo)


def _dot3_right(a, m01):
    hi, mid, lo = _split3(a)
    return _dot(hi, m01) + _dot(mid, m01) + _dot(lo, m01)


def _sigmoid(x):
    return 1.0 / (1.0 + jnp.exp(-x))


def _silu(x):
    return x * _sigmoid(x)


def _softplus(x):
    return jnp.maximum(x, 0.0) + jnp.log1p(jnp.exp(-jnp.abs(x)))


def _rms(x, gain):
    return x * lax.rsqrt(jnp.mean(x * x, axis=-1, keepdims=True) + EPS) * gain


PROJ_TILE = 1024
MAIN_TILES = N_MAIN // PROJ_TILE
GATE_TILE0 = MAIN_TILES


def _project_tile(xn_ref, w_ref, wg_ref, o_ref):
    j = pl.program_id(1)

    @pl.when(j < MAIN_TILES)
    def _():
        o_ref[...] = _dot_nt(xn_ref[...], w_ref[0].astype(BF16)).astype(o_ref.dtype)

    @pl.when(j >= MAIN_TILES)
    def _():
        o_ref[...] = _dot_nt(xn_ref[...], wg_ref[...]).astype(o_ref.dtype)


def _in_proj_kernel(x_ref, g_ref, w_ref, wg_ref, o_ref, xn_ref):
    @pl.when(pl.program_id(1) == 0)
    def _():
        xn_ref[...] = _rms(x_ref[...], g_ref[...]).astype(BF16)

    _project_tile(xn_ref, w_ref, wg_ref, o_ref)


def _in_proj(x, gain, w_in_t, layer, w_gates, tm, out_dtype):
    t = x.shape[0]
    tn = PROJ_TILE
    rows = pl.BlockSpec((tm, D_MODEL), lambda i, j: (i, 0))
    w_specs = [pl.BlockSpec((1, tn, D_MODEL), lambda i, j: (layer, jnp.minimum(j, MAIN_TILES - 1), 0)),
               pl.BlockSpec((tn, D_MODEL), lambda i, j: (jnp.maximum(j - MAIN_TILES, 0), 0))]
    common = dict(grid=(t // tm, N_PROJ // tn), compiler_params=_cparams(("parallel", "arbitrary")),
                  name="in_proj")
    out_spec = pl.BlockSpec((tm, tn), lambda i, j: (i, j))
    out_shape = jax.ShapeDtypeStruct((t, N_PROJ), out_dtype)
    if gain is None:
        proj = pl.pallas_call(_project_tile, in_specs=[rows] + w_specs, out_specs=out_spec,
                              out_shape=out_shape, **common)(x, w_in_t, w_gates)
        return proj, x
    return pl.pallas_call(
        _in_proj_kernel, in_specs=[rows, pl.BlockSpec((1, D_MODEL), lambda i, j: (0, 0))] + w_specs,
        out_specs=[out_spec, rows], out_shape=[out_shape, jax.ShapeDtypeStruct((t, D_MODEL), BF16)], **common,
    )(x, gain, w_in_t, w_gates)


def _ret_log_gamma():
    return np.log1p(-np.exp2(-5.0 - np.arange(RET_HEADS, dtype=np.float64)))


def _ret_consts(r):
    lg = _ret_log_gamma()
    i = np.arange(r)
    diff = (i[:, None] - i[None, :]).astype(np.float64)
    dec = np.where(diff >= 0, np.exp(lg[:, None, None] * np.maximum(diff, 0.0)), 0.0)
    qd = np.exp(lg[:, None] * (i + 1.0))
    kd = np.exp(lg[:, None] * (r - 1.0 - i))
    gl = tuple(float(v) for v in np.exp(lg * r))
    bc = lambda a: jnp.asarray(np.broadcast_to(a[:, :, None], (RET_HEADS, r, LANES)), F32)
    return jnp.asarray(dec, F32), bc(qd), bc(kd), gl


def _rope_tables(pos):
    half = RET_DK // 2
    inv = ROPE_BASE ** (-(np.arange(half, dtype=np.float64) / half))
    ang = pos.astype(np.float64)[:, None] * inv[None, :]
    cos, sin = np.cos(ang), np.sin(ang)
    cos2 = np.concatenate([cos, cos], axis=1)
    sin2 = np.concatenate([-sin, sin], axis=1)
    scale = RET_DK ** -0.5
    return (jnp.asarray(cos2, F32), jnp.asarray(sin2, F32),
            jnp.asarray(cos2 * scale, F32), jnp.asarray(sin2 * scale, F32))


def _rope(t, cos2, sin2):
    return t * cos2 + pltpu.roll(t, RET_DK // 2, 1) * sin2


def _head_layernorm(o, gain):
    oc = o - jnp.mean(o, axis=-1, keepdims=True)
    return oc * lax.rsqrt(jnp.mean(oc * oc, axis=-1, keepdims=True) + EPS) * gain


def _ret_kernel(rq_ref, rk_ref, rv_ref, rg_ref, cq_ref, sq_ref, ck_ref, sk_ref, dec_ref, qd_ref, kd_ref,
                gain_ref, y_ref, sout_ref, s_scr, *, gl):
    c = pl.program_id(1)

    @pl.when(c == 0)
    def _():
        s_scr[...] = jnp.zeros_like(s_scr)

    cq, sq, ck, sk = cq_ref[...], sq_ref[...], ck_ref[...], sk_ref[...]
    for h in range(RET_HEADS):
        ks = slice(h * RET_DK, (h + 1) * RET_DK)
        vs = slice(h * RET_DV, (h + 1) * RET_DV)
        q = _rope(rq_ref[:, ks].astype(F32), cq, sq)
        k = _rope(rk_ref[:, ks].astype(F32), ck, sk)
        v = rv_ref[:, vs].astype(BF16)
        att = (_dot_nt(q.astype(BF16), k.astype(BF16)) * dec_ref[h]).astype(BF16)
        s_old = s_scr[h]
        o = _dot(att, v) + _dot((q * qd_ref[h]).astype(BF16), s_old.astype(BF16))
        s_scr[h] = gl[h] * s_old + _dot_tn((k * kd_ref[h]).astype(BF16), v)
        y = _silu(rg_ref[:, vs].astype(F32)) * _head_layernorm(o, gain_ref[:, vs])
        y_ref[:, vs] = y.astype(y_ref.dtype)

    @pl.when(c == pl.num_programs(1) - 1)
    def _():
        sout_ref[0] = s_scr[...]


def _ret_prompt(proj, gain, batch, seq):
    r = CHUNK_ROWS
    nc = seq // r
    dec, qd, kd, gl = _ret_consts(r)
    cq, sq, ck, sk = _rope_tables(np.arange(seq))
    row = lambda b, c: b * nc + c
    tab = pl.BlockSpec((r, RET_DK), lambda b, c: (c, 0))
    const3 = lambda shape: pl.BlockSpec(shape, lambda b, c: (0, 0, 0))
    return pl.pallas_call(
        functools.partial(_ret_kernel, gl=gl),
        grid=(batch, nc),
        in_specs=[
            pl.BlockSpec((r, 512), lambda b, c: (row(b, c), 0)),
            pl.BlockSpec((r, 512), lambda b, c: (row(b, c), 1)),
            pl.BlockSpec((r, 1024), lambda b, c: (row(b, c), 1)),
            pl.BlockSpec((r, 1024), lambda b, c: (row(b, c), 2)),
            tab, tab, tab, tab,
            const3((RET_HEADS, r, r)), const3((RET_HEADS, r, LANES)), const3((RET_HEADS, r, LANES)),
            pl.BlockSpec((1, 1024), lambda b, c: (0, 0)),
        ],
        out_specs=[
            pl.BlockSpec((r, 1024), lambda b, c: (row(b, c), 0)),
            pl.BlockSpec((1, RET_HEADS, RET_DK, RET_DV), lambda b, c: (b, 0, 0, 0)),
        ],
        out_shape=[jax.ShapeDtypeStruct((batch * seq, 1024), BF16),
                   jax.ShapeDtypeStruct((batch, RET_HEADS, RET_DK, RET_DV), F32)],
        scratch_shapes=[pltpu.VMEM((RET_HEADS, RET_DK, RET_DV), F32)],
        compiler_params=_cparams(("parallel", "arbitrary")),
        name="ret_prompt",
    )(proj, proj, proj, proj, cq, sq, ck, sk, dec, qd, kd, gain)


HG_CHUNK = 128
HG_LEVELS = (64, 32, 16, 8)
HG_TILE_REF = SUBLANES // 2 - 1
LOG2E = 1.4426950408889634


def _hg_masks():
    i = np.arange(HG_CHUNK)
    out = [(i[:, None] // (2 * b)) == (i[None, :] // (2 * b)) for b in HG_LEVELS[1:]]
    out.append(((i[:, None] // SUBLANES) == (i[None, :] // SUBLANES)) & (i[:, None] >= i[None, :]))
    return jnp.asarray(np.stack(out).astype(np.float32))


def _tri_ones(r):
    i = np.arange(r)
    return jnp.asarray((i[:, None] >= i[None, :]).astype(np.float32), BF16)


def _halves(a, nb):
    a5 = a.reshape(HG_CHUNK // (2 * nb * SUBLANES), 2, nb, SUBLANES, a.shape[-1])
    return a5[:, 0], a5[:, 1]


def _join_halves(lower, upper):
    return jnp.stack([lower, upper], axis=1).reshape(HG_CHUNK, lower.shape[-1])


def _tile_row_bcast(a, j):
    a3 = a.reshape(HG_CHUNK // SUBLANES, SUBLANES, a.shape[-1])
    return jnp.broadcast_to(a3[:, j:j + 1, :], a3.shape).reshape(a.shape)


def _per_head_nt(a, b):
    return [_dot_nt(a[:, h * LANES:(h + 1) * LANES], b[:, h * LANES:(h + 1) * LANES]) for h in range(HG_HEADS)]


def _hg_chunk_att(q, kk, g2, mask_ref):
    tile_last = _tile_row_bcast(g2, SUBLANES - 1)
    att = None
    for lvl, b in enumerate(HG_LEVELS):
        nb = b // SUBLANES
        _, q_up = _halves(q, nb)
        k_lo, _ = _halves(kk, nb)
        g_lo, g_up = _halves(g2, nb)
        ref = _halves(tile_last, nb)[0][:, nb - 1:nb]
        qe = q_up * jnp.exp2(g_up - ref)
        ke = k_lo * jnp.exp2(ref - g_lo)
        zeros = jnp.zeros_like(qe)
        prods = _per_head_nt(_join_halves(zeros, qe).astype(BF16), _join_halves(ke, zeros).astype(BF16))
        att = prods if lvl == 0 else [a + p * mask_ref[lvl - 1] for a, p in zip(att, prods)]
    d = g2 - _tile_row_bcast(g2, HG_TILE_REF)
    prods = _per_head_nt((q * jnp.exp2(d)).astype(BF16), (kk * jnp.exp2(-d)).astype(BF16))
    same_tile = mask_ref[len(HG_LEVELS) - 1] > 0.5
    return [a + jnp.where(same_tile, p, 0.0) for a, p in zip(att, prods)]


def _hgrn_kernel(hq_ref, hf_ref, hi_ref, hg_ref, lb_ref, gain_ref, tri_ref, mask_ref, y_ref, sout_ref,
                 st_scr, *, r):
    c = pl.program_id(1)

    @pl.when(c == 0)
    def _():
        st_scr[...] = jnp.zeros_like(st_scr)

    lb = lb_ref[...]
    for ck in range(r // HG_CHUNK):
        rows = slice(ck * HG_CHUNK, (ck + 1) * HG_CHUNK)
        q = _silu(hq_ref[rows, :].astype(F32))
        f = jnp.clip(lb + (1.0 - lb) * _sigmoid(hf_ref[rows, :].astype(F32)), HG_CLIP_MIN, 1.0)
        kk = 1.0 - f
        g2 = _dot3_left(tri_ref[...], jnp.log(f)) * LOG2E
        g2_last = g2[HG_CHUNK - 1:HG_CHUNK, :]
        att = _hg_chunk_att(q, kk, g2, mask_ref)
        q_in = (q * jnp.exp2(g2)).astype(BF16)
        k_out = (kk * jnp.exp2(g2_last - g2)).astype(BF16)
        e_last = jnp.exp2(g2_last)
        gate = _sigmoid(hg_ref[rows, :].astype(F32))
        for h in range(HG_HEADS):
            sl = slice(h * LANES, (h + 1) * LANES)
            v = hi_ref[rows, sl]
            st_old = st_scr[h]
            o = _dot(att[h].astype(BF16), v) + _dot_nt(q_in[:, sl], st_old.astype(BF16))
            st_scr[h] = st_old * e_last[:, sl] + _dot_tn(v, k_out[:, sl])
            o = o * lax.rsqrt(jnp.mean(o * o, axis=-1, keepdims=True) + EPS) * gain_ref[:, sl]
            y_ref[rows, sl] = (o * gate[:, sl]).astype(y_ref.dtype)

    @pl.when(c == pl.num_programs(1) - 1)
    def _():
        for h in range(HG_HEADS):
            sout_ref[0, h] = st_scr[h].T


def _hgrn_prompt(proj, lb, gain, batch, seq):
    r = CHUNK_ROWS
    nc = seq // r
    masks = _hg_masks()
    row = lambda b, c: b * nc + c
    blk = lambda j: pl.BlockSpec((r, 1024), lambda b, c: (row(b, c), j))
    vec = pl.BlockSpec((1, 1024), lambda b, c: (0, 0))
    return pl.pallas_call(
        functools.partial(_hgrn_kernel, r=r),
        grid=(batch, nc),
        in_specs=[blk(3), blk(4), blk(5), blk(6), vec, vec,
                  pl.BlockSpec((HG_CHUNK, HG_CHUNK), lambda b, c: (0, 0)),
                  pl.BlockSpec(masks.shape, lambda b, c: (0, 0, 0))],
        out_specs=[
            pl.BlockSpec((r, 1024), lambda b, c: (row(b, c), 0)),
            pl.BlockSpec((1, HG_HEADS, HG_EXPAND, HG_DV), lambda b, c: (b, 0, 0, 0)),
        ],
        out_shape=[jax.ShapeDtypeStruct((batch * seq, 1024), BF16),
                   jax.ShapeDtypeStruct((batch, HG_HEADS, HG_EXPAND, HG_DV), F32)],
        scratch_shapes=[pltpu.VMEM((HG_HEADS, HG_DV, HG_EXPAND), F32)],
        compiler_params=_cparams(("parallel", "arbitrary")),
        name="hgrn_prompt",
    )(proj, proj, proj, proj, lb, gain, _tri_ones(HG_CHUNK), masks)


HEADS_PER_GROUP = SSM_HEADS // SSM_GROUPS
GROUP_W = HEADS_PER_GROUP * SSM_HEADDIM
CONV_PAD = SUBLANES
SSD_CHUNK = 128


def _head_expand(vals, h0, lane_head):
    out = jnp.zeros((vals.shape[0], GROUP_W), F32)
    for j in range(HEADS_PER_GROUP):
        out = jnp.where(lane_head == j, vals[:, h0 + j:h0 + j + 1], out)
    return out


def _ssd_kernel(z_ref, xbc_ref, xn_ref, wdt_ref, cw_ref, cb_ref, dtb_ref, alog_ref, dsk_ref, norm_ref, tri_ref,
                shift_ref, y_ref, sout_ref, cout_ref, hist_scr, s_scr, o_scr, *, r):
    c = pl.program_id(1)
    last = pl.num_programs(1) - 1

    @pl.when(c == 0)
    def _():
        hist_scr[0:CONV_PAD, :] = jnp.zeros((CONV_PAD, SSM_CONV_CH), F32)
        s_scr[...] = jnp.zeros_like(s_scr)

    u_bf = xbc_ref[...]
    u = u_bf.astype(F32)
    hist_scr[CONV_PAD:2 * CONV_PAD, :] = u[0:CONV_PAD]
    conv = cb_ref[...] + cw_ref[SSM_CONV - 1:SSM_CONV, :] * u
    head = cb_ref[...]
    for w in range(SSM_CONV):
        off = CONV_PAD - (SSM_CONV - 1) + w
        head = head + cw_ref[w:w + 1, :] * hist_scr[off:off + CONV_PAD, :]
    for back in range(1, SSM_CONV):
        conv = conv + cw_ref[SSM_CONV - 1 - back:SSM_CONV - back, :] * _dot(shift_ref[back - 1], u_bf)
    conv = jnp.concatenate([head, conv[CONV_PAD:]], axis=0)

    @pl.when(c == last)
    def _():
        cout_ref[0] = u[r - (SSM_CONV - 1):r, :]

    hist_scr[0:CONV_PAD, :] = u[r - CONV_PAD:r]

    xc = _silu(conv)
    xs = xc[:, :SSM_INNER]
    bm = xc[:, SSM_INNER:SSM_INNER + SSM_GROUPS * SSM_STATE]
    cm = xc[:, SSM_INNER + SSM_GROUPS * SSM_STATE:]

    dt_all = _softplus(_dot_nt(xn_ref[...], wdt_ref[...]) + dtb_ref[...])
    rate = -jnp.exp(alog_ref[...]) * LOG2E
    rc = SSD_CHUNK
    ri = lax.broadcasted_iota(jnp.int32, (rc, rc), 0)
    ci = lax.broadcasted_iota(jnp.int32, (rc, rc), 1)
    causal = ri >= ci
    lane_head = lax.broadcasted_iota(jnp.int32, (rc, GROUP_W), 1) // SSM_HEADDIM
    lane_head1 = lax.broadcasted_iota(jnp.int32, (1, GROUP_W), 1) // SSM_HEADDIM
    groups = range(SSM_GROUPS)

    for ck in range(r // rc):
        rows = slice(ck * rc, (ck + 1) * rc)
        dt = dt_all[rows]
        g2 = _dot3_left(tri_ref[...], dt * rate)
        g2_last = g2[rc - 1:rc, :]
        g2_t = g2.T
        dt_t = dt.T
        e_cum = jnp.exp2(g2)
        w_in = dt * jnp.exp2(g2_last - g2)
        e_last = jnp.exp2(g2_last)

        cg = [cm[rows, g * SSM_STATE:(g + 1) * SSM_STATE].astype(BF16) for g in groups]
        bg = [bm[rows, g * SSM_STATE:(g + 1) * SSM_STATE].astype(BF16) for g in groups]
        xs_g = [xs[rows, g * GROUP_W:(g + 1) * GROUP_W] for g in groups]
        xs_b = [x.astype(BF16) for x in xs_g]
        s_old = [s_scr[g] for g in groups]
        cb = [jnp.where(causal, _dot_nt(cg[g], bg[g]), 0.0) for g in groups]
        carried = [_dot(cg[g], s_old[g].astype(BF16)) for g in groups]
        for g in groups:
            h0 = g * HEADS_PER_GROUP
            o_g = _head_expand(e_cum, h0, lane_head) * carried[g]
            for j in range(HEADS_PER_GROUP):
                h = h0 + j
                dec = jnp.exp2(jnp.minimum(g2[:, h:h + 1] - g2_t[h:h + 1, :], 0.0))
                att = (cb[g] * dec * dt_t[h:h + 1, :]).astype(BF16)
                o_g = o_g + jnp.where(lane_head == j, _dot(att, xs_b[g]), 0.0)
            s_scr[g] = (s_old[g] * _head_expand(e_last, h0, lane_head1)
                        + _dot_tn(bg[g], (xs_g[g] * _head_expand(w_in, h0, lane_head)).astype(BF16)))
            o_scr[rows, g * GROUP_W:(g + 1) * GROUP_W] = o_g + dsk_ref[:, g * GROUP_W:(g + 1) * GROUP_W] * xs_g[g]

    y_ref[...] = _rms(o_scr[...] * _silu(z_ref[...].astype(F32)), norm_ref[...]).astype(y_ref.dtype)

    @pl.when(c == last)
    def _():
        for h in range(SSM_HEADS):
            j = h % HEADS_PER_GROUP
            sout_ref[0, h] = s_scr[h // HEADS_PER_GROUP][:, j * SSM_HEADDIM:(j + 1) * SSM_HEADDIM]


def _shift_ones(r):
    i = np.arange(r)
    mats = [(i[:, None] - i[None, :]) == k for k in range(1, SSM_CONV)]
    return jnp.asarray(np.stack(mats).astype(np.float32), BF16)


def _pad_lanes(v):
    return jnp.pad(v.astype(F32), (0, LANES - v.shape[0])).reshape(1, LANES)


def _ssd_prompt(proj, xn, w_dt, conv_w, conv_b, dt_bias, a_log, dsk_exp, norm, batch, seq):
    r = CHUNK_ROWS
    nc = seq // r
    row = lambda b, c: b * nc + c
    vec = lambda n: pl.BlockSpec((1, n), lambda b, c: (0, 0))
    return pl.pallas_call(
        functools.partial(_ssd_kernel, r=r),
        grid=(batch, nc),
        in_specs=[
            pl.BlockSpec((r, 1024), lambda b, c: (row(b, c), 7)),
            pl.BlockSpec((r, SSM_CONV_CH), lambda b, c: (row(b, c), 4)),
            pl.BlockSpec((r, D_MODEL), lambda b, c: (row(b, c), 0)),
            pl.BlockSpec((DT_PAD, D_MODEL), lambda b, c: (0, 0)),
            pl.BlockSpec((SSM_CONV, SSM_CONV_CH), lambda b, c: (0, 0)),
            vec(SSM_CONV_CH), vec(LANES), vec(LANES), vec(SSM_INNER), vec(SSM_INNER),
            pl.BlockSpec((SSD_CHUNK, SSD_CHUNK), lambda b, c: (0, 0)),
            pl.BlockSpec((SSM_CONV - 1, r, r), lambda b, c: (0, 0, 0)),
        ],
        out_specs=[
            pl.BlockSpec((r, 1024), lambda b, c: (row(b, c), 0)),
            pl.BlockSpec((1, SSM_HEADS, SSM_STATE, SSM_HEADDIM), lambda b, c: (b, 0, 0, 0)),
            pl.BlockSpec((1, SSM_CONV - 1, SSM_CONV_CH), lambda b, c: (b, 0, 0)),
        ],
        out_shape=[jax.ShapeDtypeStruct((batch * seq, 1024), BF16),
                   jax.ShapeDtypeStruct((batch, SSM_HEADS, SSM_STATE, SSM_HEADDIM), F32),
                   jax.ShapeDtypeStruct((batch, SSM_CONV - 1, SSM_CONV_CH), F32)],
        scratch_shapes=[pltpu.VMEM((2 * CONV_PAD, SSM_CONV_CH), F32),
                        pltpu.VMEM((SSM_GROUPS, SSM_STATE, GROUP_W), F32),
                        pltpu.VMEM((r, SSM_INNER), F32)],
        compiler_params=_cparams(("parallel", "arbitrary")),
        name="ssd_prompt",
    )(proj, proj, xn, w_dt, conv_w, conv_b.reshape(1, -1), _pad_lanes(dt_bias), _pad_lanes(a_log), dsk_exp, norm,
      _tri_ones(SSD_CHUNK), _shift_ones(r))


def _merge_kernel(x_ref, yr_ref, yh_ref, ym_ref, g0_ref, g1_ref, g2_ref, wb_ref, wo_ref, npost_ref, nnext_ref,
                  o_ref, xn_ref):
    merged = _sigmoid(g0_ref[...].astype(F32)) * _dot(yr_ref[...].astype(BF16), wb_ref[0])
    merged = merged + _sigmoid(g1_ref[...].astype(F32)) * _dot(yh_ref[...].astype(BF16), wb_ref[1])
    merged = merged + _sigmoid(g2_ref[...].astype(F32)) * _dot(ym_ref[...].astype(BF16), wb_ref[2])
    m = _dot(merged.astype(BF16), wo_ref[...])
    out = x_ref[...] + _rms(m, npost_ref[...])
    o_ref[...] = out
    xn_ref[...] = _rms(out, nnext_ref[...]).astype(BF16)


def _merge(x, yr, yh, ym, proj, wb, wo, npost, nnext, tm):
    t = x.shape[0]
    rows = lambda j: pl.BlockSpec((tm, D_MODEL), lambda i: (i, j))
    vec = pl.BlockSpec((1, D_MODEL), lambda i: (0, 0))
    return pl.pallas_call(
        _merge_kernel,
        grid=(t // tm,),
        in_specs=[rows(0), rows(0), rows(0), rows(0), rows(GATE_TILE0), rows(GATE_TILE0 + 1), rows(GATE_TILE0 + 2),
                  pl.BlockSpec((3, D_MODEL, D_MODEL), lambda i: (0, 0, 0)),
                  pl.BlockSpec((D_MODEL, D_MODEL), lambda i: (0, 0)), vec, vec],
        out_specs=[rows(0), rows(0)],
        out_shape=[jax.ShapeDtypeStruct((t, D_MODEL), F32), jax.ShapeDtypeStruct((t, D_MODEL), BF16)],
        compiler_params=_cparams(("parallel",)),
        name="merge",
    )(x, yr, yh, ym, proj, proj, proj, wb, wo, npost, nnext)


def _ffn_kernel(x_ref, xn_ref, wu_ref, wd_ref, npost_ref, *rest):
    (nnext_ref, o_ref, xnn_ref, acc_scr) = rest if len(rest) == 4 else (None, rest[0], None, rest[1])
    j = pl.program_id(1)

    @pl.when(j == 0)
    def _():
        acc_scr[...] = jnp.zeros_like(acc_scr)

    hid = jnp.square(jnp.maximum(_dot(xn_ref[...], wu_ref[0].astype(BF16)), 0.0))
    acc_scr[...] += _dot(hid.astype(BF16), wd_ref[0].astype(BF16))

    @pl.when(j == pl.num_programs(1) - 1)
    def _():
        out = x_ref[...] + _rms(acc_scr[...], npost_ref[...])
        o_ref[...] = out
        if xnn_ref is not None:
            xnn_ref[...] = _rms(out, nnext_ref[...]).astype(BF16)


def _ffn(x, xn, npost, nnext, w_up, w_down, layer, tm):
    t = x.shape[0]
    tf = 1024
    vec = pl.BlockSpec((1, D_MODEL), lambda i, j: (0, 0))
    rows = pl.BlockSpec((tm, D_MODEL), lambda i, j: (i, 0))
    chain = nnext is not None
    wu, wd = w_up, w_down
    outs = pl.pallas_call(
        _ffn_kernel,
        grid=(t // tm, D_FF // tf),
        in_specs=[rows, rows,
                  pl.BlockSpec((1, D_MODEL, tf), lambda i, j: (layer, 0, j)),
                  pl.BlockSpec((1, tf, D_MODEL), lambda i, j: (layer, j, 0)), vec] + ([vec] if chain else []),
        out_specs=[rows, rows] if chain else [rows],
        out_shape=[jax.ShapeDtypeStruct((t, D_MODEL), F32)]
        + ([jax.ShapeDtypeStruct((t, D_MODEL), BF16)] if chain else []),
        scratch_shapes=[pltpu.VMEM((tm, D_MODEL), F32)],
        compiler_params=_cparams(("parallel", "arbitrary")),
        name="ffn",
    )(x, xn, wu, wd, npost, *([nnext] if chain else []))
    return (outs[0], outs[1]) if chain else (outs[0], None)


def _dec_prep_kernel(proj_ref, xn_ref, wdt_ref, cst_ref, cq_ref, sq_ref, ck_ref, sk_ref, lb_ref, cw_ref, cb_ref,
                     dtb_ref, alog_ref, hexp_ref,
                     cret_ref, chg_ref, cssd_ref, xs_ref, xdt_ref, drow_ref, cnew_ref):
    n = proj_ref.shape[0]
    cq, sq, ck, sk = cq_ref[...], sq_ref[...], ck_ref[...], sk_ref[...]
    for h in range(RET_HEADS):
        ks = slice(h * RET_DK, (h + 1) * RET_DK)
        cret_ref[:, ks] = _rope(proj_ref[:, ks], cq, sq)
        k_off = RET_HEADS * RET_DK
        cret_ref[:, k_off + h * RET_DK:k_off + (h + 1) * RET_DK] = _rope(
            proj_ref[:, k_off + h * RET_DK:k_off + (h + 1) * RET_DK], ck, sk)

    lb = lb_ref[...]
    chg_ref[:, 0:1024] = _silu(proj_ref[:, 3072:4096])
    chg_ref[:, 1024:2048] = jnp.clip(lb + (1.0 - lb) * _sigmoid(proj_ref[:, 4096:5120]), HG_CLIP_MIN, 1.0)

    u = proj_ref[:, 8192:8192 + SSM_CONV_CH]
    conv = cb_ref[...] + cw_ref[SSM_CONV - 1:SSM_CONV, :] * u
    for w in range(SSM_CONV - 1):
        conv = conv + cw_ref[w:w + 1, :] * cst_ref[w]
    cnew_ref[0] = cst_ref[1]
    cnew_ref[1] = cst_ref[2]
    cnew_ref[2] = u
    xc = _silu(conv)
    xs = xc[:, :SSM_INNER]
    xs_ref[...] = xs
    cssd_ref[:, 0:512] = xc[:, SSM_INNER + 512:]
    cssd_ref[:, 512:1024] = xc[:, SSM_INNER:SSM_INNER + 512]
    dt = _softplus(_dot_nt(xn_ref[...], wdt_ref[...]) + dtb_ref[...])
    decay = jnp.exp(dt * (-jnp.exp(alog_ref[...])))
    hexp = hexp_ref[...]
    xdt_ref[...] = xs * _dot3_right(dt, hexp)
    drow_ref[...] = _dot3_right(decay, hexp)
    del n


def _dec_prep(proj, xn, w_dt, conv_state_t, lb, conv_w, conv_b, dt_bias, a_log):
    n = proj.shape[0]
    cq, sq, ck, sk = _rope_tables(np.full((n,), PAST_LEN))
    hexp = np.zeros((LANES, SSM_INNER), np.float32)
    for h in range(SSM_HEADS):
        hexp[h, h * SSM_HEADDIM:(h + 1) * SSM_HEADDIM] = 1.0
    outs = [(n, 1024), (n, 2048), (n, 1024), (n, 1024), (n, 1024), (n, 1024)]
    return pl.pallas_call(
        _dec_prep_kernel,
        out_shape=[jax.ShapeDtypeStruct(s, F32) for s in outs]
        + [jax.ShapeDtypeStruct((SSM_CONV - 1, n, SSM_CONV_CH), F32)],
        compiler_params=pltpu.CompilerParams(vmem_limit_bytes=VMEM_LIMIT),
        name="dec_prep",
    )(proj, xn, w_dt, conv_state_t, cq, sq, ck, sk, lb, conv_w, conv_b.reshape(1, -1), _pad_lanes(dt_bias),
      _pad_lanes(a_log), jnp.asarray(hexp, BF16))


def _cols(a, tb):
    n, ch = a.shape
    return a.reshape(n // tb, tb, ch).transpose(0, 2, 1)


def _state_update_call(body, name, layer, state, carried, n_tokens, heads_per_step, in_specs, operands,
                       scratch):
    depth, _, heads, sk, sv = state.shape
    tb = DEC_TOKENS
    in_spec = pl.BlockSpec((1, tb, heads_per_step, sk, sv), lambda t, j: (layer, t, j, 0, 0))
    if carried is None:
        carried, aliases = state, {}
        out_spec = pl.BlockSpec((depth, tb, heads_per_step, sk, sv), lambda t, j: (0, t, j, 0, 0))
        own, fill = layer, tuple(d for d in range(depth) if d != layer)
    else:
        aliases = {1: 1}
        out_spec, own, fill = in_spec, 0, ()
    return pl.pallas_call(
        functools.partial(body, own=own, fill=fill),
        grid=(n_tokens // tb, heads // heads_per_step),
        in_specs=[in_spec, pl.BlockSpec(memory_space=pl.ANY)] + in_specs,
        out_specs=[pl.BlockSpec((tb, 1024), lambda t, j: (t, 0)), out_spec],
        out_shape=[jax.ShapeDtypeStruct((n_tokens, 1024), F32), jax.ShapeDtypeStruct(state.shape, F32)],
        input_output_aliases=aliases,
        scratch_shapes=[scratch],
        compiler_params=_cparams(("parallel", "arbitrary")),
        name=name,
    )(state, carried, *operands)


def _zero_fill(so_ref, fill):
    for d in fill:
        so_ref[d] = jnp.zeros(so_ref.shape[1:], so_ref.dtype)


def _ret_dec_kernel(s_ref, carried_ref, ct_ref, v_ref, rg_ref, gain_ref, y_ref, so_ref, o_scr, *, gamma, own,
                    fill):
    del carried_ref
    for i in range(DEC_TOKENS):
        for h in range(RET_HEADS):
            qcol = ct_ref[0, h * RET_DK:(h + 1) * RET_DK, i:i + 1]
            kcol = ct_ref[0, (RET_HEADS + h) * RET_DK:(RET_HEADS + h + 1) * RET_DK, i:i + 1]
            vs = slice(h * RET_DV, (h + 1) * RET_DV)
            s_new = gamma[h] * s_ref[0, i, h] + kcol * v_ref[i:i + 1, vs]
            so_ref[own, i, h] = s_new
            o_scr[i:i + 1, vs] = jnp.sum(qcol * s_new, axis=0, keepdims=True)
    _zero_fill(so_ref, fill)
    for h in range(RET_HEADS):
        vs = slice(h * RET_DV, (h + 1) * RET_DV)
        y_ref[:, vs] = _silu(rg_ref[:, vs]) * _head_layernorm(o_scr[:, vs], gain_ref[:, vs])


def _ret_dec(layer, state, carried, c_ret, proj, gain):
    tb = DEC_TOKENS
    gamma = tuple(float(v) for v in np.exp(_ret_log_gamma()))
    return _state_update_call(
        functools.partial(_ret_dec_kernel, gamma=gamma), "ret_dec", layer, state, carried, proj.shape[0],
        RET_HEADS,
        [pl.BlockSpec((1, 2 * RET_HEADS * RET_DK, tb), lambda t, j: (t, 0, 0)),
         pl.BlockSpec((tb, 1024), lambda t, j: (t, 1)), pl.BlockSpec((tb, 1024), lambda t, j: (t, 2)),
         pl.BlockSpec((1, 1024), lambda t, j: (0, 0))],
        (_cols(c_ret, tb), proj, proj, gain), pltpu.VMEM((tb, 1024), F32))


def _hgrn_dec_kernel(s_ref, carried_ref, ct_ref, v_ref, hg_ref, gain_ref, y_ref, so_ref, o_scr, *, own, fill):
    del carried_ref
    for i in range(DEC_TOKENS):
        for h in range(HG_HEADS):
            sl = slice(h * LANES, (h + 1) * LANES)
            qcol = ct_ref[0, h * HG_EXPAND:(h + 1) * HG_EXPAND, i:i + 1]
            fcol = ct_ref[0, (HG_HEADS + h) * HG_EXPAND:(HG_HEADS + h + 1) * HG_EXPAND, i:i + 1]
            s_new = fcol * s_ref[0, i, h] + (1.0 - fcol) * v_ref[i:i + 1, sl]
            so_ref[own, i, h] = s_new
            o_scr[i:i + 1, sl] = jnp.sum(qcol * s_new, axis=0, keepdims=True)
    _zero_fill(so_ref, fill)
    for h in range(HG_HEADS):
        sl = slice(h * LANES, (h + 1) * LANES)
        o = o_scr[:, sl]
        o = o * lax.rsqrt(jnp.mean(o * o, axis=-1, keepdims=True) + EPS) * gain_ref[:, sl]
        y_ref[:, sl] = o * _sigmoid(hg_ref[:, sl])


def _hgrn_dec(layer, state, carried, c_hg, proj, gain):
    tb = DEC_TOKENS
    return _state_update_call(
        _hgrn_dec_kernel, "hgrn_dec", layer, state, carried, proj.shape[0], HG_HEADS,
        [pl.BlockSpec((1, 2 * HG_HEADS * HG_EXPAND, tb), lambda t, j: (t, 0, 0)),
         pl.BlockSpec((tb, 1024), lambda t, j: (t, 5)), pl.BlockSpec((tb, 1024), lambda t, j: (t, 6)),
         pl.BlockSpec((1, 1024), lambda t, j: (0, 0))],
        (_cols(c_hg, tb), proj, proj, gain), pltpu.VMEM((tb, 1024), F32))


def _ssd_dec_kernel(s_ref, carried_ref, c_ref, b_ref, xdt_ref, drow_ref, xs_ref, z_ref, dsk_ref, norm_ref,
                    y_ref, so_ref, o_scr, *, own, fill):
    del carried_ref
    for g in range(SSM_GROUPS):
        c_rows = c_ref[:, g * SSM_STATE:(g + 1) * SSM_STATE].astype(BF16)
        for i in range(DEC_TOKENS):
            b_row = b_ref[i:i + 1, g * SSM_STATE:(g + 1) * SSM_STATE]
            new = []
            for h in range(g * HEADS_PER_GROUP, (g + 1) * HEADS_PER_GROUP):
                hs = slice(h * SSM_HEADDIM, (h + 1) * SSM_HEADDIM)
                decay = drow_ref[i:i + 1, h * SSM_HEADDIM:h * SSM_HEADDIM + 1]
                s_new = decay * s_ref[0, i, h] + xdt_ref[0, hs, i:i + 1] * b_row
                so_ref[own, i, h] = s_new
                new.append(s_new)
            s_grp = jnp.concatenate(new, axis=0).astype(BF16)
            o_scr[i:i + 1, g * GROUP_W:(g + 1) * GROUP_W] = _dot_nt(c_rows, s_grp)[i:i + 1, :]
    _zero_fill(so_ref, fill)
    o = o_scr[...] + dsk_ref[...] * xs_ref[...]
    y_ref[...] = _rms(o * _silu(z_ref[...]), norm_ref[...])


def _ssd_dec(layer, state, carried, c_ssd, xs, xdt, drow, proj, dsk_exp, norm):
    tb = DEC_TOKENS
    rows = lambda col: pl.BlockSpec((tb, 1024), lambda t, j: (t, col))
    half = lambda col: pl.BlockSpec((tb, SSM_GROUPS * SSM_STATE), lambda t, j: (t, col))
    vec = pl.BlockSpec((1, 1024), lambda t, j: (0, 0))
    return _state_update_call(
        _ssd_dec_kernel, "ssd_dec", layer, state, carried, proj.shape[0], SSM_HEADS,
        [half(0), half(1), pl.BlockSpec((1, SSM_INNER, tb), lambda t, j: (t, 0, 0)),
         rows(0), rows(0), rows(7), vec, vec],
        (c_ssd, c_ssd, _cols(xdt, tb), drow, xs, proj, dsk_exp, norm), pltpu.VMEM((tb, 1024), F32))


def kernel(x_prompt, x_sample, state_ret, state_hgrn, state_ssm, state_conv, w_in, ret_norm, hg_norm,
           hg_lb_logits, conv_w, conv_b, dt_bias, a_log, d_skip, ssm_norm, w_branch, w_out, norm_mix_pre,
           norm_mix_post, norm_ffn_pre, norm_ffn_post, w_up, w_down):
    bp, lp, _ = x_prompt.shape
    bs, ls, _ = x_sample.shape
    assert ls == 1 and lp % CHUNK_ROWS == 0 and bs % DEC_TOKENS == 0
    depth = w_in.shape[0]

    lb_w = jax.nn.softmax(hg_lb_logits.astype(F32), axis=0)
    lb_all = jnp.cumsum(lb_w, axis=0) - lb_w[0]

    hp = x_prompt.reshape(bp * lp, D_MODEL)
    hs = x_sample.reshape(bs, D_MODEL)
    outs = [[] for _ in range(5)]
    new_ret = new_hg = new_ssm = None
    ssm_t = jnp.swapaxes(state_ssm, -1, -2)
    w_in_t = jnp.swapaxes(w_in, 1, 2)
    vec = lambda a: a.reshape(1, -1).astype(F32)
    hp_normed = hs_normed = None
    for l in range(depth):
        next_gain = vec(norm_mix_pre[l + 1]) if l + 1 < depth else None
        w_gates = w_in_t[l, N_MAIN + SSM_HEADS:].astype(BF16)
        w_dt = jnp.pad(w_in_t[l, N_MAIN:N_MAIN + SSM_HEADS], ((0, DT_PAD - SSM_HEADS), (0, 0))).astype(BF16)
        wb, wo = w_branch[l].astype(BF16), w_out[l].astype(BF16)
        dsk_exp = jnp.repeat(d_skip[l].astype(F32), SSM_HEADDIM).reshape(1, SSM_INNER)
        lb = vec(lb_all[l])

        proj, xn = _in_proj(hp if hp_normed is None else hp_normed,
                            vec(norm_mix_pre[l]) if hp_normed is None else None,
                            w_in_t, l, w_gates, tm=min(2048, bp * lp), out_dtype=BF16)
        yr, s_ret = _ret_prompt(proj, vec(ret_norm[l]), bp, lp)
        yh, s_hg = _hgrn_prompt(proj, lb, vec(hg_norm[l]), bp, lp)
        ym, s_ssm, s_conv = _ssd_prompt(proj, xn, w_dt, conv_w[l], conv_b[l], dt_bias[l], a_log[l],
                                        dsk_exp, vec(ssm_norm[l]), bp, lp)
        hp, xn = _merge(hp, yr, yh, ym, proj, wb, wo, vec(norm_mix_post[l]), vec(norm_ffn_pre[l]), tm=512)
        hp, hp_normed = _ffn(hp, xn, vec(norm_ffn_post[l]), next_gain, w_up, w_down, l, tm=1024)
        for lst, val in zip(outs[:4], (s_ret, s_hg, s_ssm, s_conv)):
            lst.append(val)

        proj, xn = _in_proj(hs if hs_normed is None else hs_normed,
                            vec(norm_mix_pre[l]) if hs_normed is None else None,
                            w_in_t, l, w_gates, tm=bs, out_dtype=F32)
        c_ret, c_hg, c_ssd, xs, xdt, drow, conv_new = _dec_prep(
            proj, xn, w_dt, state_conv[l].transpose(1, 0, 2), lb, conv_w[l], conv_b[l], dt_bias[l], a_log[l])
        yr, new_ret = _ret_dec(l, state_ret, new_ret, c_ret, proj, vec(ret_norm[l]))
        yh, new_hg = _hgrn_dec(l, state_hgrn, new_hg, c_hg, proj, vec(hg_norm[l]))
        ym, new_ssm = _ssd_dec(l, ssm_t, new_ssm, c_ssd, xs, xdt, drow, proj, dsk_exp, vec(ssm_norm[l]))
        hs, xn = _merge(hs, yr, yh, ym, proj, wb, wo, vec(norm_mix_post[l]), vec(norm_ffn_pre[l]), tm=bs)
        hs, hs_normed = _ffn(hs, xn, vec(norm_ffn_post[l]), next_gain, w_up, w_down, l, tm=bs)
        outs[4].append(conv_new.transpose(1, 0, 2))

    p_ret, p_hg, p_ssm, p_conv, s_conv = [jnp.stack(o) for o in outs]
    return (hp.reshape(bp, lp, D_MODEL), hs.reshape(bs, ls, D_MODEL), p_ret, p_hg, p_ssm, p_conv,
            new_ret, new_hg, jnp.swapaxes(new_ssm, -1, -2), s_conv)
```

```python
import functools

import numpy as np
import jax
import jax.numpy as jnp
from jax import lax
from jax.experimental import pallas as pl
from jax.experimental.pallas import tpu as pltpu

F32 = jnp.float32
BF16 = jnp.bfloat16

D_MODEL = 1024
PAST_LEN = 16384
RET_HEADS, RET_DK, RET_DV = 4, 128, 256
ROPE_BASE = 10000.0
HG_HEADS, HG_EXPAND, HG_DV = 8, 128, 128
SSM_HEADS, SSM_HEADDIM, SSM_GROUPS, SSM_STATE, SSM_CONV = 16, 64, 4, 128, 4
SSM_INNER = SSM_HEADS * SSM_HEADDIM
SSM_CONV_CH = SSM_INNER + 2 * SSM_GROUPS * SSM_STATE
N_BRANCH = 3
D_FF = 4 * D_MODEL
EPS = 1e-6
HG_CLIP_MIN = 1e-6

N_MAIN = 10240
N_PROJ = N_MAIN + 3 * D_MODEL
DT_PAD = 128

LANES = 128
SUBLANES = 8
VMEM_LIMIT = 56 * 1024 * 1024

CHUNK_ROWS = 256
DEC_TOKENS = 8


def _cparams(sem):
    return pltpu.CompilerParams(dimension_semantics=sem, vmem_limit_bytes=VMEM_LIMIT)


def _dot(a, b):
    return jnp.dot(a, b, preferred_element_type=F32)


def _dot_nt(a, b):
    return lax.dot_general(a, b, (((1,), (1,)), ((), ())), preferred_element_type=F32)


def _dot_tn(a, b):
    return lax.dot_general(a, b, (((0,), (0,)), ((), ())), preferred_element_type=F32)


def _split3(a):
    hi = a.astype(BF16)
    r1 = a - hi.astype(F32)
    mid = r1.astype(BF16)
    lo = (r1 - mid.astype(F32)).astype(BF16)
    return hi, mid, lo


def _dot3_left(m01, a):
    hi, mid, lo = _split3(a)
    return _dot(m01, hi) + _dot(m01, mid) + _dot(m01, lo)


def _dot3_right(a, m01):
    hi, mid, lo = _split3(a)
    return _dot(hi, m01) + _dot(mid, m01) + _dot(lo, m01)


def _sigmoid(x):
    return 1.0 / (1.0 + jnp.exp(-x))


def _silu(x):
    return x * _sigmoid(x)


def _softplus(x):
    return jnp.maximum(x, 0.0) + jnp.log1p(jnp.exp(-jnp.abs(x)))


def _rms(x, gain):
    return x * lax.rsqrt(jnp.mean(x * x, axis=-1, keepdims=True) + EPS) * gain


PROJ_TILE = 1024
MAIN_TILES = N_MAIN // PROJ_TILE
GATE_TILE0 = MAIN_TILES


def _project_tile(xn_ref, w_ref, o_ref):
    o_ref[...] = _dot_nt(xn_ref[...], w_ref[0].astype(BF16)).astype(o_ref.dtype)


def _in_proj_kernel(x_ref, g_ref, w_ref, o_ref, xn_ref):
    @pl.when(pl.program_id(1) == 0)
    def _():
        xn_ref[...] = _rms(x_ref[...], g_ref[...]).astype(BF16)

    _project_tile(xn_ref, w_ref, o_ref)


def _in_proj(x, gain, w_in_t, layer, tm, out_dtype):
    t = x.shape[0]
    tn = PROJ_TILE
    rows = pl.BlockSpec((tm, D_MODEL), lambda i, j: (i, 0))
    first_row = lambda j: (j * (tn // SUBLANES) + jnp.where(j >= MAIN_TILES, SSM_HEADS // SUBLANES, 0)) * SUBLANES
    w_spec = pl.BlockSpec((pl.Element(1), pl.Element(tn), pl.Element(D_MODEL)),
                          lambda i, j: (layer, first_row(j), 0))
    common = dict(grid=(t // tm, N_PROJ // tn), compiler_params=_cparams(("parallel", "arbitrary")),
                  name="in_proj")
    out_spec = pl.BlockSpec((tm, tn), lambda i, j: (i, j))
    out_shape = jax.ShapeDtypeStruct((t, N_PROJ), out_dtype)
    if gain is None:
        proj = pl.pallas_call(_project_tile, in_specs=[rows, w_spec], out_specs=out_spec,
                              out_shape=out_shape, **common)(x, w_in_t)
        return proj, x
    return pl.pallas_call(
        _in_proj_kernel, in_specs=[rows, pl.BlockSpec((1, D_MODEL), lambda i, j: (0, 0)), w_spec],
        out_specs=[out_spec, rows], out_shape=[out_shape, jax.ShapeDtypeStruct((t, D_MODEL), BF16)], **common,
    )(x, gain, w_in_t)


def _ret_log_gamma():
    return np.log1p(-np.exp2(-5.0 - np.arange(RET_HEADS, dtype=np.float64)))


def _ret_consts(r):
    lg = _ret_log_gamma()
    i = np.arange(r)
    diff = (i[:, None] - i[None, :]).astype(np.float64)
    dec = np.where(diff >= 0, np.exp(lg[:, None, None] * np.maximum(diff, 0.0)), 0.0)
    qd = np.exp(lg[:, None] * (i + 1.0))
    kd = np.exp(lg[:, None] * (r - 1.0 - i))
    gl = tuple(float(v) for v in np.exp(lg * r))
    bc = lambda a: jnp.asarray(np.broadcast_to(a[:, :, None], (RET_HEADS, r, LANES)), F32)
    return jnp.asarray(dec, F32), bc(qd), bc(kd), gl


def _rope_tables(pos):
    half = RET_DK // 2
    inv = ROPE_BASE ** (-(np.arange(half, dtype=np.float64) / half))
    ang = pos.astype(np.float64)[:, None] * inv[None, :]
    cos, sin = np.cos(ang), np.sin(ang)
    cos2 = np.concatenate([cos, cos], axis=1)
    sin2 = np.concatenate([-sin, sin], axis=1)
    scale = RET_DK ** -0.5
    return (jnp.asarray(cos2, F32), jnp.asarray(sin2, F32),
            jnp.asarray(cos2 * scale, F32), jnp.asarray(sin2 * scale, F32))


def _rope(t, cos2, sin2):
    return t * cos2 + pltpu.roll(t, RET_DK // 2, 1) * sin2


def _head_layernorm(o, gain):
    oc = o - jnp.mean(o, axis=-1, keepdims=True)
    return oc * lax.rsqrt(jnp.mean(oc * oc, axis=-1, keepdims=True) + EPS) * gain


def _ret_kernel(rq_ref, rk_ref, rv_ref, rg_ref, cq_ref, sq_ref, ck_ref, sk_ref, dec_ref, qd_ref, kd_ref,
                gain_ref, y_ref, sout_ref, s_scr, *, gl):
    c = pl.program_id(1)

    @pl.when(c == 0)
    def _():
        s_scr[...] = jnp.zeros_like(s_scr)

    cq, sq, ck, sk = cq_ref[...], sq_ref[...], ck_ref[...], sk_ref[...]
    for h in range(RET_HEADS):
        ks = slice(h * RET_DK, (h + 1) * RET_DK)
        vs = slice(h * RET_DV, (h + 1) * RET_DV)
        q = _rope(rq_ref[:, ks].astype(F32), cq, sq)
        k = _rope(rk_ref[:, ks].astype(F32), ck, sk)
        v = rv_ref[:, vs].astype(BF16)
        att = (_dot_nt(q.astype(BF16), k.astype(BF16)) * dec_ref[h]).astype(BF16)
        s_old = s_scr[h]
        o = _dot(att, v) + _dot((q * qd_ref[h]).astype(BF16), s_old.astype(BF16))
        s_scr[h] = gl[h] * s_old + _dot_tn((k * kd_ref[h]).astype(BF16), v)
        y = _silu(rg_ref[:, vs].astype(F32)) * _head_layernorm(o, gain_ref[:, vs])
        y_ref[:, vs] = y.astype(y_ref.dtype)

    @pl.when(c == pl.num_programs(1) - 1)
    def _():
        sout_ref[0] = s_scr[...]


def _ret_prompt(proj, gain, batch, seq):
    r = CHUNK_ROWS
    nc = seq // r
    dec, qd, kd, gl = _ret_consts(r)
    cq, sq, ck, sk = _rope_tables(np.arange(seq))
    row = lambda b, c: b * nc + c
    tab = pl.BlockSpec((r, RET_DK), lambda b, c: (c, 0))
    const3 = lambda shape: pl.BlockSpec(shape, lambda b, c: (0, 0, 0))
    return pl.pallas_call(
        functools.partial(_ret_kernel, gl=gl),
        grid=(batch, nc),
        in_specs=[
            pl.BlockSpec((r, 512), lambda b, c: (row(b, c), 0)),
            pl.BlockSpec((r, 512), lambda b, c: (row(b, c), 1)),
            pl.BlockSpec((r, 1024), lambda b, c: (row(b, c), 1)),
            pl.BlockSpec((r, 1024), lambda b, c: (row(b, c), 2)),
            tab, tab, tab, tab,
            const3((RET_HEADS, r, r)), const3((RET_HEADS, r, LANES)), const3((RET_HEADS, r, LANES)),
            pl.BlockSpec((1, 1024), lambda b, c: (0, 0)),
        ],
        out_specs=[
            pl.BlockSpec((r, 1024), lambda b, c: (row(b, c), 0)),
            pl.BlockSpec((1, RET_HEADS, RET_DK, RET_DV), lambda b, c: (b, 0, 0, 0)),
        ],
        out_shape=[jax.ShapeDtypeStruct((batch * seq, 1024), BF16),
                   jax.ShapeDtypeStruct((batch, RET_HEADS, RET_DK, RET_DV), F32)],
        scratch_shapes=[pltpu.VMEM((RET_HEADS, RET_DK, RET_DV), F32)],
        compiler_params=_cparams(("parallel", "arbitrary")),
        name="ret_prompt",
    )(proj, proj, proj, proj, cq, sq, ck, sk, dec, qd, kd, gain)


HG_CHUNK = 128
HG_LEVELS = (64, 32, 16, 8)
HG_TILE_REF = SUBLANES // 2 - 1
LOG2E = 1.4426950408889634


def _hg_masks():
    i = np.arange(HG_CHUNK)
    out = [(i[:, None] // (2 * b)) == (i[None, :] // (2 * b)) for b in HG_LEVELS[1:]]
    out.append(((i[:, None] // SUBLANES) == (i[None, :] // SUBLANES)) & (i[:, None] >= i[None, :]))
    return jnp.asarray(np.stack(out).astype(np.float32))


def _tri_ones(r):
    i = np.arange(r)
    return jnp.asarray((i[:, None] >= i[None, :]).astype(np.float32), BF16)


def _halves(a, nb):
    a5 = a.reshape(HG_CHUNK // (2 * nb * SUBLANES), 2, nb, SUBLANES, a.shape[-1])
    return a5[:, 0], a5[:, 1]


def _join_halves(lower, upper):
    return jnp.stack([lower, upper], axis=1).reshape(HG_CHUNK, lower.shape[-1])


def _tile_row_bcast(a, j):
    a3 = a.reshape(HG_CHUNK // SUBLANES, SUBLANES, a.shape[-1])
    return jnp.broadcast_to(a3[:, j:j + 1, :], a3.shape).reshape(a.shape)


def _per_head_nt(a, b):
    return [_dot_nt(a[:, h * LANES:(h + 1) * LANES], b[:, h * LANES:(h + 1) * LANES]) for h in range(HG_HEADS)]


def _hg_chunk_att(q, kk, g2, mask_ref):
    tile_last = _tile_row_bcast(g2, SUBLANES - 1)
    att = None
    for lvl, b in enumerate(HG_LEVELS):
        nb = b // SUBLANES
        _, q_up = _halves(q, nb)
        k_lo, _ = _halves(kk, nb)
        g_lo, g_up = _halves(g2, nb)
        ref = _halves(tile_last, nb)[0][:, nb - 1:nb]
        qe = q_up * jnp.exp2(g_up - ref)
        ke = k_lo * jnp.exp2(ref - g_lo)
        zeros = jnp.zeros_like(qe)
        prods = _per_head_nt(_join_halves(zeros, qe).astype(BF16), _join_halves(ke, zeros).astype(BF16))
        att = prods if lvl == 0 else [a + p * mask_ref[lvl - 1] for a, p in zip(att, prods)]
    d = g2 - _tile_row_bcast(g2, HG_TILE_REF)
    prods = _per_head_nt((q * jnp.exp2(d)).astype(BF16), (kk * jnp.exp2(-d)).astype(BF16))
    same_tile = mask_ref[len(HG_LEVELS) - 1] > 0.5
    return [a + jnp.where(same_tile, p, 0.0) for a, p in zip(att, prods)]


def _hgrn_kernel(hq_ref, hf_ref, hi_ref, hg_ref, lb_ref, gain_ref, tri_ref, mask_ref, y_ref, sout_ref,
                 st_scr, *, r):
    c = pl.program_id(1)

    @pl.when(c == 0)
    def _():
        st_scr[...] = jnp.zeros_like(st_scr)

    lb = lb_ref[...]
    for ck in range(r // HG_CHUNK):
        rows = slice(ck * HG_CHUNK, (ck + 1) * HG_CHUNK)
        q = _silu(hq_ref[rows, :].astype(F32))
        f = jnp.clip(lb + (1.0 - lb) * _sigmoid(hf_ref[rows, :].astype(F32)), HG_CLIP_MIN, 1.0)
        kk = 1.0 - f
        g2 = _dot3_left(tri_ref[...], jnp.log(f)) * LOG2E
        g2_last = g2[HG_CHUNK - 1:HG_CHUNK, :]
        att = _hg_chunk_att(q, kk, g2, mask_ref)
        q_in = (q * jnp.exp2(g2)).astype(BF16)
        k_out = (kk * jnp.exp2(g2_last - g2)).astype(BF16)
        e_last = jnp.exp2(g2_last)
        gate = _sigmoid(hg_ref[rows, :].astype(F32))
        for h in range(HG_HEADS):
            sl = slice(h * LANES, (h + 1) * LANES)
            v = hi_ref[rows, sl]
            st_old = st_scr[h]
            o = _dot(att[h].astype(BF16), v) + _dot_nt(q_in[:, sl], st_old.astype(BF16))
            st_scr[h] = st_old * e_last[:, sl] + _dot_tn(v, k_out[:, sl])
            o = o * lax.rsqrt(jnp.mean(o * o, axis=-1, keepdims=True) + EPS) * gain_ref[:, sl]
            y_ref[rows, sl] = (o * gate[:, sl]).astype(y_ref.dtype)

    @pl.when(c == pl.num_programs(1) - 1)
    def _():
        for h in range(HG_HEADS):
            sout_ref[0, h] = st_scr[h].T


def _hgrn_prompt(proj, lb, gain, batch, seq):
    r = CHUNK_ROWS
    nc = seq // r
    masks = _hg_masks()
    row = lambda b, c: b * nc + c
    blk = lambda j: pl.BlockSpec((r, 1024), lambda b, c: (row(b, c), j))
    vec = pl.BlockSpec((1, 1024), lambda b, c: (0, 0))
    return pl.pallas_call(
        functools.partial(_hgrn_kernel, r=r),
        grid=(batch, nc),
        in_specs=[blk(3), blk(4), blk(5), blk(6), vec, vec,
                  pl.BlockSpec((HG_CHUNK, HG_CHUNK), lambda b, c: (0, 0)),
                  pl.BlockSpec(masks.shape, lambda b, c: (0, 0, 0))],
        out_specs=[
            pl.BlockSpec((r, 1024), lambda b, c: (row(b, c), 0)),
            pl.BlockSpec((1, HG_HEADS, HG_EXPAND, HG_DV), lambda b, c: (b, 0, 0, 0)),
        ],
        out_shape=[jax.ShapeDtypeStruct((batch * seq, 1024), BF16),
                   jax.ShapeDtypeStruct((batch, HG_HEADS, HG_EXPAND, HG_DV), F32)],
        scratch_shapes=[pltpu.VMEM((HG_HEADS, HG_DV, HG_EXPAND), F32)],
        compiler_params=_cparams(("parallel", "arbitrary")),
        name="hgrn_prompt",
    )(proj, proj, proj, proj, lb, gain, _tri_ones(HG_CHUNK), masks)


HEADS_PER_GROUP = SSM_HEADS // SSM_GROUPS
GROUP_W = HEADS_PER_GROUP * SSM_HEADDIM
CONV_PAD = SUBLANES
SSD_CHUNK = 128


def _head_expand(vals, h0, lane_head):
    out = jnp.zeros((vals.shape[0], GROUP_W), F32)
    for j in range(HEADS_PER_GROUP):
        out = jnp.where(lane_head == j, vals[:, h0 + j:h0 + j + 1], out)
    return out


def _ssd_kernel(z_ref, xbc_ref, xn_ref, wdt_ref, cw_ref, cb_ref, dtb_ref, alog_ref, dsk_ref, norm_ref, tri_ref,
                shift_ref, y_ref, sout_ref, cout_ref, hist_scr, s_scr, o_scr, *, r):
    c = pl.program_id(1)
    last = pl.num_programs(1) - 1

    @pl.when(c == 0)
    def _():
        hist_scr[0:CONV_PAD, :] = jnp.zeros((CONV_PAD, SSM_CONV_CH), F32)
        s_scr[...] = jnp.zeros_like(s_scr)

    u_bf = xbc_ref[...]
    u = u_bf.astype(F32)
    hist_scr[CONV_PAD:2 * CONV_PAD, :] = u[0:CONV_PAD]
    conv = cb_ref[...] + cw_ref[SSM_CONV - 1:SSM_CONV, :] * u
    head = cb_ref[...]
    for w in range(SSM_CONV):
        off = CONV_PAD - (SSM_CONV - 1) + w
        head = head + cw_ref[w:w + 1, :] * hist_scr[off:off + CONV_PAD, :]
    for back in range(1, SSM_CONV):
        conv = conv + cw_ref[SSM_CONV - 1 - back:SSM_CONV - back, :] * _dot(shift_ref[back - 1], u_bf)
    conv = jnp.concatenate([head, conv[CONV_PAD:]], axis=0)

    @pl.when(c == last)
    def _():
        cout_ref[0] = u[r - (SSM_CONV - 1):r, :]

    hist_scr[0:CONV_PAD, :] = u[r - CONV_PAD:r]

    xc = _silu(conv)
    xs = xc[:, :SSM_INNER]
    bm = xc[:, SSM_INNER:SSM_INNER + SSM_GROUPS * SSM_STATE]
    cm = xc[:, SSM_INNER + SSM_GROUPS * SSM_STATE:]

    dt_all = _softplus(_dot_nt(xn_ref[...], wdt_ref[...].astype(BF16)) + dtb_ref[...])
    rate = -jnp.exp(alog_ref[...]) * LOG2E
    rc = SSD_CHUNK
    ri = lax.broadcasted_iota(jnp.int32, (rc, rc), 0)
    ci = lax.broadcasted_iota(jnp.int32, (rc, rc), 1)
    causal = ri >= ci
    lane_head = lax.broadcasted_iota(jnp.int32, (rc, GROUP_W), 1) // SSM_HEADDIM
    lane_head1 = lax.broadcasted_iota(jnp.int32, (1, GROUP_W), 1) // SSM_HEADDIM
    groups = range(SSM_GROUPS)

    for ck in range(r // rc):
        rows = slice(ck * rc, (ck + 1) * rc)
        dt = dt_all[rows]
        g2 = _dot3_left(tri_ref[...], dt * rate)
        g2_last = g2[rc - 1:rc, :]
        g2_t = g2.T
        dt_t = dt.T
        e_cum = jnp.exp2(g2)
        w_in = dt * jnp.exp2(g2_last - g2)
        e_last = jnp.exp2(g2_last)

        cg = [cm[rows, g * SSM_STATE:(g + 1) * SSM_STATE].astype(BF16) for g in groups]
        bg = [bm[rows, g * SSM_STATE:(g + 1) * SSM_STATE].astype(BF16) for g in groups]
        xs_g = [xs[rows, g * GROUP_W:(g + 1) * GROUP_W] for g in groups]
        xs_b = [x.astype(BF16) for x in xs_g]
        s_old = [s_scr[g] for g in groups]
        cb = [jnp.where(causal, _dot_nt(cg[g], bg[g]), 0.0) for g in groups]
        carried = [_dot(cg[g], s_old[g].astype(BF16)) for g in groups]
        for g in groups:
            h0 = g * HEADS_PER_GROUP
            o_g = _head_expand(e_cum, h0, lane_head) * carried[g]
            for j in range(HEADS_PER_GROUP):
                h = h0 + j
                dec = jnp.exp2(jnp.minimum(g2[:, h:h + 1] - g2_t[h:h + 1, :], 0.0))
                att = (cb[g] * dec * dt_t[h:h + 1, :]).astype(BF16)
                o_g = o_g + jnp.where(lane_head == j, _dot(att, xs_b[g]), 0.0)
            s_scr[g] = (s_old[g] * _head_expand(e_last, h0, lane_head1)
                        + _dot_tn(bg[g], (xs_g[g] * _head_expand(w_in, h0, lane_head)).astype(BF16)))
            o_scr[rows, g * GROUP_W:(g + 1) * GROUP_W] = o_g + dsk_ref[:, g * GROUP_W:(g + 1) * GROUP_W] * xs_g[g]

    y_ref[...] = _rms(o_scr[...] * _silu(z_ref[...].astype(F32)), norm_ref[...]).astype(y_ref.dtype)

    @pl.when(c == last)
    def _():
        for h in range(SSM_HEADS):
            j = h % HEADS_PER_GROUP
            sout_ref[0, h] = s_scr[h // HEADS_PER_GROUP][:, j * SSM_HEADDIM:(j + 1) * SSM_HEADDIM]


def _shift_ones(r):
    i = np.arange(r)
    mats = [(i[:, None] - i[None, :]) == k for k in range(1, SSM_CONV)]
    return jnp.asarray(np.stack(mats).astype(np.float32), BF16)


def _pad_lanes(v):
    return jnp.pad(v.astype(F32), (0, LANES - v.shape[0])).reshape(1, LANES)


def _ssd_prompt(proj, xn, w_dt, conv_w, conv_b, dt_bias, a_log, dsk_exp, norm, batch, seq):
    r = CHUNK_ROWS
    nc = seq // r
    row = lambda b, c: b * nc + c
    vec = lambda n: pl.BlockSpec((1, n), lambda b, c: (0, 0))
    return pl.pallas_call(
        functools.partial(_ssd_kernel, r=r),
        grid=(batch, nc),
        in_specs=[
            pl.BlockSpec((r, 1024), lambda b, c: (row(b, c), 7)),
            pl.BlockSpec((r, SSM_CONV_CH), lambda b, c: (row(b, c), 4)),
            pl.BlockSpec((r, D_MODEL), lambda b, c: (row(b, c), 0)),
            pl.BlockSpec((DT_PAD, D_MODEL), lambda b, c: (0, 0)),
            pl.BlockSpec((SSM_CONV, SSM_CONV_CH), lambda b, c: (0, 0)),
            vec(SSM_CONV_CH), vec(LANES), vec(LANES), vec(SSM_INNER), vec(SSM_INNER),
            pl.BlockSpec((SSD_CHUNK, SSD_CHUNK), lambda b, c: (0, 0)),
            pl.BlockSpec((SSM_CONV - 1, r, r), lambda b, c: (0, 0, 0)),
        ],
        out_specs=[
            pl.BlockSpec((r, 1024), lambda b, c: (row(b, c), 0)),
            pl.BlockSpec((1, SSM_HEADS, SSM_STATE, SSM_HEADDIM), lambda b, c: (b, 0, 0, 0)),
            pl.BlockSpec((1, SSM_CONV - 1, SSM_CONV_CH), lambda b, c: (b, 0, 0)),
        ],
        out_shape=[jax.ShapeDtypeStruct((batch * seq, 1024), BF16),
                   jax.ShapeDtypeStruct((batch, SSM_HEADS, SSM_STATE, SSM_HEADDIM), F32),
                   jax.ShapeDtypeStruct((batch, SSM_CONV - 1, SSM_CONV_CH), F32)],
        scratch_shapes=[pltpu.VMEM((2 * CONV_PAD, SSM_CONV_CH), F32),
                        pltpu.VMEM((SSM_GROUPS, SSM_STATE, GROUP_W), F32),
                        pltpu.VMEM((r, SSM_INNER), F32)],
        compiler_params=_cparams(("parallel", "arbitrary")),
        name="ssd_prompt",
    )(proj, proj, xn, w_dt, conv_w, conv_b.reshape(1, -1), _pad_lanes(dt_bias), _pad_lanes(a_log), dsk_exp, norm,
      _tri_ones(SSD_CHUNK), _shift_ones(r))


def _merge_kernel(x_ref, yr_ref, yh_ref, ym_ref, g0_ref, g1_ref, g2_ref, wb_ref, wo_ref, npost_ref, nnext_ref,
                  o_ref, xn_ref):
    merged = _sigmoid(g0_ref[...].astype(F32)) * _dot(yr_ref[...].astype(BF16), wb_ref[0])
    merged = merged + _sigmoid(g1_ref[...].astype(F32)) * _dot(yh_ref[...].astype(BF16), wb_ref[1])
    merged = merged + _sigmoid(g2_ref[...].astype(F32)) * _dot(ym_ref[...].astype(BF16), wb_ref[2])
    m = _dot(merged.astype(BF16), wo_ref[...])
    out = x_ref[...] + _rms(m, npost_ref[...])
    o_ref[...] = out
    xn_ref[...] = _rms(out, nnext_ref[...]).astype(BF16)


def _merge(x, yr, yh, ym, proj, wb, wo, npost, nnext, tm):
    t = x.shape[0]
    rows = lambda j: pl.BlockSpec((tm, D_MODEL), lambda i: (i, j))
    vec = pl.BlockSpec((1, D_MODEL), lambda i: (0, 0))
    return pl.pallas_call(
        _merge_kernel,
        grid=(t // tm,),
        in_specs=[rows(0), rows(0), rows(0), rows(0), rows(GATE_TILE0), rows(GATE_TILE0 + 1), rows(GATE_TILE0 + 2),
                  pl.BlockSpec((3, D_MODEL, D_MODEL), lambda i: (0, 0, 0)),
                  pl.BlockSpec((D_MODEL, D_MODEL), lambda i: (0, 0)), vec, vec],
        out_specs=[rows(0), rows(0)],
        out_shape=[jax.ShapeDtypeStruct((t, D_MODEL), F32), jax.ShapeDtypeStruct((t, D_MODEL), BF16)],
        compiler_params=_cparams(("parallel",)),
        name="merge",
    )(x, yr, yh, ym, proj, proj, proj, wb, wo, npost, nnext)


def _ffn_kernel(x_ref, xn_ref, wu_ref, wd_ref, npost_ref, *rest):
    (nnext_ref, o_ref, xnn_ref, acc_scr) = rest if len(rest) == 4 else (None, rest[0], None, rest[1])
    j = pl.program_id(1)

    @pl.when(j == 0)
    def _():
        acc_scr[...] = jnp.zeros_like(acc_scr)

    hid = jnp.square(jnp.maximum(_dot(xn_ref[...], wu_ref[0].astype(BF16)), 0.0))
    acc_scr[...] += _dot(hid.astype(BF16), wd_ref[0].astype(BF16))

    @pl.when(j == pl.num_programs(1) - 1)
    def _():
        out = x_ref[...] + _rms(acc_scr[...], npost_ref[...])
        o_ref[...] = out
        if xnn_ref is not None:
            xnn_ref[...] = _rms(out, nnext_ref[...]).astype(BF16)


def _ffn(x, xn, npost, nnext, w_up, w_down, layer, tm):
    t = x.shape[0]
    tf = 1024
    vec = pl.BlockSpec((1, D_MODEL), lambda i, j: (0, 0))
    rows = pl.BlockSpec((tm, D_MODEL), lambda i, j: (i, 0))
    chain = nnext is not None
    wu, wd = w_up, w_down
    outs = pl.pallas_call(
        _ffn_kernel,
        grid=(t // tm, D_FF // tf),
        in_specs=[rows, rows,
                  pl.BlockSpec((1, D_MODEL, tf), lambda i, j: (layer, 0, j)),
                  pl.BlockSpec((1, tf, D_MODEL), lambda i, j: (layer, j, 0)), vec] + ([vec] if chain else []),
        out_specs=[rows, rows] if chain else [rows],
        out_shape=[jax.ShapeDtypeStruct((t, D_MODEL), F32)]
        + ([jax.ShapeDtypeStruct((t, D_MODEL), BF16)] if chain else []),
        scratch_shapes=[pltpu.VMEM((tm, D_MODEL), F32)],
        compiler_params=_cparams(("parallel", "arbitrary")),
        name="ffn",
    )(x, xn, wu, wd, npost, *([nnext] if chain else []))
    return (outs[0], outs[1]) if chain else (outs[0], None)


def _dec_prep_kernel(proj_ref, xn_ref, wdt_ref, cst_ref, cq_ref, sq_ref, ck_ref, sk_ref, lb_ref, cw_ref, cb_ref,
                     dtb_ref, alog_ref, hexp_ref,
                     cret_ref, chg_ref, cssd_ref, xs_ref, xdt_ref, drow_ref, cnew_ref):
    n = proj_ref.shape[0]
    cq, sq, ck, sk = cq_ref[...], sq_ref[...], ck_ref[...], sk_ref[...]
    for h in range(RET_HEADS):
        ks = slice(h * RET_DK, (h + 1) * RET_DK)
        cret_ref[:, ks] = _rope(proj_ref[:, ks], cq, sq)
        k_off = RET_HEADS * RET_DK
        cret_ref[:, k_off + h * RET_DK:k_off + (h + 1) * RET_DK] = _rope(
            proj_ref[:, k_off + h * RET_DK:k_off + (h + 1) * RET_DK], ck, sk)

    lb = lb_ref[...]
    chg_ref[:, 0:1024] = _silu(proj_ref[:, 3072:4096])
    chg_ref[:, 1024:2048] = jnp.clip(lb + (1.0 - lb) * _sigmoid(proj_ref[:, 4096:5120]), HG_CLIP_MIN, 1.0)

    u = proj_ref[:, 8192:8192 + SSM_CONV_CH]
    conv = cb_ref[...] + cw_ref[SSM_CONV - 1:SSM_CONV, :] * u
    for w in range(SSM_CONV - 1):
        conv = conv + cw_ref[w:w + 1, :] * cst_ref[w]
    cnew_ref[0] = cst_ref[1]
    cnew_ref[1] = cst_ref[2]
    cnew_ref[2] = u
    xc = _silu(conv)
    xs = xc[:, :SSM_INNER]
    xs_ref[...] = xs
    cssd_ref[:, 0:512] = xc[:, SSM_INNER + 512:]
    cssd_ref[:, 512:1024] = xc[:, SSM_INNER:SSM_INNER + 512]
    dt = _softplus(_dot_nt(xn_ref[...], wdt_ref[...].astype(BF16)) + dtb_ref[...])
    decay = jnp.exp(dt * (-jnp.exp(alog_ref[...])))
    hexp = hexp_ref[...]
    xdt_ref[...] = xs * _dot3_right(dt, hexp)
    drow_ref[...] = _dot3_right(decay, hexp)
    del n


def _dec_prep(proj, xn, w_dt, conv_state_t, lb, conv_w, conv_b, dt_bias, a_log):
    n = proj.shape[0]
    cq, sq, ck, sk = _rope_tables(np.full((n,), PAST_LEN))
    hexp = np.zeros((LANES, SSM_INNER), np.float32)
    for h in range(SSM_HEADS):
        hexp[h, h * SSM_HEADDIM:(h + 1) * SSM_HEADDIM] = 1.0
    outs = [(n, 1024), (n, 2048), (n, 1024), (n, 1024), (n, 1024), (n, 1024)]
    return pl.pallas_call(
        _dec_prep_kernel,
        out_shape=[jax.ShapeDtypeStruct(s, F32) for s in outs]
        + [jax.ShapeDtypeStruct((SSM_CONV - 1, n, SSM_CONV_CH), F32)],
        compiler_params=pltpu.CompilerParams(vmem_limit_bytes=VMEM_LIMIT),
        name="dec_prep",
    )(proj, xn, w_dt, conv_state_t, cq, sq, ck, sk, lb, conv_w, conv_b.reshape(1, -1), _pad_lanes(dt_bias),
      _pad_lanes(a_log), jnp.asarray(hexp, BF16))


def _cols(a, tb):
    n, ch = a.shape
    return a.reshape(n // tb, tb, ch).transpose(0, 2, 1)


def _state_update_call(body, name, layer, state, carried, n_tokens, heads_per_step, in_specs, operands,
                       scratch):
    depth, _, heads, sk, sv = state.shape
    tb = DEC_TOKENS
    in_spec = pl.BlockSpec((1, tb, heads_per_step, sk, sv), lambda t, j: (layer, t, j, 0, 0))
    if carried is None:
        carried, aliases = state, {}
        out_spec = pl.BlockSpec((depth, tb, heads_per_step, sk, sv), lambda t, j: (0, t, j, 0, 0))
        own, fill = layer, tuple(d for d in range(depth) if d != layer)
    else:
        aliases = {1: 1}
        out_spec, own, fill = in_spec, 0, ()
    return pl.pallas_call(
        functools.partial(body, own=own, fill=fill),
        grid=(n_tokens // tb, heads // heads_per_step),
        in_specs=[in_spec, pl.BlockSpec(memory_space=pl.ANY)] + in_specs,
        out_specs=[pl.BlockSpec((tb, 1024), lambda t, j: (t, 0)), out_spec],
        out_shape=[jax.ShapeDtypeStruct((n_tokens, 1024), F32), jax.ShapeDtypeStruct(state.shape, F32)],
        input_output_aliases=aliases,
        scratch_shapes=[scratch],
        compiler_params=_cparams(("parallel", "arbitrary")),
        name=name,
    )(state, carried, *operands)


def _zero_fill(so_ref, fill):
    for d in fill:
        so_ref[d] = jnp.zeros(so_ref.shape[1:], so_ref.dtype)


def _ret_dec_kernel(s_ref, carried_ref, ct_ref, v_ref, rg_ref, gain_ref, y_ref, so_ref, o_scr, *, gamma, own,
                    fill):
    del carried_ref
    for i in range(DEC_TOKENS):
        for h in range(RET_HEADS):
            qcol = ct_ref[0, h * RET_DK:(h + 1) * RET_DK, i:i + 1]
            kcol = ct_ref[0, (RET_HEADS + h) * RET_DK:(RET_HEADS + h + 1) * RET_DK, i:i + 1]
            vs = slice(h * RET_DV, (h + 1) * RET_DV)
            s_new = gamma[h] * s_ref[0, i, h] + kcol * v_ref[i:i + 1, vs]
            so_ref[own, i, h] = s_new
            o_scr[i:i + 1, vs] = jnp.sum(qcol * s_new, axis=0, keepdims=True)
    _zero_fill(so_ref, fill)
    for h in range(RET_HEADS):
        vs = slice(h * RET_DV, (h + 1) * RET_DV)
        y_ref[:, vs] = _silu(rg_ref[:, vs]) * _head_layernorm(o_scr[:, vs], gain_ref[:, vs])


def _ret_dec(layer, state, carried, c_ret, proj, gain):
    tb = DEC_TOKENS
    gamma = tuple(float(v) for v in np.exp(_ret_log_gamma()))
    return _state_update_call(
        functools.partial(_ret_dec_kernel, gamma=gamma), "ret_dec", layer, state, carried, proj.shape[0],
        RET_HEADS,
        [pl.BlockSpec((1, 2 * RET_HEADS * RET_DK, tb), lambda t, j: (t, 0, 0)),
         pl.BlockSpec((tb, 1024), lambda t, j: (t, 1)), pl.BlockSpec((tb, 1024), lambda t, j: (t, 2)),
         pl.BlockSpec((1, 1024), lambda t, j: (0, 0))],
        (_cols(c_ret, tb), proj, proj, gain), pltpu.VMEM((tb, 1024), F32))


def _hgrn_dec_kernel(s_ref, carried_ref, ct_ref, v_ref, hg_ref, gain_ref, y_ref, so_ref, o_scr, *, own, fill):
    del carried_ref
    for i in range(DEC_TOKENS):
        for h in range(HG_HEADS):
            sl = slice(h * LANES, (h + 1) * LANES)
            qcol = ct_ref[0, h * HG_EXPAND:(h + 1) * HG_EXPAND, i:i + 1]
            fcol = ct_ref[0, (HG_HEADS + h) * HG_EXPAND:(HG_HEADS + h + 1) * HG_EXPAND, i:i + 1]
            s_new = fcol * s_ref[0, i, h] + (1.0 - fcol) * v_ref[i:i + 1, sl]
            so_ref[own, i, h] = s_new
            o_scr[i:i + 1, sl] = jnp.sum(qcol * s_new, axis=0, keepdims=True)
    _zero_fill(so_ref, fill)
    for h in range(HG_HEADS):
        sl = slice(h * LANES, (h + 1) * LANES)
        o = o_scr[:, sl]
        o = o * lax.rsqrt(jnp.mean(o * o, axis=-1, keepdims=True) + EPS) * gain_ref[:, sl]
        y_ref[:, sl] = o * _sigmoid(hg_ref[:, sl])


def _hgrn_dec(layer, state, carried, c_hg, proj, gain):
    tb = DEC_TOKENS
    return _state_update_call(
        _hgrn_dec_kernel, "hgrn_dec", layer, state, carried, proj.shape[0], HG_HEADS,
        [pl.BlockSpec((1, 2 * HG_HEADS * HG_EXPAND, tb), lambda t, j: (t, 0, 0)),
         pl.BlockSpec((tb, 1024), lambda t, j: (t, 5)), pl.BlockSpec((tb, 1024), lambda t, j: (t, 6)),
         pl.BlockSpec((1, 1024), lambda t, j: (0, 0))],
        (_cols(c_hg, tb), proj, proj, gain), pltpu.VMEM((tb, 1024), F32))


def _ssd_dec_kernel(s_ref, carried_ref, c_ref, b_ref, xdt_ref, drow_ref, xs_ref, z_ref, dsk_ref, norm_ref,
                    y_ref, so_ref, o_scr, *, own, fill):
    del carried_ref
    for g in range(SSM_GROUPS):
        c_rows = c_ref[:, g * SSM_STATE:(g + 1) * SSM_STATE].astype(BF16)
        for i in range(DEC_TOKENS):
            b_row = b_ref[i:i + 1, g * SSM_STATE:(g + 1) * SSM_STATE]
            new = []
            for h in range(g * HEADS_PER_GROUP, (g + 1) * HEADS_PER_GROUP):
                hs = slice(h * SSM_HEADDIM, (h + 1) * SSM_HEADDIM)
                decay = drow_ref[i:i + 1, h * SSM_HEADDIM:h * SSM_HEADDIM + 1]
                s_new = decay * s_ref[0, i, h] + xdt_ref[0, hs, i:i + 1] * b_row
                so_ref[own, i, h] = s_new
                new.append(s_new)
            s_grp = jnp.concatenate(new, axis=0).astype(BF16)
            o_scr[i:i + 1, g * GROUP_W:(g + 1) * GROUP_W] = _dot_nt(c_rows, s_grp)[i:i + 1, :]
    _zero_fill(so_ref, fill)
    o = o_scr[...] + dsk_ref[...] * xs_ref[...]
    y_ref[...] = _rms(o * _silu(z_ref[...]), norm_ref[...])


def _ssd_dec(layer, state, carried, c_ssd, xs, xdt, drow, proj, dsk_exp, norm):
    tb = DEC_TOKENS
    rows = lambda col: pl.BlockSpec((tb, 1024), lambda t, j: (t, col))
    half = lambda col: pl.BlockSpec((tb, SSM_GROUPS * SSM_STATE), lambda t, j: (t, col))
    vec = pl.BlockSpec((1, 1024), lambda t, j: (0, 0))
    return _state_update_call(
        _ssd_dec_kernel, "ssd_dec", layer, state, carried, proj.shape[0], SSM_HEADS,
        [half(0), half(1), pl.BlockSpec((1, SSM_INNER, tb), lambda t, j: (t, 0, 0)),
         rows(0), rows(0), rows(7), vec, vec],
        (c_ssd, c_ssd, _cols(xdt, tb), drow, xs, proj, dsk_exp, norm), pltpu.VMEM((tb, 1024), F32))


def kernel(x_prompt, x_sample, state_ret, state_hgrn, state_ssm, state_conv, w_in, ret_norm, hg_norm,
           hg_lb_logits, conv_w, conv_b, dt_bias, a_log, d_skip, ssm_norm, w_branch, w_out, norm_mix_pre,
           norm_mix_post, norm_ffn_pre, norm_ffn_post, w_up, w_down):
    bp, lp, _ = x_prompt.shape
    bs, ls, _ = x_sample.shape
    assert ls == 1 and lp % CHUNK_ROWS == 0 and bs % DEC_TOKENS == 0
    depth = w_in.shape[0]

    lb_w = jax.nn.softmax(hg_lb_logits.astype(F32), axis=0)
    lb_all = jnp.cumsum(lb_w, axis=0) - lb_w[0]

    hp = x_prompt.reshape(bp * lp, D_MODEL)
    hs = x_sample.reshape(bs, D_MODEL)
    outs = [[] for _ in range(5)]
    new_ret = new_hg = new_ssm = None
    ssm_t = jnp.swapaxes(state_ssm, -1, -2)
    w_in_t = jnp.swapaxes(w_in, 1, 2)
    vec = lambda a: a.reshape(1, -1).astype(F32)
    hp_normed = hs_normed = None
    for l in range(depth):
        next_gain = vec(norm_mix_pre[l + 1]) if l + 1 < depth else None
        w_dt = jnp.pad(w_in_t[l, N_MAIN:N_MAIN + SSM_HEADS], ((0, DT_PAD - SSM_HEADS), (0, 0)))
        wb, wo = w_branch[l].astype(BF16), w_out[l].astype(BF16)
        dsk_exp = jnp.repeat(d_skip[l].astype(F32), SSM_HEADDIM).reshape(1, SSM_INNER)
        lb = vec(lb_all[l])

        proj, xn = _in_proj(hp if hp_normed is None else hp_normed,
                            vec(norm_mix_pre[l]) if hp_normed is None else None,
                            w_in_t, l, tm=min(2048, bp * lp), out_dtype=BF16)
        yr, s_ret = _ret_prompt(proj, vec(ret_norm[l]), bp, lp)
        yh, s_hg = _hgrn_prompt(proj, lb, vec(hg_norm[l]), bp, lp)
        ym, s_ssm, s_conv = _ssd_prompt(proj, xn, w_dt, conv_w[l], conv_b[l], dt_bias[l], a_log[l],
                                        dsk_exp, vec(ssm_norm[l]), bp, lp)
        hp, xn = _merge(hp, yr, yh, ym, proj, wb, wo, vec(norm_mix_post[l]), vec(norm_ffn_pre[l]), tm=512)
        hp, hp_normed = _ffn(hp, xn, vec(norm_ffn_post[l]), next_gain, w_up, w_down, l, tm=1024)
        for lst, val in zip(outs[:4], (s_ret, s_hg, s_ssm, s_conv)):
            lst.append(val)

        proj, xn = _in_proj(hs if hs_normed is None else hs_normed,
                            vec(norm_mix_pre[l]) if hs_normed is None else None,
                            w_in_t, l, tm=bs, out_dtype=F32)
        c_ret, c_hg, c_ssd, xs, xdt, drow, conv_new = _dec_prep(
            proj, xn, w_dt, state_conv[l].transpose(1, 0, 2), lb, conv_w[l], conv_b[l], dt_bias[l], a_log[l])
        yr, new_ret = _ret_dec(l, state_ret, new_ret, c_ret, proj, vec(ret_norm[l]))
        yh, new_hg = _hgrn_dec(l, state_hgrn, new_hg, c_hg, proj, vec(hg_norm[l]))
        ym, new_ssm = _ssd_dec(l, ssm_t, new_ssm, c_ssd, xs, xdt, drow, proj, dsk_exp, vec(ssm_norm[l]))
        hs, xn = _merge(hs, yr, yh, ym, proj, wb, wo, vec(norm_mix_post[l]), vec(norm_ffn_pre[l]), tm=bs)
        hs, hs_normed = _ffn(hs, xn, vec(norm_ffn_post[l]), next_gain, w_up, w_down, l, tm=bs)
        outs[4].append(conv_new.transpose(1, 0, 2))

    p_ret, p_hg, p_ssm, p_conv, s_conv = [jnp.stack(o) for o in outs]
    return (hp.reshape(bp, lp, D_MODEL), hs.reshape(bs, ls, D_MODEL), p_ret, p_hg, p_ssm, p_conv,
            new_ret, new_hg, jnp.swapaxes(new_ssm, -1, -2), s_conv)
```

```python
import functools

import numpy as np
import jax
import jax.numpy as jnp
from jax import lax
from jax.experimental import pallas as pl
from jax.experimental.pallas import tpu as pltpu

F32 = jnp.float32
BF16 = jnp.bfloat16

D_MODEL = 1024
PAST_LEN = 16384
RET_HEADS, RET_DK, RET_DV = 4, 128, 256
ROPE_BASE = 10000.0
HG_HEADS, HG_EXPAND, HG_DV = 8, 128, 128
SSM_HEADS, SSM_HEADDIM, SSM_GROUPS, SSM_STATE, SSM_CONV = 16, 64, 4, 128, 4
SSM_INNER = SSM_HEADS * SSM_HEADDIM
SSM_CONV_CH = SSM_INNER + 2 * SSM_GROUPS * SSM_STATE
N_BRANCH = 3
D_FF = 4 * D_MODEL
EPS = 1e-6
HG_CLIP_MIN = 1e-6

N_MAIN = 10240
N_PROJ = N_MAIN + 3 * D_MODEL
DT_PAD = 128

LANES = 128
SUBLANES = 8
VMEM_LIMIT = 56 * 1024 * 1024

CHUNK_ROWS = 256
DEC_TOKENS = 8


def _cparams(sem):
    return pltpu.CompilerParams(dimension_semantics=sem, vmem_limit_bytes=VMEM_LIMIT)


def _dot(a, b):
    return jnp.dot(a, b, preferred_element_type=F32)


def _dot_nt(a, b):
    return lax.dot_general(a, b, (((1,), (1,)), ((), ())), preferred_element_type=F32)


def _dot_tn(a, b):
    return lax.dot_general(a, b, (((0,), (0,)), ((), ())), preferred_element_type=F32)


def _split3(a):
    hi = a.astype(BF16)
    r1 = a - hi.astype(F32)
    mid = r1.astype(BF16)
    lo = (r1 - mid.astype(F32)).astype(BF16)
    return hi, mid, lo


def _dot3_left(m01, a):
    hi, mid, lo = _split3(a)
    return _dot(m01, hi) + _dot(m01, mid) + _dot(m01, lo)


def _dot3_right(a, m01):
    hi, mid, lo = _split3(a)
    return _dot(hi, m01) + _dot(mid, m01) + _dot(lo, m01)


def _sigmoid(x):
    return 1.0 / (1.0 + jnp.exp(-x))


def _silu(x):
    return x * _sigmoid(x)


def _softplus(x):
    return jnp.maximum(x, 0.0) + jnp.log1p(jnp.exp(-jnp.abs(x)))


def _rms(x, gain):
    return x * lax.rsqrt(jnp.mean(x * x, axis=-1, keepdims=True) + EPS) * gain


PROJ_TILE = 1024
MAIN_TILES = N_MAIN // PROJ_TILE
GATE_TILE0 = MAIN_TILES


def _project_tile(xn_ref, w_ref, o_ref):
    o_ref[...] = _dot_nt(xn_ref[...], w_ref[0].astype(BF16)).astype(o_ref.dtype)


def _in_proj_kernel(x_ref, g_ref, w_ref, o_ref, xn_ref):
    @pl.when(pl.program_id(1) == 0)
    def _():
        xn_ref[...] = _rms(x_ref[...], g_ref[...]).astype(BF16)

    _project_tile(xn_ref, w_ref, o_ref)


def _in_proj(x, gain, w_in_t, layer, tm, out_dtype):
    t = x.shape[0]
    tn = PROJ_TILE
    rows = pl.BlockSpec((tm, D_MODEL), lambda i, j: (i, 0))
    first_row = lambda j: (j * (tn // SUBLANES) + jnp.where(j >= MAIN_TILES, SSM_HEADS // SUBLANES, 0)) * SUBLANES
    w_spec = pl.BlockSpec((pl.Element(1), pl.Element(tn), pl.Element(D_MODEL)),
                          lambda i, j: (layer, first_row(j), 0))
    common = dict(grid=(t // tm, N_PROJ // tn), compiler_params=_cparams(("parallel", "arbitrary")),
                  name="in_proj")
    out_spec = pl.BlockSpec((tm, tn), lambda i, j: (i, j))
    out_shape = jax.ShapeDtypeStruct((t, N_PROJ), out_dtype)
    if gain is None:
        proj = pl.pallas_call(_project_tile, in_specs=[rows, w_spec], out_specs=out_spec,
                              out_shape=out_shape, **common)(x, w_in_t)
        return proj, x
    return pl.pallas_call(
        _in_proj_kernel, in_specs=[rows, pl.BlockSpec((1, D_MODEL), lambda i, j: (0, 0)), w_spec],
        out_specs=[out_spec, rows], out_shape=[out_shape, jax.ShapeDtypeStruct((t, D_MODEL), BF16)], **common,
    )(x, gain, w_in_t)


def _ret_log_gamma():
    return np.log1p(-np.exp2(-5.0 - np.arange(RET_HEADS, dtype=np.float64)))


def _ret_consts(r):
    lg = _ret_log_gamma()
    i = np.arange(r)
    diff = (i[:, None] - i[None, :]).astype(np.float64)
    dec = np.where(diff >= 0, np.exp(lg[:, None, None] * np.maximum(diff, 0.0)), 0.0)
    qd = np.exp(lg[:, None] * (i + 1.0))
    kd = np.exp(lg[:, None] * (r - 1.0 - i))
    gl = tuple(float(v) for v in np.exp(lg * r))
    bc = lambda a: jnp.asarray(np.broadcast_to(a[:, :, None], (RET_HEADS, r, LANES)), F32)
    return jnp.asarray(dec, F32), bc(qd), bc(kd), gl


def _rope_tables(pos):
    half = RET_DK // 2
    inv = ROPE_BASE ** (-(np.arange(half, dtype=np.float64) / half))
    ang = pos.astype(np.float64)[:, None] * inv[None, :]
    cos, sin = np.cos(ang), np.sin(ang)
    cos2 = np.concatenate([cos, cos], axis=1)
    sin2 = np.concatenate([-sin, sin], axis=1)
    scale = RET_DK ** -0.5
    return (jnp.asarray(cos2, F32), jnp.asarray(sin2, F32),
            jnp.asarray(cos2 * scale, F32), jnp.asarray(sin2 * scale, F32))


def _rope(t, cos2, sin2):
    return t * cos2 + pltpu.roll(t, RET_DK // 2, 1) * sin2


def _head_layernorm(o, gain):
    oc = o - jnp.mean(o, axis=-1, keepdims=True)
    return oc * lax.rsqrt(jnp.mean(oc * oc, axis=-1, keepdims=True) + EPS) * gain


def _ret_kernel(rq_ref, rk_ref, rv_ref, rg_ref, cq_ref, sq_ref, ck_ref, sk_ref, dec_ref, qd_ref, kd_ref,
                gain_ref, y_ref, sout_ref, s_scr, *, gl):
    c = pl.program_id(1)

    @pl.when(c == 0)
    def _():
        s_scr[...] = jnp.zeros_like(s_scr)

    cq, sq, ck, sk = cq_ref[...], sq_ref[...], ck_ref[...], sk_ref[...]
    for h in range(RET_HEADS):
        ks = slice(h * RET_DK, (h + 1) * RET_DK)
        vs = slice(h * RET_DV, (h + 1) * RET_DV)
        q = _rope(rq_ref[:, ks].astype(F32), cq, sq)
        k = _rope(rk_ref[:, ks].astype(F32), ck, sk)
        v = rv_ref[:, vs].astype(BF16)
        att = (_dot_nt(q.astype(BF16), k.astype(BF16)) * dec_ref[h]).astype(BF16)
        s_old = s_scr[h]
        o = _dot(att, v) + _dot((q * qd_ref[h]).astype(BF16), s_old.astype(BF16))
        s_scr[h] = gl[h] * s_old + _dot_tn((k * kd_ref[h]).astype(BF16), v)
        y = _silu(rg_ref[:, vs].astype(F32)) * _head_layernorm(o, gain_ref[:, vs])
        y_ref[:, vs] = y.astype(y_ref.dtype)

    @pl.when(c == pl.num_programs(1) - 1)
    def _():
        sout_ref[0] = s_scr[...]


def _ret_prompt(proj, gain, batch, seq):
    r = CHUNK_ROWS
    nc = seq // r
    dec, qd, kd, gl = _ret_consts(r)
    cq, sq, ck, sk = _rope_tables(np.arange(seq))
    row = lambda b, c: b * nc + c
    tab = pl.BlockSpec((r, RET_DK), lambda b, c: (c, 0))
    const3 = lambda shape: pl.BlockSpec(shape, lambda b, c: (0, 0, 0))
    return pl.pallas_call(
        functools.partial(_ret_kernel, gl=gl),
        grid=(batch, nc),
        in_specs=[
            pl.BlockSpec((r, 512), lambda b, c: (row(b, c), 0)),
            pl.BlockSpec((r, 512), lambda b, c: (row(b, c), 1)),
            pl.BlockSpec((r, 1024), lambda b, c: (row(b, c), 1)),
            pl.BlockSpec((r, 1024), lambda b, c: (row(b, c), 2)),
            tab, tab, tab, tab,
            const3((RET_HEADS, r, r)), const3((RET_HEADS, r, LANES)), const3((RET_HEADS, r, LANES)),
            pl.BlockSpec((1, 1024), lambda b, c: (0, 0)),
        ],
        out_specs=[
            pl.BlockSpec((r, 1024), lambda b, c: (row(b, c), 0)),
            pl.BlockSpec((1, RET_HEADS, RET_DK, RET_DV), lambda b, c: (b, 0, 0, 0)),
        ],
        out_shape=[jax.ShapeDtypeStruct((batch * seq, 1024), BF16),
                   jax.ShapeDtypeStruct((batch, RET_HEADS, RET_DK, RET_DV), F32)],
        scratch_shapes=[pltpu.VMEM((RET_HEADS, RET_DK, RET_DV), F32)],
        compiler_params=_cparams(("parallel", "arbitrary")),
        name="ret_prompt",
    )(proj, proj, proj, proj, cq, sq, ck, sk, dec, qd, kd, gain)


HG_CHUNK = 128
HG_LEVELS = (64, 32, 16, 8)
HG_TILE_REF = SUBLANES // 2 - 1
LOG2E = 1.4426950408889634


def _hg_masks():
    i = np.arange(HG_CHUNK)
    out = [(i[:, None] // (2 * b)) == (i[None, :] // (2 * b)) for b in HG_LEVELS[1:]]
    out.append(((i[:, None] // SUBLANES) == (i[None, :] // SUBLANES)) & (i[:, None] >= i[None, :]))
    return jnp.asarray(np.stack(out).astype(np.float32))


def _tri_ones(r):
    i = np.arange(r)
    return jnp.asarray((i[:, None] >= i[None, :]).astype(np.float32), BF16)


def _halves(a, nb):
    a5 = a.reshape(HG_CHUNK // (2 * nb * SUBLANES), 2, nb, SUBLANES, a.shape[-1])
    return a5[:, 0], a5[:, 1]


def _join_halves(lower, upper):
    return jnp.stack([lower, upper], axis=1).reshape(HG_CHUNK, lower.shape[-1])


def _tile_row_bcast(a, j):
    a3 = a.reshape(HG_CHUNK // SUBLANES, SUBLANES, a.shape[-1])
    return jnp.broadcast_to(a3[:, j:j + 1, :], a3.shape).reshape(a.shape)


def _per_head_nt(a, b):
    return [_dot_nt(a[:, h * LANES:(h + 1) * LANES], b[:, h * LANES:(h + 1) * LANES]) for h in range(HG_HEADS)]


def _hg_chunk_att(q, kk, g2, mask_ref):
    tile_last = _tile_row_bcast(g2, SUBLANES - 1)
    att = None
    for lvl, b in enumerate(HG_LEVELS):
        nb = b // SUBLANES
        _, q_up = _halves(q, nb)
        k_lo, _ = _halves(kk, nb)
        g_lo, g_up = _halves(g2, nb)
        ref = _halves(tile_last, nb)[0][:, nb - 1:nb]
        qe = q_up * jnp.exp2(g_up - ref)
        ke = k_lo * jnp.exp2(ref - g_lo)
        zeros = jnp.zeros_like(qe)
        prods = _per_head_nt(_join_halves(zeros, qe).astype(BF16), _join_halves(ke, zeros).astype(BF16))
        att = prods if lvl == 0 else [a + p * mask_ref[lvl - 1] for a, p in zip(att, prods)]
    d = g2 - _tile_row_bcast(g2, HG_TILE_REF)
    prods = _per_head_nt((q * jnp.exp2(d)).astype(BF16), (kk * jnp.exp2(-d)).astype(BF16))
    same_tile = mask_ref[len(HG_LEVELS) - 1] > 0.5
    return [a + jnp.where(same_tile, p, 0.0) for a, p in zip(att, prods)]


def _hgrn_kernel(hq_ref, hf_ref, hi_ref, hg_ref, lb_ref, gain_ref, tri_ref, mask_ref, y_ref, sout_ref,
                 st_scr, *, r):
    c = pl.program_id(1)

    @pl.when(c == 0)
    def _():
        st_scr[...] = jnp.zeros_like(st_scr)

    lb = lb_ref[...]
    for ck in range(r // HG_CHUNK):
        rows = slice(ck * HG_CHUNK, (ck + 1) * HG_CHUNK)
        q = _silu(hq_ref[rows, :].astype(F32))
        f = jnp.clip(lb + (1.0 - lb) * _sigmoid(hf_ref[rows, :].astype(F32)), HG_CLIP_MIN, 1.0)
        kk = 1.0 - f
        g2 = _dot3_left(tri_ref[...], jnp.log(f)) * LOG2E
        g2_last = g2[HG_CHUNK - 1:HG_CHUNK, :]
        att = _hg_chunk_att(q, kk, g2, mask_ref)
        q_in = (q * jnp.exp2(g2)).astype(BF16)
        k_out = (kk * jnp.exp2(g2_last - g2)).astype(BF16)
        e_last = jnp.exp2(g2_last)
        gate = _sigmoid(hg_ref[rows, :].astype(F32))
        for h in range(HG_HEADS):
            sl = slice(h * LANES, (h + 1) * LANES)
            v = hi_ref[rows, sl]
            st_old = st_scr[h]
            o = _dot(att[h].astype(BF16), v) + _dot_nt(q_in[:, sl], st_old.astype(BF16))
            st_scr[h] = st_old * e_last[:, sl] + _dot_tn(v, k_out[:, sl])
            o = o * lax.rsqrt(jnp.mean(o * o, axis=-1, keepdims=True) + EPS) * gain_ref[:, sl]
            y_ref[rows, sl] = (o * gate[:, sl]).astype(y_ref.dtype)

    @pl.when(c == pl.num_programs(1) - 1)
    def _():
        for h in range(HG_HEADS):
            sout_ref[0, h] = st_scr[h].T


def _hgrn_prompt(proj, lb, gain, batch, seq):
    r = CHUNK_ROWS
    nc = seq // r
    masks = _hg_masks()
    row = lambda b, c: b * nc + c
    blk = lambda j: pl.BlockSpec((r, 1024), lambda b, c: (row(b, c), j))
    vec = pl.BlockSpec((1, 1024), lambda b, c: (0, 0))
    return pl.pallas_call(
        functools.partial(_hgrn_kernel, r=r),
        grid=(batch, nc),
        in_specs=[blk(3), blk(4), blk(5), blk(6), vec, vec,
                  pl.BlockSpec((HG_CHUNK, HG_CHUNK), lambda b, c: (0, 0)),
                  pl.BlockSpec(masks.shape, lambda b, c: (0, 0, 0))],
        out_specs=[
            pl.BlockSpec((r, 1024), lambda b, c: (row(b, c), 0)),
            pl.BlockSpec((1, HG_HEADS, HG_EXPAND, HG_DV), lambda b, c: (b, 0, 0, 0)),
        ],
        out_shape=[jax.ShapeDtypeStruct((batch * seq, 1024), BF16),
                   jax.ShapeDtypeStruct((batch, HG_HEADS, HG_EXPAND, HG_DV), F32)],
        scratch_shapes=[pltpu.VMEM((HG_HEADS, HG_DV, HG_EXPAND), F32)],
        compiler_params=_cparams(("parallel", "arbitrary")),
        name="hgrn_prompt",
    )(proj, proj, proj, proj, lb, gain, _tri_ones(HG_CHUNK), masks)


HEADS_PER_GROUP = SSM_HEADS // SSM_GROUPS
GROUP_W = HEADS_PER_GROUP * SSM_HEADDIM
CONV_PAD = SUBLANES
SSD_CHUNK = 128


def _head_expand(vals, h0, lane_head):
    out = jnp.zeros((vals.shape[0], GROUP_W), F32)
    for j in range(HEADS_PER_GROUP):
        out = jnp.where(lane_head == j, vals[:, h0 + j:h0 + j + 1], out)
    return out


def _ssd_kernel(z_ref, xbc_ref, xn_ref, wdt_ref, cw_ref, cb_ref, dtb_ref, alog_ref, dsk_ref, norm_ref, tri_ref,
                shift_ref, y_ref, sout_ref, cout_ref, hist_scr, s_scr, o_scr, *, r):
    c = pl.program_id(1)
    last = pl.num_programs(1) - 1

    @pl.when(c == 0)
    def _():
        hist_scr[0:CONV_PAD, :] = jnp.zeros((CONV_PAD, SSM_CONV_CH), F32)
        s_scr[...] = jnp.zeros_like(s_scr)

    u_bf = xbc_ref[...]
    u = u_bf.astype(F32)
    hist_scr[CONV_PAD:2 * CONV_PAD, :] = u[0:CONV_PAD]
    conv = cb_ref[...] + cw_ref[SSM_CONV - 1:SSM_CONV, :] * u
    head = cb_ref[...]
    for w in range(SSM_CONV):
        off = CONV_PAD - (SSM_CONV - 1) + w
        head = head + cw_ref[w:w + 1, :] * hist_scr[off:off + CONV_PAD, :]
    for back in range(1, SSM_CONV):
        conv = conv + cw_ref[SSM_CONV - 1 - back:SSM_CONV - back, :] * _dot(shift_ref[back - 1], u_bf)
    conv = jnp.concatenate([head, conv[CONV_PAD:]], axis=0)

    @pl.when(c == last)
    def _():
        cout_ref[0] = u[r - (SSM_CONV - 1):r, :]

    hist_scr[0:CONV_PAD, :] = u[r - CONV_PAD:r]

    xc = _silu(conv)
    xs = xc[:, :SSM_INNER]
    bm = xc[:, SSM_INNER:SSM_INNER + SSM_GROUPS * SSM_STATE]
    cm = xc[:, SSM_INNER + SSM_GROUPS * SSM_STATE:]

    dt_all = _softplus(_dot_nt(xn_ref[...], wdt_ref[...].astype(BF16)) + dtb_ref[...])
    rate = -jnp.exp(alog_ref[...]) * LOG2E
    rc = SSD_CHUNK
    ri = lax.broadcasted_iota(jnp.int32, (rc, rc), 0)
    ci = lax.broadcasted_iota(jnp.int32, (rc, rc), 1)
    causal = ri >= ci
    lane_head = lax.broadcasted_iota(jnp.int32, (rc, GROUP_W), 1) // SSM_HEADDIM
    lane_head1 = lax.broadcasted_iota(jnp.int32, (1, GROUP_W), 1) // SSM_HEADDIM
    groups = range(SSM_GROUPS)

    for ck in range(r // rc):
        rows = slice(ck * rc, (ck + 1) * rc)
        dt = dt_all[rows]
        g2 = _dot3_left(tri_ref[...], dt * rate)
        g2_last = g2[rc - 1:rc, :]
        g2_t = g2.T
        dt_t = dt.T
        e_cum = jnp.exp2(g2)
        w_in = dt * jnp.exp2(g2_last - g2)
        e_last = jnp.exp2(g2_last)

        cg = [cm[rows, g * SSM_STATE:(g + 1) * SSM_STATE].astype(BF16) for g in groups]
        bg = [bm[rows, g * SSM_STATE:(g + 1) * SSM_STATE].astype(BF16) for g in groups]
        xs_g = [xs[rows, g * GROUP_W:(g + 1) * GROUP_W] for g in groups]
        xs_b = [x.astype(BF16) for x in xs_g]
        s_old = [s_scr[g] for g in groups]
        cb = [jnp.where(causal, _dot_nt(cg[g], bg[g]), 0.0) for g in groups]
        carried = [_dot(cg[g], s_old[g].astype(BF16)) for g in groups]
        for g in groups:
            h0 = g * HEADS_PER_GROUP
            o_g = _head_expand(e_cum, h0, lane_head) * carried[g]
            for j in range(HEADS_PER_GROUP):
                h = h0 + j
                dec = jnp.exp2(jnp.minimum(g2[:, h:h + 1] - g2_t[h:h + 1, :], 0.0))
                att = (cb[g] * dec * dt_t[h:h + 1, :]).astype(BF16)
                o_g = o_g + jnp.where(lane_head == j, _dot(att, xs_b[g]), 0.0)
            s_scr[g] = (s_old[g] * _head_expand(e_last, h0, lane_head1)
                        + _dot_tn(bg[g], (xs_g[g] * _head_expand(w_in, h0, lane_head)).astype(BF16)))
            o_scr[rows, g * GROUP_W:(g + 1) * GROUP_W] = o_g + dsk_ref[:, g * GROUP_W:(g + 1) * GROUP_W] * xs_g[g]

    y_ref[...] = _rms(o_scr[...] * _silu(z_ref[...].astype(F32)), norm_ref[...]).astype(y_ref.dtype)

    @pl.when(c == last)
    def _():
        for h in range(SSM_HEADS):
            j = h % HEADS_PER_GROUP
            sout_ref[0, h] = s_scr[h // HEADS_PER_GROUP][:, j * SSM_HEADDIM:(j + 1) * SSM_HEADDIM]


def _shift_ones(r):
    i = np.arange(r)
    mats = [(i[:, None] - i[None, :]) == k for k in range(1, SSM_CONV)]
    return jnp.asarray(np.stack(mats).astype(np.float32), BF16)


def _pad_lanes(v):
    return jnp.pad(v.astype(F32), (0, LANES - v.shape[0])).reshape(1, LANES)


def _ssd_prompt(proj, xn, w_dt, conv_w, conv_b, dt_bias, a_log, dsk_exp, norm, batch, seq):
    r = CHUNK_ROWS
    nc = seq // r
    row = lambda b, c: b * nc + c
    vec = lambda n: pl.BlockSpec((1, n), lambda b, c: (0, 0))
    return pl.pallas_call(
        functools.partial(_ssd_kernel, r=r),
        grid=(batch, nc),
        in_specs=[
            pl.BlockSpec((r, 1024), lambda b, c: (row(b, c), 7)),
            pl.BlockSpec((r, SSM_CONV_CH), lambda b, c: (row(b, c), 4)),
            pl.BlockSpec((r, D_MODEL), lambda b, c: (row(b, c), 0)),
            pl.BlockSpec((DT_PAD, D_MODEL), lambda b, c: (0, 0)),
            pl.BlockSpec((SSM_CONV, SSM_CONV_CH), lambda b, c: (0, 0)),
            vec(SSM_CONV_CH), vec(LANES), vec(LANES), vec(SSM_INNER), vec(SSM_INNER),
            pl.BlockSpec((SSD_CHUNK, SSD_CHUNK), lambda b, c: (0, 0)),
            pl.BlockSpec((SSM_CONV - 1, r, r), lambda b, c: (0, 0, 0)),
        ],
        out_specs=[
            pl.BlockSpec((r, 1024), lambda b, c: (row(b, c), 0)),
            pl.BlockSpec((1, SSM_HEADS, SSM_STATE, SSM_HEADDIM), lambda b, c: (b, 0, 0, 0)),
            pl.BlockSpec((1, SSM_CONV - 1, SSM_CONV_CH), lambda b, c: (b, 0, 0)),
        ],
        out_shape=[jax.ShapeDtypeStruct((batch * seq, 1024), BF16),
                   jax.ShapeDtypeStruct((batch, SSM_HEADS, SSM_STATE, SSM_HEADDIM), F32),
                   jax.ShapeDtypeStruct((batch, SSM_CONV - 1, SSM_CONV_CH), F32)],
        scratch_shapes=[pltpu.VMEM((2 * CONV_PAD, SSM_CONV_CH), F32),
                        pltpu.VMEM((SSM_GROUPS, SSM_STATE, GROUP_W), F32),
                        pltpu.VMEM((r, SSM_INNER), F32)],
        compiler_params=_cparams(("parallel", "arbitrary")),
        name="ssd_prompt",
    )(proj, proj, xn, w_dt, conv_w, conv_b.reshape(1, -1), _pad_lanes(dt_bias), _pad_lanes(a_log), dsk_exp, norm,
      _tri_ones(SSD_CHUNK), _shift_ones(r))


def _merge_kernel(x_ref, yr_ref, yh_ref, ym_ref, g0_ref, g1_ref, g2_ref, wb_ref, wo_ref, npost_ref, nnext_ref,
                  o_ref, xn_ref):
    merged = _sigmoid(g0_ref[...].astype(F32)) * _dot(yr_ref[...].astype(BF16), wb_ref[0])
    merged = merged + _sigmoid(g1_ref[...].astype(F32)) * _dot(yh_ref[...].astype(BF16), wb_ref[1])
    merged = merged + _sigmoid(g2_ref[...].astype(F32)) * _dot(ym_ref[...].astype(BF16), wb_ref[2])
    m = _dot(merged.astype(BF16), wo_ref[...])
    out = x_ref[...] + _rms(m, npost_ref[...])
    o_ref[...] = out
    xn_ref[...] = _rms(out, nnext_ref[...]).astype(BF16)


def _merge(x, yr, yh, ym, proj, wb, wo, npost, nnext, tm):
    t = x.shape[0]
    rows = lambda j: pl.BlockSpec((tm, D_MODEL), lambda i: (i, j))
    vec = pl.BlockSpec((1, D_MODEL), lambda i: (0, 0))
    return pl.pallas_call(
        _merge_kernel,
        grid=(t // tm,),
        in_specs=[rows(0), rows(0), rows(0), rows(0), rows(GATE_TILE0), rows(GATE_TILE0 + 1), rows(GATE_TILE0 + 2),
                  pl.BlockSpec((3, D_MODEL, D_MODEL), lambda i: (0, 0, 0)),
                  pl.BlockSpec((D_MODEL, D_MODEL), lambda i: (0, 0)), vec, vec],
        out_specs=[rows(0), rows(0)],
        out_shape=[jax.ShapeDtypeStruct((t, D_MODEL), F32), jax.ShapeDtypeStruct((t, D_MODEL), BF16)],
        compiler_params=_cparams(("parallel",)),
        name="merge",
    )(x, yr, yh, ym, proj, proj, proj, wb, wo, npost, nnext)


def _ffn_kernel(x_ref, xn_ref, wu_ref, wd_ref, npost_ref, *rest):
    (nnext_ref, o_ref, xnn_ref, acc_scr) = rest if len(rest) == 4 else (None, rest[0], None, rest[1])
    j = pl.program_id(1)

    @pl.when(j == 0)
    def _():
        acc_scr[...] = jnp.zeros_like(acc_scr)

    hid = jnp.square(jnp.maximum(_dot(xn_ref[...], wu_ref[0].astype(BF16)), 0.0))
    acc_scr[...] += _dot(hid.astype(BF16), wd_ref[0].astype(BF16))

    @pl.when(j == pl.num_programs(1) - 1)
    def _():
        out = x_ref[...] + _rms(acc_scr[...], npost_ref[...])
        o_ref[...] = out
        if xnn_ref is not None:
            xnn_ref[...] = _rms(out, nnext_ref[...]).astype(BF16)


def _ffn(x, xn, npost, nnext, w_up, w_down, layer, tm):
    t = x.shape[0]
    tf = 1024
    vec = pl.BlockSpec((1, D_MODEL), lambda i, j: (0, 0))
    rows = pl.BlockSpec((tm, D_MODEL), lambda i, j: (i, 0))
    chain = nnext is not None
    wu, wd = w_up, w_down
    outs = pl.pallas_call(
        _ffn_kernel,
        grid=(t // tm, D_FF // tf),
        in_specs=[rows, rows,
                  pl.BlockSpec((1, D_MODEL, tf), lambda i, j: (layer, 0, j)),
                  pl.BlockSpec((1, tf, D_MODEL), lambda i, j: (layer, j, 0)), vec] + ([vec] if chain else []),
        out_specs=[rows, rows] if chain else [rows],
        out_shape=[jax.ShapeDtypeStruct((t, D_MODEL), F32)]
        + ([jax.ShapeDtypeStruct((t, D_MODEL), BF16)] if chain else []),
        scratch_shapes=[pltpu.VMEM((tm, D_MODEL), F32)],
        compiler_params=_cparams(("parallel", "arbitrary")),
        name="ffn",
    )(x, xn, wu, wd, npost, *([nnext] if chain else []))
    return (outs[0], outs[1]) if chain else (outs[0], None)


def _dec_prep_kernel(proj_ref, xn_ref, wdt_ref, cst_ref, cq_ref, sq_ref, ck_ref, sk_ref, lb_ref, cw_ref, cb_ref,
                     dtb_ref, alog_ref, hexp_ref,
                     cret_ref, chg_ref, cssd_ref, xs_ref, xdt_ref, drow_ref, cnew_ref):
    n = proj_ref.shape[0]
    cq, sq, ck, sk = cq_ref[...], sq_ref[...], ck_ref[...], sk_ref[...]
    for h in range(RET_HEADS):
        ks = slice(h * RET_DK, (h + 1) * RET_DK)
        cret_ref[:, ks] = _rope(proj_ref[:, ks], cq, sq)
        k_off = RET_HEADS * RET_DK
        cret_ref[:, k_off + h * RET_DK:k_off + (h + 1) * RET_DK] = _rope(
            proj_ref[:, k_off + h * RET_DK:k_off + (h + 1) * RET_DK], ck, sk)

    lb = lb_ref[...]
    chg_ref[:, 0:1024] = _silu(proj_ref[:, 3072:4096])
    chg_ref[:, 1024:2048] = jnp.clip(lb + (1.0 - lb) * _sigmoid(proj_ref[:, 4096:5120]), HG_CLIP_MIN, 1.0)

    u = proj_ref[:, 8192:8192 + SSM_CONV_CH]
    conv = cb_ref[...] + cw_ref[SSM_CONV - 1:SSM_CONV, :] * u
    for w in range(SSM_CONV - 1):
        conv = conv + cw_ref[w:w + 1, :] * cst_ref[w]
    cnew_ref[0] = cst_ref[1]
    cnew_ref[1] = cst_ref[2]
    cnew_ref[2] = u
    xc = _silu(conv)
    xs = xc[:, :SSM_INNER]
    xs_ref[...] = xs
    cssd_ref[:, 0:512] = xc[:, SSM_INNER + 512:]
    cssd_ref[:, 512:1024] = xc[:, SSM_INNER:SSM_INNER + 512]
    dt = _softplus(_dot_nt(xn_ref[...], wdt_ref[...].astype(BF16)) + dtb_ref[...])
    decay = jnp.exp(dt * (-jnp.exp(alog_ref[...])))
    hexp = hexp_ref[...]
    xdt_ref[...] = xs * _dot3_right(dt, hexp)
    drow_ref[...] = _dot3_right(decay, hexp)
    del n


def _dec_prep(proj, xn, w_dt, conv_state_t, lb, conv_w, conv_b, dt_bias, a_log):
    n = proj.shape[0]
    cq, sq, ck, sk = _rope_tables(np.full((n,), PAST_LEN))
    hexp = np.zeros((LANES, SSM_INNER), np.float32)
    for h in range(SSM_HEADS):
        hexp[h, h * SSM_HEADDIM:(h + 1) * SSM_HEADDIM] = 1.0
    outs = [(n, 1024), (n, 2048), (n, 1024), (n, 1024), (n, 1024), (n, 1024)]
    return pl.pallas_call(
        _dec_prep_kernel,
        out_shape=[jax.ShapeDtypeStruct(s, F32) for s in outs]
        + [jax.ShapeDtypeStruct((SSM_CONV - 1, n, SSM_CONV_CH), F32)],
        compiler_params=pltpu.CompilerParams(vmem_limit_bytes=VMEM_LIMIT),
        name="dec_prep",
    )(proj, xn, w_dt, conv_state_t, cq, sq, ck, sk, lb, conv_w, conv_b.reshape(1, -1), _pad_lanes(dt_bias),
      _pad_lanes(a_log), jnp.asarray(hexp, BF16))


def _cols(a, tb):
    n, ch = a.shape
    return a.reshape(n // tb, tb, ch).transpose(0, 2, 1)


def _state_update_call(body, name, layer, state, carried, n_tokens, heads_per_step, in_specs, operands,
                       scratch):
    depth, _, heads, sk, sv = state.shape
    tb = DEC_TOKENS
    in_spec = pl.BlockSpec((1, tb, heads_per_step, sk, sv), lambda t, j: (layer, t, j, 0, 0))
    if carried is None:
        carried, aliases = state, {}
        out_spec = pl.BlockSpec((depth, tb, heads_per_step, sk, sv), lambda t, j: (0, t, j, 0, 0))
        own, fill = layer, tuple(d for d in range(depth) if d != layer)
    else:
        aliases = {1: 1}
        out_spec, own, fill = in_spec, 0, ()
    return pl.pallas_call(
        functools.partial(body, own=own, fill=fill),
        grid=(n_tokens // tb, heads // heads_per_step),
        in_specs=[in_spec, pl.BlockSpec(memory_space=pl.ANY)] + in_specs,
        out_specs=[pl.BlockSpec((tb, 1024), lambda t, j: (t, 0)), out_spec],
        out_shape=[jax.ShapeDtypeStruct((n_tokens, 1024), F32), jax.ShapeDtypeStruct(state.shape, F32)],
        input_output_aliases=aliases,
        scratch_shapes=[scratch],
        compiler_params=_cparams(("parallel", "arbitrary")),
        name=name,
    )(state, carried, *operands)


def _zero_fill(so_ref, fill):
    for d in fill:
        so_ref[d] = jnp.zeros(so_ref.shape[1:], so_ref.dtype)


def _ret_dec_kernel(s_ref, carried_ref, ct_ref, v_ref, rg_ref, gain_ref, y_ref, so_ref, o_scr, *, gamma, own,
                    fill):
    del carried_ref
    for i in range(DEC_TOKENS):
        for h in range(RET_HEADS):
            qcol = ct_ref[0, h * RET_DK:(h + 1) * RET_DK, i:i + 1]
            kcol = ct_ref[0, (RET_HEADS + h) * RET_DK:(RET_HEADS + h + 1) * RET_DK, i:i + 1]
            vs = slice(h * RET_DV, (h + 1) * RET_DV)
            s_new = gamma[h] * s_ref[0, i, h] + kcol * v_ref[i:i + 1, vs]
            so_ref[own, i, h] = s_new
            o_scr[i:i + 1, vs] = jnp.sum(qcol * s_new, axis=0, keepdims=True)
    _zero_fill(so_ref, fill)
    for h in range(RET_HEADS):
        vs = slice(h * RET_DV, (h + 1) * RET_DV)
        y_ref[:, vs] = _silu(rg_ref[:, vs]) * _head_layernorm(o_scr[:, vs], gain_ref[:, vs])


def _ret_dec(layer, state, carried, c_ret, proj, gain):
    tb = DEC_TOKENS
    gamma = tuple(float(v) for v in np.exp(_ret_log_gamma()))
    return _state_update_call(
        functools.partial(_ret_dec_kernel, gamma=gamma), "ret_dec", layer, state, carried, proj.shape[0],
        RET_HEADS,
        [pl.BlockSpec((1, 2 * RET_HEADS * RET_DK, tb), lambda t, j: (t, 0, 0)),
         pl.BlockSpec((tb, 1024), lambda t, j: (t, 1)), pl.BlockSpec((tb, 1024), lambda t, j: (t, 2)),
         pl.BlockSpec((1, 1024), lambda t, j: (0, 0))],
        (_cols(c_ret, tb), proj, proj, gain), pltpu.VMEM((tb, 1024), F32))


def _hgrn_dec_kernel(s_ref, carried_ref, ct_ref, v_ref, hg_ref, gain_ref, y_ref, so_ref, o_scr, *, own, fill):
    del carried_ref
    for i in range(DEC_TOKENS):
        for h in range(HG_HEADS):
            sl = slice(h * LANES, (h + 1) * LANES)
            qcol = ct_ref[0, h * HG_EXPAND:(h + 1) * HG_EXPAND, i:i + 1]
            fcol = ct_ref[0, (HG_HEADS + h) * HG_EXPAND:(HG_HEADS + h + 1) * HG_EXPAND, i:i + 1]
            s_new = fcol * s_ref[0, i, h] + (1.0 - fcol) * v_ref[i:i + 1, sl]
            so_ref[own, i, h] = s_new
            o_scr[i:i + 1, sl] = jnp.sum(qcol * s_new, axis=0, keepdims=True)
    _zero_fill(so_ref, fill)
    for h in range(HG_HEADS):
        sl = slice(h * LANES, (h + 1) * LANES)
        o = o_scr[:, sl]
        o = o * lax.rsqrt(jnp.mean(o * o, axis=-1, keepdims=True) + EPS) * gain_ref[:, sl]
        y_ref[:, sl] = o * _sigmoid(hg_ref[:, sl])


def _hgrn_dec(layer, state, carried, c_hg, proj, gain):
    tb = DEC_TOKENS
    return _state_update_call(
        _hgrn_dec_kernel, "hgrn_dec", layer, state, carried, proj.shape[0], HG_HEADS,
        [pl.BlockSpec((1, 2 * HG_HEADS * HG_EXPAND, tb), lambda t, j: (t, 0, 0)),
         pl.BlockSpec((tb, 1024), lambda t, j: (t, 5)), pl.BlockSpec((tb, 1024), lambda t, j: (t, 6)),
         pl.BlockSpec((1, 1024), lambda t, j: (0, 0))],
        (_cols(c_hg, tb), proj, proj, gain), pltpu.VMEM((tb, 1024), F32))


def _ssd_dec_kernel(s_ref, carried_ref, c_ref, b_ref, xdt_ref, drow_ref, xs_ref, z_ref, dsk_ref, norm_ref,
                    y_ref, so_ref, o_scr, *, own, fill):
    del carried_ref
    for g in range(SSM_GROUPS):
        c_rows = c_ref[:, g * SSM_STATE:(g + 1) * SSM_STATE].astype(BF16)
        for i in range(DEC_TOKENS):
            b_row = b_ref[i:i + 1, g * SSM_STATE:(g + 1) * SSM_STATE]
            new = []
            for h in range(g * HEADS_PER_GROUP, (g + 1) * HEADS_PER_GROUP):
                hs = slice(h * SSM_HEADDIM, (h + 1) * SSM_HEADDIM)
                decay = drow_ref[i:i + 1, h * SSM_HEADDIM:h * SSM_HEADDIM + 1]
                s_new = decay * s_ref[0, i, h] + xdt_ref[0, hs, i:i + 1] * b_row
                so_ref[own, i, h] = s_new
                new.append(s_new)
            s_grp = jnp.concatenate(new, axis=0).astype(BF16)
            o_scr[i:i + 1, g * GROUP_W:(g + 1) * GROUP_W] = _dot_nt(c_rows, s_grp)[i:i + 1, :]
    _zero_fill(so_ref, fill)
    o = o_scr[...] + dsk_ref[...] * xs_ref[...]
    y_ref[...] = _rms(o * _silu(z_ref[...]), norm_ref[...])


def _ssd_dec(layer, state, carried, c_ssd, xs, xdt, drow, proj, dsk_exp, norm):
    tb = DEC_TOKENS
    rows = lambda col: pl.BlockSpec((tb, 1024), lambda t, j: (t, col))
    half = lambda col: pl.BlockSpec((tb, SSM_GROUPS * SSM_STATE), lambda t, j: (t, col))
    vec = pl.BlockSpec((1, 1024), lambda t, j: (0, 0))
    return _state_update_call(
        _ssd_dec_kernel, "ssd_dec", layer, state, carried, proj.shape[0], SSM_HEADS,
        [half(0), half(1), pl.BlockSpec((1, SSM_INNER, tb), lambda t, j: (t, 0, 0)),
         rows(0), rows(0), rows(7), vec, vec],
        (c_ssd, c_ssd, _cols(xdt, tb), drow, xs, proj, dsk_exp, norm), pltpu.VMEM((tb, 1024), F32))


def kernel(x_prompt, x_sample, state_ret, state_hgrn, state_ssm, state_conv, w_in, ret_norm, hg_norm,
           hg_lb_logits, conv_w, conv_b, dt_bias, a_log, d_skip, ssm_norm, w_branch, w_out, norm_mix_pre,
           norm_mix_post, norm_ffn_pre, norm_ffn_post, w_up, w_down):
    bp, lp, _ = x_prompt.shape
    bs, ls, _ = x_sample.shape
    assert ls == 1 and lp % CHUNK_ROWS == 0 and bs % DEC_TOKENS == 0
    depth = w_in.shape[0]

    lb_w = jax.nn.softmax(hg_lb_logits.astype(F32), axis=0)
    lb_all = jnp.cumsum(lb_w, axis=0) - lb_w[0]

    hp = x_prompt.reshape(bp * lp, D_MODEL)
    hs = x_sample.reshape(bs, D_MODEL)
    outs = [[] for _ in range(5)]
    new_ret = new_hg = new_ssm = None
    ssm_t = jnp.swapaxes(state_ssm, -1, -2)
    w_in_t = jnp.swapaxes(w_in, 1, 2)
    vec = lambda a: a.reshape(1, -1).astype(F32)
    hp_normed = hs_normed = None
    for l in range(depth):
        next_gain = vec(norm_mix_pre[l + 1]) if l + 1 < depth else None
        w_dt = jnp.pad(w_in_t[l, N_MAIN:N_MAIN + SSM_HEADS], ((0, DT_PAD - SSM_HEADS), (0, 0)))
        wb, wo = w_branch[l].astype(BF16), w_out[l].astype(BF16)
        dsk_exp = jnp.repeat(d_skip[l].astype(F32), SSM_HEADDIM).reshape(1, SSM_INNER)
        lb = vec(lb_all[l])

        proj, xn = _in_proj(hp if hp_normed is None else hp_normed,
                            vec(norm_mix_pre[l]) if hp_normed is None else None,
                            w_in_t, l, tm=min(2048 if hp_normed is None else 4096, bp * lp), out_dtype=BF16)
        yr, s_ret = _ret_prompt(proj, vec(ret_norm[l]), bp, lp)
        yh, s_hg = _hgrn_prompt(proj, lb, vec(hg_norm[l]), bp, lp)
        ym, s_ssm, s_conv = _ssd_prompt(proj, xn, w_dt, conv_w[l], conv_b[l], dt_bias[l], a_log[l],
                                        dsk_exp, vec(ssm_norm[l]), bp, lp)
        hp, xn = _merge(hp, yr, yh, ym, proj, wb, wo, vec(norm_mix_post[l]), vec(norm_ffn_pre[l]), tm=512)
        hp, hp_normed = _ffn(hp, xn, vec(norm_ffn_post[l]), next_gain, w_up, w_down, l, tm=1024)
        for lst, val in zip(outs[:4], (s_ret, s_hg, s_ssm, s_conv)):
            lst.append(val)

        proj, xn = _in_proj(hs if hs_normed is None else hs_normed,
                            vec(norm_mix_pre[l]) if hs_normed is None else None,
                            w_in_t, l, tm=bs, out_dtype=F32)
        c_ret, c_hg, c_ssd, xs, xdt, drow, conv_new = _dec_prep(
            proj, xn, w_dt, state_conv[l].transpose(1, 0, 2), lb, conv_w[l], conv_b[l], dt_bias[l], a_log[l])
        yr, new_ret = _ret_dec(l, state_ret, new_ret, c_ret, proj, vec(ret_norm[l]))
        yh, new_hg = _hgrn_dec(l, state_hgrn, new_hg, c_hg, proj, vec(hg_norm[l]))
        ym, new_ssm = _ssd_dec(l, ssm_t, new_ssm, c_ssd, xs, xdt, drow, proj, dsk_exp, vec(ssm_norm[l]))
        hs, xn = _merge(hs, yr, yh, ym, proj, wb, wo, vec(norm_mix_post[l]), vec(norm_ffn_pre[l]), tm=bs)
        hs, hs_normed = _ffn(hs, xn, vec(norm_ffn_post[l]), next_gain, w_up, w_down, l, tm=bs)
        outs[4].append(conv_new.transpose(1, 0, 2))

    p_ret, p_hg, p_ssm, p_conv, s_conv = [jnp.stack(o) for o in outs]
    return (hp.reshape(bp, lp, D_MODEL), hs.reshape(bs, ls, D_MODEL), p_ret, p_hg, p_ssm, p_conv,
            new_ret, new_hg, jnp.swapaxes(new_ssm, -1, -2), s_conv)
```

```python
import functools

import numpy as np
import jax
import jax.numpy as jnp
from jax import lax
from jax.experimental import pallas as pl
from jax.experimental.pallas import tpu as pltpu

F32 = jnp.float32
BF16 = jnp.bfloat16

D_MODEL = 1024
PAST_LEN = 16384
RET_HEADS, RET_DK, RET_DV = 4, 128, 256
ROPE_BASE = 10000.0
HG_HEADS, HG_EXPAND, HG_DV = 8, 128, 128
SSM_HEADS, SSM_HEADDIM, SSM_GROUPS, SSM_STATE, SSM_CONV = 16, 64, 4, 128, 4
SSM_INNER = SSM_HEADS * SSM_HEADDIM
SSM_CONV_CH = SSM_INNER + 2 * SSM_GROUPS * SSM_STATE
N_BRANCH = 3
D_FF = 4 * D_MODEL
EPS = 1e-6
HG_CLIP_MIN = 1e-6

N_MAIN = 10240
N_PROJ = N_MAIN + 3 * D_MODEL
DT_PAD = 128

LANES = 128
SUBLANES = 8
VMEM_LIMIT = 56 * 1024 * 1024

CHUNK_ROWS = 256
DEC_TOKENS = 8


def _cparams(sem):
    return pltpu.CompilerParams(dimension_semantics=sem, vmem_limit_bytes=VMEM_LIMIT)


def _dot(a, b):
    return jnp.dot(a, b, preferred_element_type=F32)


def _dot_nt(a, b):
    return lax.dot_general(a, b, (((1,), (1,)), ((), ())), preferred_element_type=F32)


def _dot_tn(a, b):
    return lax.dot_general(a, b, (((0,), (0,)), ((), ())), preferred_element_type=F32)


def _split3(a):
    hi = a.astype(BF16)
    r1 = a - hi.astype(F32)
    mid = r1.astype(BF16)
    lo = (r1 - mid.astype(F32)).astype(BF16)
    return hi, mid, lo


def _dot3_left(m01, a):
    hi, mid, lo = _split3(a)
    return _dot(m01, hi) + _dot(m01, mid) + _dot(m01, lo)


def _dot3_right(a, m01):
    hi, mid, lo = _split3(a)
    return _dot(hi, m01) + _dot(mid, m01) + _dot(lo, m01)


def _sigmoid(x):
    return 1.0 / (1.0 + jnp.exp(-x))


def _silu(x):
    return x * _sigmoid(x)


def _softplus(x):
    return jnp.maximum(x, 0.0) + jnp.log1p(jnp.exp(-jnp.abs(x)))


def _rms(x, gain):
    return x * lax.rsqrt(jnp.mean(x * x, axis=-1, keepdims=True) + EPS) * gain


PROJ_TILE = 1024
MAIN_TILES = N_MAIN // PROJ_TILE
GATE_TILE0 = MAIN_TILES


def _project_tile(xn_ref, w_ref, o_ref):
    o_ref[...] = _dot_nt(xn_ref[...], w_ref[0].astype(BF16)).astype(o_ref.dtype)


def _rmsnorm_kernel(x_ref, g_ref, o_ref):
    o_ref[...] = _rms(x_ref[...], g_ref[...]).astype(o_ref.dtype)


def _rmsnorm_bf16(x, gain, tm):
    t = x.shape[0]
    rows = pl.BlockSpec((tm, D_MODEL), lambda i: (i, 0))
    return pl.pallas_call(
        _rmsnorm_kernel, grid=(t // tm,), in_specs=[rows, pl.BlockSpec((1, D_MODEL), lambda i: (0, 0))],
        out_specs=rows, out_shape=jax.ShapeDtypeStruct((t, D_MODEL), BF16),
        compiler_params=_cparams(("parallel",)), name="rmsnorm",
    )(x, gain)


def _in_proj(xn, w_in_t, layer, tm, out_dtype):
    t = xn.shape[0]
    tn = PROJ_TILE
    first_row = lambda j: (j * (tn // SUBLANES) + jnp.where(j >= MAIN_TILES, SSM_HEADS // SUBLANES, 0)) * SUBLANES
    return pl.pallas_call(
        _project_tile,
        grid=(t // tm, N_PROJ // tn),
        in_specs=[pl.BlockSpec((tm, D_MODEL), lambda i, j: (i, 0)),
                  pl.BlockSpec((pl.Element(1), pl.Element(tn), pl.Element(D_MODEL)),
                               lambda i, j: (layer, first_row(j), 0))],
        out_specs=pl.BlockSpec((tm, tn), lambda i, j: (i, j)),
        out_shape=jax.ShapeDtypeStruct((t, N_PROJ), out_dtype),
        compiler_params=_cparams(("parallel", "arbitrary")),
        name="in_proj",
    )(xn, w_in_t)


def _ret_log_gamma():
    return np.log1p(-np.exp2(-5.0 - np.arange(RET_HEADS, dtype=np.float64)))


def _ret_consts(r):
    lg = _ret_log_gamma()
    i = np.arange(r)
    diff = (i[:, None] - i[None, :]).astype(np.float64)
    dec = np.where(diff >= 0, np.exp(lg[:, None, None] * np.maximum(diff, 0.0)), 0.0)
    qd = np.exp(lg[:, None] * (i + 1.0))
    kd = np.exp(lg[:, None] * (r - 1.0 - i))
    gl = tuple(float(v) for v in np.exp(lg * r))
    bc = lambda a: jnp.asarray(np.broadcast_to(a[:, :, None], (RET_HEADS, r, LANES)), F32)
    return jnp.asarray(dec, F32), bc(qd), bc(kd), gl


def _rope_tables(pos):
    half = RET_DK // 2
    inv = ROPE_BASE ** (-(np.arange(half, dtype=np.float64) / half))
    ang = pos.astype(np.float64)[:, None] * inv[None, :]
    cos, sin = np.cos(ang), np.sin(ang)
    cos2 = np.concatenate([cos, cos], axis=1)
    sin2 = np.concatenate([-sin, sin], axis=1)
    scale = RET_DK ** -0.5
    return (jnp.asarray(cos2, F32), jnp.asarray(sin2, F32),
            jnp.asarray(cos2 * scale, F32), jnp.asarray(sin2 * scale, F32))


def _rope(t, cos2, sin2):
    return t * cos2 + pltpu.roll(t, RET_DK // 2, 1) * sin2


def _head_layernorm(o, gain):
    oc = o - jnp.mean(o, axis=-1, keepdims=True)
    return oc * lax.rsqrt(jnp.mean(oc * oc, axis=-1, keepdims=True) + EPS) * gain


def _ret_kernel(rq_ref, rk_ref, rv_ref, rg_ref, cq_ref, sq_ref, ck_ref, sk_ref, dec_ref, qd_ref, kd_ref,
                gain_ref, y_ref, sout_ref, s_scr, *, gl):
    c = pl.program_id(1)

    @pl.when(c == 0)
    def _():
        s_scr[...] = jnp.zeros_like(s_scr)

    cq, sq, ck, sk = cq_ref[...], sq_ref[...], ck_ref[...], sk_ref[...]
    for h in range(RET_HEADS):
        ks = slice(h * RET_DK, (h + 1) * RET_DK)
        vs = slice(h * RET_DV, (h + 1) * RET_DV)
        q = _rope(rq_ref[:, ks].astype(F32), cq, sq)
        k = _rope(rk_ref[:, ks].astype(F32), ck, sk)
        v = rv_ref[:, vs].astype(BF16)
        att = (_dot_nt(q.astype(BF16), k.astype(BF16)) * dec_ref[h]).astype(BF16)
        s_old = s_scr[h]
        o = _dot(att, v) + _dot((q * qd_ref[h]).astype(BF16), s_old.astype(BF16))
        s_scr[h] = gl[h] * s_old + _dot_tn((k * kd_ref[h]).astype(BF16), v)
        y = _silu(rg_ref[:, vs].astype(F32)) * _head_layernorm(o, gain_ref[:, vs])
        y_ref[:, vs] = y.astype(y_ref.dtype)

    @pl.when(c == pl.num_programs(1) - 1)
    def _():
        sout_ref[0] = s_scr[...]


def _ret_prompt(proj, gain, batch, seq):
    r = CHUNK_ROWS
    nc = seq // r
    dec, qd, kd, gl = _ret_consts(r)
    cq, sq, ck, sk = _rope_tables(np.arange(seq))
    row = lambda b, c: b * nc + c
    tab = pl.BlockSpec((r, RET_DK), lambda b, c: (c, 0))
    const3 = lambda shape: pl.BlockSpec(shape, lambda b, c: (0, 0, 0))
    return pl.pallas_call(
        functools.partial(_ret_kernel, gl=gl),
        grid=(batch, nc),
        in_specs=[
            pl.BlockSpec((r, 512), lambda b, c: (row(b, c), 0)),
            pl.BlockSpec((r, 512), lambda b, c: (row(b, c), 1)),
            pl.BlockSpec((r, 1024), lambda b, c: (row(b, c), 1)),
            pl.BlockSpec((r, 1024), lambda b, c: (row(b, c), 2)),
            tab, tab, tab, tab,
            const3((RET_HEADS, r, r)), const3((RET_HEADS, r, LANES)), const3((RET_HEADS, r, LANES)),
            pl.BlockSpec((1, 1024), lambda b, c: (0, 0)),
        ],
        out_specs=[
            pl.BlockSpec((r, 1024), lambda b, c: (row(b, c), 0)),
            pl.BlockSpec((1, RET_HEADS, RET_DK, RET_DV), lambda b, c: (b, 0, 0, 0)),
        ],
        out_shape=[jax.ShapeDtypeStruct((batch * seq, 1024), BF16),
                   jax.ShapeDtypeStruct((batch, RET_HEADS, RET_DK, RET_DV), F32)],
        scratch_shapes=[pltpu.VMEM((RET_HEADS, RET_DK, RET_DV), F32)],
        compiler_params=_cparams(("parallel", "arbitrary")),
        name="ret_prompt",
    )(proj, proj, proj, proj, cq, sq, ck, sk, dec, qd, kd, gain)


HG_CHUNK = 128
HG_LEVELS = (64, 32, 16, 8)
HG_TILE_REF = SUBLANES // 2 - 1
LOG2E = 1.4426950408889634


def _hg_masks():
    i = np.arange(HG_CHUNK)
    out = [(i[:, None] // (2 * b)) == (i[None, :] // (2 * b)) for b in HG_LEVELS[1:]]
    out.append(((i[:, None] // SUBLANES) == (i[None, :] // SUBLANES)) & (i[:, None] >= i[None, :]))
    return jnp.asarray(np.stack(out).astype(np.float32))


def _tri_ones(r):
    i = np.arange(r)
    return jnp.asarray((i[:, None] >= i[None, :]).astype(np.float32), BF16)


def _halves(a, nb):
    a5 = a.reshape(HG_CHUNK // (2 * nb * SUBLANES), 2, nb, SUBLANES, a.shape[-1])
    return a5[:, 0], a5[:, 1]


def _join_halves(lower, upper):
    return jnp.stack([lower, upper], axis=1).reshape(HG_CHUNK, lower.shape[-1])


def _tile_row_bcast(a, j):
    a3 = a.reshape(HG_CHUNK // SUBLANES, SUBLANES, a.shape[-1])
    return jnp.broadcast_to(a3[:, j:j + 1, :], a3.shape).reshape(a.shape)


def _per_head_nt(a, b):
    return [_dot_nt(a[:, h * LANES:(h + 1) * LANES], b[:, h * LANES:(h + 1) * LANES]) for h in range(HG_HEADS)]


def _hg_chunk_att(q, kk, g2, mask_ref):
    tile_last = _tile_row_bcast(g2, SUBLANES - 1)
    att = None
    for lvl, b in enumerate(HG_LEVELS):
        nb = b // SUBLANES
        _, q_up = _halves(q, nb)
        k_lo, _ = _halves(kk, nb)
        g_lo, g_up = _halves(g2, nb)
        ref = _halves(tile_last, nb)[0][:, nb - 1:nb]
        qe = q_up * jnp.exp2(g_up - ref)
        ke = k_lo * jnp.exp2(ref - g_lo)
        zeros = jnp.zeros_like(qe)
        prods = _per_head_nt(_join_halves(zeros, qe).astype(BF16), _join_halves(ke, zeros).astype(BF16))
        att = prods if lvl == 0 else [a + p * mask_ref[lvl - 1] for a, p in zip(att, prods)]
    d = g2 - _tile_row_bcast(g2, HG_TILE_REF)
    prods = _per_head_nt((q * jnp.exp2(d)).astype(BF16), (kk * jnp.exp2(-d)).astype(BF16))
    same_tile = mask_ref[len(HG_LEVELS) - 1] > 0.5
    return [a + jnp.where(same_tile, p, 0.0) for a, p in zip(att, prods)]


def _hgrn_kernel(hq_ref, hf_ref, hi_ref, hg_ref, lb_ref, gain_ref, tri_ref, mask_ref, y_ref, sout_ref,
                 st_scr, *, r):
    c = pl.program_id(1)

    @pl.when(c == 0)
    def _():
        st_scr[...] = jnp.zeros_like(st_scr)

    lb = lb_ref[...]
    for ck in range(r // HG_CHUNK):
        rows = slice(ck * HG_CHUNK, (ck + 1) * HG_CHUNK)
        q = _silu(hq_ref[rows, :].astype(F32))
        f = jnp.clip(lb + (1.0 - lb) * _sigmoid(hf_ref[rows, :].astype(F32)), HG_CLIP_MIN, 1.0)
        kk = 1.0 - f
        g2 = _dot3_left(tri_ref[...], jnp.log(f)) * LOG2E
        g2_last = g2[HG_CHUNK - 1:HG_CHUNK, :]
        att = _hg_chunk_att(q, kk, g2, mask_ref)
        q_in = (q * jnp.exp2(g2)).astype(BF16)
        k_out = (kk * jnp.exp2(g2_last - g2)).astype(BF16)
        e_last = jnp.exp2(g2_last)
        gate = _sigmoid(hg_ref[rows, :].astype(F32))
        for h in range(HG_HEADS):
            sl = slice(h * LANES, (h + 1) * LANES)
            v = hi_ref[rows, sl]
            st_old = st_scr[h]
            o = _dot(att[h].astype(BF16), v) + _dot_nt(q_in[:, sl], st_old.astype(BF16))
            st_scr[h] = st_old * e_last[:, sl] + _dot_tn(v, k_out[:, sl])
            o = o * lax.rsqrt(jnp.mean(o * o, axis=-1, keepdims=True) + EPS) * gain_ref[:, sl]
            y_ref[rows, sl] = (o * gate[:, sl]).astype(y_ref.dtype)

    @pl.when(c == pl.num_programs(1) - 1)
    def _():
        for h in range(HG_HEADS):
            sout_ref[0, h] = st_scr[h].T


def _hgrn_prompt(proj, lb, gain, batch, seq):
    r = CHUNK_ROWS
    nc = seq // r
    masks = _hg_masks()
    row = lambda b, c: b * nc + c
    blk = lambda j: pl.BlockSpec((r, 1024), lambda b, c: (row(b, c), j))
    vec = pl.BlockSpec((1, 1024), lambda b, c: (0, 0))
    return pl.pallas_call(
        functools.partial(_hgrn_kernel, r=r),
        grid=(batch, nc),
        in_specs=[blk(3), blk(4), blk(5), blk(6), vec, vec,
                  pl.BlockSpec((HG_CHUNK, HG_CHUNK), lambda b, c: (0, 0)),
                  pl.BlockSpec(masks.shape, lambda b, c: (0, 0, 0))],
        out_specs=[
            pl.BlockSpec((r, 1024), lambda b, c: (row(b, c), 0)),
            pl.BlockSpec((1, HG_HEADS, HG_EXPAND, HG_DV), lambda b, c: (b, 0, 0, 0)),
        ],
        out_shape=[jax.ShapeDtypeStruct((batch * seq, 1024), BF16),
                   jax.ShapeDtypeStruct((batch, HG_HEADS, HG_EXPAND, HG_DV), F32)],
        scratch_shapes=[pltpu.VMEM((HG_HEADS, HG_DV, HG_EXPAND), F32)],
        compiler_params=_cparams(("parallel", "arbitrary")),
        name="hgrn_prompt",
    )(proj, proj, proj, proj, lb, gain, _tri_ones(HG_CHUNK), masks)


HEADS_PER_GROUP = SSM_HEADS // SSM_GROUPS
GROUP_W = HEADS_PER_GROUP * SSM_HEADDIM
CONV_PAD = SUBLANES
SSD_CHUNK = 128


def _head_expand(vals, h0, lane_head):
    out = jnp.zeros((vals.shape[0], GROUP_W), F32)
    for j in range(HEADS_PER_GROUP):
        out = jnp.where(lane_head == j, vals[:, h0 + j:h0 + j + 1], out)
    return out


def _ssd_kernel(z_ref, xbc_ref, xn_ref, wdt_ref, cw_ref, cb_ref, dtb_ref, alog_ref, dsk_ref, norm_ref, tri_ref,
                shift_ref, y_ref, sout_ref, cout_ref, hist_scr, s_scr, o_scr, *, r):
    c = pl.program_id(1)
    last = pl.num_programs(1) - 1

    @pl.when(c == 0)
    def _():
        hist_scr[0:CONV_PAD, :] = jnp.zeros((CONV_PAD, SSM_CONV_CH), F32)
        s_scr[...] = jnp.zeros_like(s_scr)

    u_bf = xbc_ref[...]
    u = u_bf.astype(F32)
    hist_scr[CONV_PAD:2 * CONV_PAD, :] = u[0:CONV_PAD]
    conv = cb_ref[...] + cw_ref[SSM_CONV - 1:SSM_CONV, :] * u
    head = cb_ref[...]
    for w in range(SSM_CONV):
        off = CONV_PAD - (SSM_CONV - 1) + w
        head = head + cw_ref[w:w + 1, :] * hist_scr[off:off + CONV_PAD, :]
    for back in range(1, SSM_CONV):
        conv = conv + cw_ref[SSM_CONV - 1 - back:SSM_CONV - back, :] * _dot(shift_ref[back - 1], u_bf)
    conv = jnp.concatenate([head, conv[CONV_PAD:]], axis=0)

    @pl.when(c == last)
    def _():
        cout_ref[0] = u[r - (SSM_CONV - 1):r, :]

    hist_scr[0:CONV_PAD, :] = u[r - CONV_PAD:r]

    xc = _silu(conv)
    xs = xc[:, :SSM_INNER]
    bm = xc[:, SSM_INNER:SSM_INNER + SSM_GROUPS * SSM_STATE]
    cm = xc[:, SSM_INNER + SSM_GROUPS * SSM_STATE:]

    dt_all = _softplus(_dot_nt(xn_ref[...], wdt_ref[...].astype(BF16)) + dtb_ref[...])
    rate = -jnp.exp(alog_ref[...]) * LOG2E
    rc = SSD_CHUNK
    ri = lax.broadcasted_iota(jnp.int32, (rc, rc), 0)
    ci = lax.broadcasted_iota(jnp.int32, (rc, rc), 1)
    causal = ri >= ci
    lane_head = lax.broadcasted_iota(jnp.int32, (rc, GROUP_W), 1) // SSM_HEADDIM
    lane_head1 = lax.broadcasted_iota(jnp.int32, (1, GROUP_W), 1) // SSM_HEADDIM
    groups = range(SSM_GROUPS)

    for ck in range(r // rc):
        rows = slice(ck * rc, (ck + 1) * rc)
        dt = dt_all[rows]
        g2 = _dot3_left(tri_ref[...], dt * rate)
        g2_last = g2[rc - 1:rc, :]
        g2_t = g2.T
        dt_t = dt.T
        e_cum = jnp.exp2(g2)
        w_in = dt * jnp.exp2(g2_last - g2)
        e_last = jnp.exp2(g2_last)

        cg = [cm[rows, g * SSM_STATE:(g + 1) * SSM_STATE].astype(BF16) for g in groups]
        bg = [bm[rows, g * SSM_STATE:(g + 1) * SSM_STATE].astype(BF16) for g in groups]
        xs_g = [xs[rows, g * GROUP_W:(g + 1) * GROUP_W] for g in groups]
        xs_b = [x.astype(BF16) for x in xs_g]
        s_old = [s_scr[g] for g in groups]
        cb = [jnp.where(causal, _dot_nt(cg[g], bg[g]), 0.0) for g in groups]
        carried = [_dot(cg[g], s_old[g].astype(BF16)) for g in groups]
        for g in groups:
            h0 = g * HEADS_PER_GROUP
            o_g = _head_expand(e_cum, h0, lane_head) * carried[g]
            for j in range(HEADS_PER_GROUP):
                h = h0 + j
                dec = jnp.exp2(jnp.minimum(g2[:, h:h + 1] - g2_t[h:h + 1, :], 0.0))
                att = (cb[g] * dec * dt_t[h:h + 1, :]).astype(BF16)
                o_g = o_g + jnp.where(lane_head == j, _dot(att, xs_b[g]), 0.0)
            s_scr[g] = (s_old[g] * _head_expand(e_last, h0, lane_head1)
                        + _dot_tn(bg[g], (xs_g[g] * _head_expand(w_in, h0, lane_head)).astype(BF16)))
            o_scr[rows, g * GROUP_W:(g + 1) * GROUP_W] = o_g + dsk_ref[:, g * GROUP_W:(g + 1) * GROUP_W] * xs_g[g]

    y_ref[...] = _rms(o_scr[...] * _silu(z_ref[...].astype(F32)), norm_ref[...]).astype(y_ref.dtype)

    @pl.when(c == last)
    def _():
        for h in range(SSM_HEADS):
            j = h % HEADS_PER_GROUP
            sout_ref[0, h] = s_scr[h // HEADS_PER_GROUP][:, j * SSM_HEADDIM:(j + 1) * SSM_HEADDIM]


def _shift_ones(r):
    i = np.arange(r)
    mats = [(i[:, None] - i[None, :]) == k for k in range(1, SSM_CONV)]
    return jnp.asarray(np.stack(mats).astype(np.float32), BF16)


def _pad_lanes(v):
    return jnp.pad(v.astype(F32), (0, LANES - v.shape[0])).reshape(1, LANES)


def _ssd_prompt(proj, xn, w_dt, conv_w, conv_b, dt_bias, a_log, dsk_exp, norm, batch, seq):
    r = CHUNK_ROWS
    nc = seq // r
    row = lambda b, c: b * nc + c
    vec = lambda n: pl.BlockSpec((1, n), lambda b, c: (0, 0))
    return pl.pallas_call(
        functools.partial(_ssd_kernel, r=r),
        grid=(batch, nc),
        in_specs=[
            pl.BlockSpec((r, 1024), lambda b, c: (row(b, c), 7)),
            pl.BlockSpec((r, SSM_CONV_CH), lambda b, c: (row(b, c), 4)),
            pl.BlockSpec((r, D_MODEL), lambda b, c: (row(b, c), 0)),
            pl.BlockSpec((DT_PAD, D_MODEL), lambda b, c: (0, 0)),
            pl.BlockSpec((SSM_CONV, SSM_CONV_CH), lambda b, c: (0, 0)),
            vec(SSM_CONV_CH), vec(LANES), vec(LANES), vec(SSM_INNER), vec(SSM_INNER),
            pl.BlockSpec((SSD_CHUNK, SSD_CHUNK), lambda b, c: (0, 0)),
            pl.BlockSpec((SSM_CONV - 1, r, r), lambda b, c: (0, 0, 0)),
        ],
        out_specs=[
            pl.BlockSpec((r, 1024), lambda b, c: (row(b, c), 0)),
            pl.BlockSpec((1, SSM_HEADS, SSM_STATE, SSM_HEADDIM), lambda b, c: (b, 0, 0, 0)),
            pl.BlockSpec((1, SSM_CONV - 1, SSM_CONV_CH), lambda b, c: (b, 0, 0)),
        ],
        out_shape=[jax.ShapeDtypeStruct((batch * seq, 1024), BF16),
                   jax.ShapeDtypeStruct((batch, SSM_HEADS, SSM_STATE, SSM_HEADDIM), F32),
                   jax.ShapeDtypeStruct((batch, SSM_CONV - 1, SSM_CONV_CH), F32)],
        scratch_shapes=[pltpu.VMEM((2 * CONV_PAD, SSM_CONV_CH), F32),
                        pltpu.VMEM((SSM_GROUPS, SSM_STATE, GROUP_W), F32),
                        pltpu.VMEM((r, SSM_INNER), F32)],
        compiler_params=_cparams(("parallel", "arbitrary")),
        name="ssd_prompt",
    )(proj, proj, xn, w_dt, conv_w, conv_b.reshape(1, -1), _pad_lanes(dt_bias), _pad_lanes(a_log), dsk_exp, norm,
      _tri_ones(SSD_CHUNK), _shift_ones(r))


def _merge_kernel(x_ref, yr_ref, yh_ref, ym_ref, g0_ref, g1_ref, g2_ref, wb_ref, wo_ref, npost_ref, nnext_ref,
                  o_ref, xn_ref):
    merged = _sigmoid(g0_ref[...].astype(F32)) * _dot(yr_ref[...].astype(BF16), wb_ref[0])
    merged = merged + _sigmoid(g1_ref[...].astype(F32)) * _dot(yh_ref[...].astype(BF16), wb_ref[1])
    merged = merged + _sigmoid(g2_ref[...].astype(F32)) * _dot(ym_ref[...].astype(BF16), wb_ref[2])
    m = _dot(merged.astype(BF16), wo_ref[...])
    out = x_ref[...] + _rms(m, npost_ref[...])
    o_ref[...] = out
    xn_ref[...] = _rms(out, nnext_ref[...]).astype(BF16)


def _merge(x, yr, yh, ym, proj, wb, wo, npost, nnext, tm):
    t = x.shape[0]
    rows = lambda j: pl.BlockSpec((tm, D_MODEL), lambda i: (i, j))
    vec = pl.BlockSpec((1, D_MODEL), lambda i: (0, 0))
    return pl.pallas_call(
        _merge_kernel,
        grid=(t // tm,),
        in_specs=[rows(0), rows(0), rows(0), rows(0), rows(GATE_TILE0), rows(GATE_TILE0 + 1), rows(GATE_TILE0 + 2),
                  pl.BlockSpec((3, D_MODEL, D_MODEL), lambda i: (0, 0, 0)),
                  pl.BlockSpec((D_MODEL, D_MODEL), lambda i: (0, 0)), vec, vec],
        out_specs=[rows(0), rows(0)],
        out_shape=[jax.ShapeDtypeStruct((t, D_MODEL), F32), jax.ShapeDtypeStruct((t, D_MODEL), BF16)],
        compiler_params=_cparams(("parallel",)),
        name="merge",
    )(x, yr, yh, ym, proj, proj, proj, wb, wo, npost, nnext)


def _ffn_kernel(x_ref, xn_ref, wu_ref, wd_ref, npost_ref, *rest):
    (nnext_ref, o_ref, xnn_ref, acc_scr) = rest if len(rest) == 4 else (None, rest[0], None, rest[1])
    j = pl.program_id(1)

    @pl.when(j == 0)
    def _():
        acc_scr[...] = jnp.zeros_like(acc_scr)

    hid = jnp.square(jnp.maximum(_dot(xn_ref[...], wu_ref[0].astype(BF16)), 0.0))
    acc_scr[...] += _dot(hid.astype(BF16), wd_ref[0].astype(BF16))

    @pl.when(j == pl.num_programs(1) - 1)
    def _():
        out = x_ref[...] + _rms(acc_scr[...], npost_ref[...])
        o_ref[...] = out
        if xnn_ref is not None:
            xnn_ref[...] = _rms(out, nnext_ref[...]).astype(BF16)


def _ffn(x, xn, npost, nnext, w_up, w_down, layer, tm):
    t = x.shape[0]
    tf = 1024
    vec = pl.BlockSpec((1, D_MODEL), lambda i, j: (0, 0))
    rows = pl.BlockSpec((tm, D_MODEL), lambda i, j: (i, 0))
    chain = nnext is not None
    wu, wd = w_up, w_down
    outs = pl.pallas_call(
        _ffn_kernel,
        grid=(t // tm, D_FF // tf),
        in_specs=[rows, rows,
                  pl.BlockSpec((1, D_MODEL, tf), lambda i, j: (layer, 0, j)),
                  pl.BlockSpec((1, tf, D_MODEL), lambda i, j: (layer, j, 0)), vec] + ([vec] if chain else []),
        out_specs=[rows, rows] if chain else [rows],
        out_shape=[jax.ShapeDtypeStruct((t, D_MODEL), F32)]
        + ([jax.ShapeDtypeStruct((t, D_MODEL), BF16)] if chain else []),
        scratch_shapes=[pltpu.VMEM((tm, D_MODEL), F32)],
        compiler_params=_cparams(("parallel", "arbitrary")),
        name="ffn",
    )(x, xn, wu, wd, npost, *([nnext] if chain else []))
    return (outs[0], outs[1]) if chain else (outs[0], None)


def _dec_prep_kernel(proj_ref, xn_ref, wdt_ref, cst_ref, cq_ref, sq_ref, ck_ref, sk_ref, lb_ref, cw_ref, cb_ref,
                     dtb_ref, alog_ref, hexp_ref,
                     cret_ref, chg_ref, cssd_ref, xs_ref, xdt_ref, drow_ref, cnew_ref):
    n = proj_ref.shape[0]
    cq, sq, ck, sk = cq_ref[...], sq_ref[...], ck_ref[...], sk_ref[...]
    for h in range(RET_HEADS):
        ks = slice(h * RET_DK, (h + 1) * RET_DK)
        cret_ref[:, ks] = _rope(proj_ref[:, ks], cq, sq)
        k_off = RET_HEADS * RET_DK
        cret_ref[:, k_off + h * RET_DK:k_off + (h + 1) * RET_DK] = _rope(
            proj_ref[:, k_off + h * RET_DK:k_off + (h + 1) * RET_DK], ck, sk)

    lb = lb_ref[...]
    chg_ref[:, 0:1024] = _silu(proj_ref[:, 3072:4096])
    chg_ref[:, 1024:2048] = jnp.clip(lb + (1.0 - lb) * _sigmoid(proj_ref[:, 4096:5120]), HG_CLIP_MIN, 1.0)

    u = proj_ref[:, 8192:8192 + SSM_CONV_CH]
    conv = cb_ref[...] + cw_ref[SSM_CONV - 1:SSM_CONV, :] * u
    for w in range(SSM_CONV - 1):
        conv = conv + cw_ref[w:w + 1, :] * cst_ref[w]
    cnew_ref[0] = cst_ref[1]
    cnew_ref[1] = cst_ref[2]
    cnew_ref[2] = u
    xc = _silu(conv)
    xs = xc[:, :SSM_INNER]
    xs_ref[...] = xs
    cssd_ref[:, 0:512] = xc[:, SSM_INNER + 512:]
    cssd_ref[:, 512:1024] = xc[:, SSM_INNER:SSM_INNER + 512]
    dt = _softplus(_dot_nt(xn_ref[...], wdt_ref[...].astype(BF16)) + dtb_ref[...])
    decay = jnp.exp(dt * (-jnp.exp(alog_ref[...])))
    hexp = hexp_ref[...]
    xdt_ref[...] = xs * _dot3_right(dt, hexp)
    drow_ref[...] = _dot3_right(decay, hexp)
    del n


def _dec_prep(proj, xn, w_dt, conv_state_t, lb, conv_w, conv_b, dt_bias, a_log):
    n = proj.shape[0]
    cq, sq, ck, sk = _rope_tables(np.full((n,), PAST_LEN))
    hexp = np.zeros((LANES, SSM_INNER), np.float32)
    for h in range(SSM_HEADS):
        hexp[h, h * SSM_HEADDIM:(h + 1) * SSM_HEADDIM] = 1.0
    outs = [(n, 1024), (n, 2048), (n, 1024), (n, 1024), (n, 1024), (n, 1024)]
    return pl.pallas_call(
        _dec_prep_kernel,
        out_shape=[jax.ShapeDtypeStruct(s, F32) for s in outs]
        + [jax.ShapeDtypeStruct((SSM_CONV - 1, n, SSM_CONV_CH), F32)],
        compiler_params=pltpu.CompilerParams(vmem_limit_bytes=VMEM_LIMIT),
        name="dec_prep",
    )(proj, xn, w_dt, conv_state_t, cq, sq, ck, sk, lb, conv_w, conv_b.reshape(1, -1), _pad_lanes(dt_bias),
      _pad_lanes(a_log), jnp.asarray(hexp, BF16))


def _cols(a, tb):
    n, ch = a.shape
    return a.reshape(n // tb, tb, ch).transpose(0, 2, 1)


def _state_update_call(body, name, layer, state, carried, n_tokens, heads_per_step, in_specs, operands,
                       scratch):
    depth, _, heads, sk, sv = state.shape
    tb = DEC_TOKENS
    in_spec = pl.BlockSpec((1, tb, heads_per_step, sk, sv), lambda t, j: (layer, t, j, 0, 0))
    if carried is None:
        carried, aliases = state, {}
        out_spec = pl.BlockSpec((depth, tb, heads_per_step, sk, sv), lambda t, j: (0, t, j, 0, 0))
        own, fill = layer, tuple(d for d in range(depth) if d != layer)
    else:
        aliases = {1: 1}
        out_spec, own, fill = in_spec, 0, ()
    return pl.pallas_call(
        functools.partial(body, own=own, fill=fill),
        grid=(n_tokens // tb, heads // heads_per_step),
        in_specs=[in_spec, pl.BlockSpec(memory_space=pl.ANY)] + in_specs,
        out_specs=[pl.BlockSpec((tb, 1024), lambda t, j: (t, 0)), out_spec],
        out_shape=[jax.ShapeDtypeStruct((n_tokens, 1024), F32), jax.ShapeDtypeStruct(state.shape, F32)],
        input_output_aliases=aliases,
        scratch_shapes=[scratch],
        compiler_params=_cparams(("parallel", "arbitrary")),
        name=name,
    )(state, carried, *operands)


def _zero_fill(so_ref, fill):
    for d in fill:
        so_ref[d] = jnp.zeros(so_ref.shape[1:], so_ref.dtype)


def _ret_dec_kernel(s_ref, carried_ref, ct_ref, v_ref, rg_ref, gain_ref, y_ref, so_ref, o_scr, *, gamma, own,
                    fill):
    del carried_ref
    for i in range(DEC_TOKENS):
        for h in range(RET_HEADS):
            qcol = ct_ref[0, h * RET_DK:(h + 1) * RET_DK, i:i + 1]
            kcol = ct_ref[0, (RET_HEADS + h) * RET_DK:(RET_HEADS + h + 1) * RET_DK, i:i + 1]
            vs = slice(h * RET_DV, (h + 1) * RET_DV)
            s_new = gamma[h] * s_ref[0, i, h] + kcol * v_ref[i:i + 1, vs]
            so_ref[own, i, h] = s_new
            o_scr[i:i + 1, vs] = jnp.sum(qcol * s_new, axis=0, keepdims=True)
    _zero_fill(so_ref, fill)
    for h in range(RET_HEADS):
        vs = slice(h * RET_DV, (h + 1) * RET_DV)
        y_ref[:, vs] = _silu(rg_ref[:, vs]) * _head_layernorm(o_scr[:, vs], gain_ref[:, vs])


def _ret_dec(layer, state, carried, c_ret, proj, gain):
    tb = DEC_TOKENS
    gamma = tuple(float(v) for v in np.exp(_ret_log_gamma()))
    return _state_update_call(
        functools.partial(_ret_dec_kernel, gamma=gamma), "ret_dec", layer, state, carried, proj.shape[0],
        RET_HEADS,
        [pl.BlockSpec((1, 2 * RET_HEADS * RET_DK, tb), lambda t, j: (t, 0, 0)),
         pl.BlockSpec((tb, 1024), lambda t, j: (t, 1)), pl.BlockSpec((tb, 1024), lambda t, j: (t, 2)),
         pl.BlockSpec((1, 1024), lambda t, j: (0, 0))],
        (_cols(c_ret, tb), proj, proj, gain), pltpu.VMEM((tb, 1024), F32))


def _hgrn_dec_kernel(s_ref, carried_ref, ct_ref, v_ref, hg_ref, gain_ref, y_ref, so_ref, o_scr, *, own, fill):
    del carried_ref
    for i in range(DEC_TOKENS):
        for h in range(HG_HEADS):
            sl = slice(h * LANES, (h + 1) * LANES)
            qcol = ct_ref[0, h * HG_EXPAND:(h + 1) * HG_EXPAND, i:i + 1]
            fcol = ct_ref[0, (HG_HEADS + h) * HG_EXPAND:(HG_HEADS + h + 1) * HG_EXPAND, i:i + 1]
            s_new = fcol * s_ref[0, i, h] + (1.0 - fcol) * v_ref[i:i + 1, sl]
            so_ref[own, i, h] = s_new
            o_scr[i:i + 1, sl] = jnp.sum(qcol * s_new, axis=0, keepdims=True)
    _zero_fill(so_ref, fill)
    for h in range(HG_HEADS):
        sl = slice(h * LANES, (h + 1) * LANES)
        o = o_scr[:, sl]
        o = o * lax.rsqrt(jnp.mean(o * o, axis=-1, keepdims=True) + EPS) * gain_ref[:, sl]
        y_ref[:, sl] = o * _sigmoid(hg_ref[:, sl])


def _hgrn_dec(layer, state, carried, c_hg, proj, gain):
    tb = DEC_TOKENS
    return _state_update_call(
        _hgrn_dec_kernel, "hgrn_dec", layer, state, carried, proj.shape[0], HG_HEADS,
        [pl.BlockSpec((1, 2 * HG_HEADS * HG_EXPAND, tb), lambda t, j: (t, 0, 0)),
         pl.BlockSpec((tb, 1024), lambda t, j: (t, 5)), pl.BlockSpec((tb, 1024), lambda t, j: (t, 6)),
         pl.BlockSpec((1, 1024), lambda t, j: (0, 0))],
        (_cols(c_hg, tb), proj, proj, gain), pltpu.VMEM((tb, 1024), F32))


def _ssd_dec_kernel(s_ref, carried_ref, c_ref, b_ref, xdt_ref, drow_ref, xs_ref, z_ref, dsk_ref, norm_ref,
                    y_ref, so_ref, o_scr, *, own, fill):
    del carried_ref
    for g in range(SSM_GROUPS):
        c_rows = c_ref[:, g * SSM_STATE:(g + 1) * SSM_STATE].astype(BF16)
        for i in range(DEC_TOKENS):
            b_row = b_ref[i:i + 1, g * SSM_STATE:(g + 1) * SSM_STATE]
            new = []
            for h in range(g * HEADS_PER_GROUP, (g + 1) * HEADS_PER_GROUP):
                hs = slice(h * SSM_HEADDIM, (h + 1) * SSM_HEADDIM)
                decay = drow_ref[i:i + 1, h * SSM_HEADDIM:h * SSM_HEADDIM + 1]
                s_new = decay * s_ref[0, i, h] + xdt_ref[0, hs, i:i + 1] * b_row
                so_ref[own, i, h] = s_new
                new.append(s_new)
            s_grp = jnp.concatenate(new, axis=0).astype(BF16)
            o_scr[i:i + 1, g * GROUP_W:(g + 1) * GROUP_W] = _dot_nt(c_rows, s_grp)[i:i + 1, :]
    _zero_fill(so_ref, fill)
    o = o_scr[...] + dsk_ref[...] * xs_ref[...]
    y_ref[...] = _rms(o * _silu(z_ref[...]), norm_ref[...])


def _ssd_dec(layer, state, carried, c_ssd, xs, xdt, drow, proj, dsk_exp, norm):
    tb = DEC_TOKENS
    rows = lambda col: pl.BlockSpec((tb, 1024), lambda t, j: (t, col))
    half = lambda col: pl.BlockSpec((tb, SSM_GROUPS * SSM_STATE), lambda t, j: (t, col))
    vec = pl.BlockSpec((1, 1024), lambda t, j: (0, 0))
    return _state_update_call(
        _ssd_dec_kernel, "ssd_dec", layer, state, carried, proj.shape[0], SSM_HEADS,
        [half(0), half(1), pl.BlockSpec((1, SSM_INNER, tb), lambda t, j: (t, 0, 0)),
         rows(0), rows(0), rows(7), vec, vec],
        (c_ssd, c_ssd, _cols(xdt, tb), drow, xs, proj, dsk_exp, norm), pltpu.VMEM((tb, 1024), F32))


def kernel(x_prompt, x_sample, state_ret, state_hgrn, state_ssm, state_conv, w_in, ret_norm, hg_norm,
           hg_lb_logits, conv_w, conv_b, dt_bias, a_log, d_skip, ssm_norm, w_branch, w_out, norm_mix_pre,
           norm_mix_post, norm_ffn_pre, norm_ffn_post, w_up, w_down):
    bp, lp, _ = x_prompt.shape
    bs, ls, _ = x_sample.shape
    assert ls == 1 and lp % CHUNK_ROWS == 0 and bs % DEC_TOKENS == 0
    depth = w_in.shape[0]

    lb_w = jax.nn.softmax(hg_lb_logits.astype(F32), axis=0)
    lb_all = jnp.cumsum(lb_w, axis=0) - lb_w[0]

    hp = x_prompt.reshape(bp * lp, D_MODEL)
    hs = x_sample.reshape(bs, D_MODEL)
    outs = [[] for _ in range(5)]
    new_ret = new_hg = new_ssm = None
    ssm_t = jnp.swapaxes(state_ssm, -1, -2)
    w_in_t = jnp.swapaxes(w_in, 1, 2)
    vec = lambda a: a.reshape(1, -1).astype(F32)
    hp_normed = _rmsnorm_bf16(hp, vec(norm_mix_pre[0]), tm=min(2048, bp * lp))
    hs_normed = _rmsnorm_bf16(hs, vec(norm_mix_pre[0]), tm=bs)
    for l in range(depth):
        next_gain = vec(norm_mix_pre[l + 1]) if l + 1 < depth else None
        w_dt = jnp.pad(w_in_t[l, N_MAIN:N_MAIN + SSM_HEADS], ((0, DT_PAD - SSM_HEADS), (0, 0)))
        wb, wo = w_branch[l].astype(BF16), w_out[l].astype(BF16)
        dsk_exp = jnp.repeat(d_skip[l].astype(F32), SSM_HEADDIM).reshape(1, SSM_INNER)
        lb = vec(lb_all[l])

        xn = hp_normed
        proj = _in_proj(xn, w_in_t, l, tm=min(4096, bp * lp), out_dtype=BF16)
        yr, s_ret = _ret_prompt(proj, vec(ret_norm[l]), bp, lp)
        yh, s_hg = _hgrn_prompt(proj, lb, vec(hg_norm[l]), bp, lp)
        ym, s_ssm, s_conv = _ssd_prompt(proj, xn, w_dt, conv_w[l], conv_b[l], dt_bias[l], a_log[l],
                                        dsk_exp, vec(ssm_norm[l]), bp, lp)
        hp, xn = _merge(hp, yr, yh, ym, proj, wb, wo, vec(norm_mix_post[l]), vec(norm_ffn_pre[l]), tm=512)
        hp, hp_normed = _ffn(hp, xn, vec(norm_ffn_post[l]), next_gain, w_up, w_down, l, tm=1024)
        for lst, val in zip(outs[:4], (s_ret, s_hg, s_ssm, s_conv)):
            lst.append(val)

        xn = hs_normed
        proj = _in_proj(xn, w_in_t, l, tm=bs, out_dtype=F32)
        c_ret, c_hg, c_ssd, xs, xdt, drow, conv_new = _dec_prep(
            proj, xn, w_dt, state_conv[l].transpose(1, 0, 2), lb, conv_w[l], conv_b[l], dt_bias[l], a_log[l])
        yr, new_ret = _ret_dec(l, state_ret, new_ret, c_ret, proj, vec(ret_norm[l]))
        yh, new_hg = _hgrn_dec(l, state_hgrn, new_hg, c_hg, proj, vec(hg_norm[l]))
        ym, new_ssm = _ssd_dec(l, ssm_t, new_ssm, c_ssd, xs, xdt, drow, proj, dsk_exp, vec(ssm_norm[l]))
        hs, xn = _merge(hs, yr, yh, ym, proj, wb, wo, vec(norm_mix_post[l]), vec(norm_ffn_pre[l]), tm=bs)
        hs, hs_normed = _ffn(hs, xn, vec(norm_ffn_post[l]), next_gain, w_up, w_down, l, tm=bs)
        outs[4].append(conv_new.transpose(1, 0, 2))

    p_ret, p_hg, p_ssm, p_conv, s_conv = [jnp.stack(o) for o in outs]
    return (hp.reshape(bp, lp, D_MODEL), hs.reshape(bs, ls, D_MODEL), p_ret, p_hg, p_ssm, p_conv,
            new_ret, new_hg, jnp.swapaxes(new_ssm, -1, -2), s_conv)
```

```python
import functools

import numpy as np
import jax
import jax.numpy as jnp
from jax import lax
from jax.experimental import pallas as pl
from jax.experimental.pallas import tpu as pltpu

F32 = jnp.float32
BF16 = jnp.bfloat16

D_MODEL = 1024
PAST_LEN = 16384
RET_HEADS, RET_DK, RET_DV = 4, 128, 256
ROPE_BASE = 10000.0
HG_HEADS, HG_EXPAND, HG_DV = 8, 128, 128
SSM_HEADS, SSM_HEADDIM, SSM_GROUPS, SSM_STATE, SSM_CONV = 16, 64, 4, 128, 4
SSM_INNER = SSM_HEADS * SSM_HEADDIM
SSM_CONV_CH = SSM_INNER + 2 * SSM_GROUPS * SSM_STATE
N_BRANCH = 3
D_FF = 4 * D_MODEL
EPS = 1e-6
HG_CLIP_MIN = 1e-6

N_MAIN = 10240
N_PROJ = N_MAIN + 3 * D_MODEL
DT_PAD = 128

LANES = 128
SUBLANES = 8
VMEM_LIMIT = 56 * 1024 * 1024

CHUNK_ROWS = 256
DEC_TOKENS = 8


def _cparams(sem):
    return pltpu.CompilerParams(dimension_semantics=sem, vmem_limit_bytes=VMEM_LIMIT)


def _dot(a, b):
    return jnp.dot(a, b, preferred_element_type=F32)


def _dot_nt(a, b):
    return lax.dot_general(a, b, (((1,), (1,)), ((), ())), preferred_element_type=F32)


def _dot_tn(a, b):
    return lax.dot_general(a, b, (((0,), (0,)), ((), ())), preferred_element_type=F32)


def _split3(a):
    hi = a.astype(BF16)
    r1 = a - hi.astype(F32)
    mid = r1.astype(BF16)
    lo = (r1 - mid.astype(F32)).astype(BF16)
    return hi, mid, lo


def _dot3_left(m01, a):
    hi, mid, lo = _split3(a)
    return _dot(m01, hi) + _dot(m01, mid) + _dot(m01, lo)


def _dot3_right(a, m01):
    hi, mid, lo = _split3(a)
    return _dot(hi, m01) + _dot(mid, m01) + _dot(lo, m01)


def _sigmoid(x):
    return 1.0 / (1.0 + jnp.exp(-x))


def _silu(x):
    return x * _sigmoid(x)


def _softplus(x):
    return jnp.maximum(x, 0.0) + jnp.log1p(jnp.exp(-jnp.abs(x)))


def _rms(x, gain):
    return x * lax.rsqrt(jnp.mean(x * x, axis=-1, keepdims=True) + EPS) * gain


PROJ_TILE = 1024
MAIN_TILES = N_MAIN // PROJ_TILE
GATE_TILE0 = MAIN_TILES


def _project_tile(xn_ref, w_ref, o_ref):
    o_ref[...] = _dot_nt(xn_ref[...], w_ref[0].astype(BF16)).astype(o_ref.dtype)


def _rmsnorm_kernel(x_ref, g_ref, o_ref):
    o_ref[...] = _rms(x_ref[...], g_ref[...]).astype(o_ref.dtype)


def _rmsnorm_bf16(x, gain, tm):
    t = x.shape[0]
    rows = pl.BlockSpec((tm, D_MODEL), lambda i: (i, 0))
    return pl.pallas_call(
        _rmsnorm_kernel, grid=(t // tm,), in_specs=[rows, pl.BlockSpec((1, D_MODEL), lambda i: (0, 0))],
        out_specs=rows, out_shape=jax.ShapeDtypeStruct((t, D_MODEL), BF16),
        compiler_params=_cparams(("parallel",)), name="rmsnorm",
    )(x, gain)


def _in_proj(xn, w_in_t, layer, tm, out_dtype):
    t = xn.shape[0]
    tn = PROJ_TILE
    first_row = lambda j: (j * (tn // SUBLANES) + jnp.where(j >= MAIN_TILES, SSM_HEADS // SUBLANES, 0)) * SUBLANES
    return pl.pallas_call(
        _project_tile,
        grid=(t // tm, N_PROJ // tn),
        in_specs=[pl.BlockSpec((tm, D_MODEL), lambda i, j: (i, 0)),
                  pl.BlockSpec((pl.Element(1), pl.Element(tn), pl.Element(D_MODEL)),
                               lambda i, j: (layer, first_row(j), 0))],
        out_specs=pl.BlockSpec((tm, tn), lambda i, j: (i, j)),
        out_shape=jax.ShapeDtypeStruct((t, N_PROJ), out_dtype),
        compiler_params=_cparams(("parallel", "arbitrary")),
        name="in_proj",
    )(xn, w_in_t)


def _ret_log_gamma():
    return np.log1p(-np.exp2(-5.0 - np.arange(RET_HEADS, dtype=np.float64)))


def _ret_consts(r):
    lg = _ret_log_gamma()
    i = np.arange(r)
    diff = (i[:, None] - i[None, :]).astype(np.float64)
    dec = np.where(diff >= 0, np.exp(lg[:, None, None] * np.maximum(diff, 0.0)), 0.0)
    qd = np.exp(lg[:, None] * (i + 1.0))
    kd = np.exp(lg[:, None] * (r - 1.0 - i))
    gl = tuple(float(v) for v in np.exp(lg * r))
    bc = lambda a: jnp.asarray(np.broadcast_to(a[:, :, None], (RET_HEADS, r, LANES)), F32)
    return jnp.asarray(dec, F32), bc(qd), bc(kd), gl


def _rope_tables(pos):
    half = RET_DK // 2
    inv = ROPE_BASE ** (-(np.arange(half, dtype=np.float64) / half))
    ang = pos.astype(np.float64)[:, None] * inv[None, :]
    cos, sin = np.cos(ang), np.sin(ang)
    cos2 = np.concatenate([cos, cos], axis=1)
    sin2 = np.concatenate([-sin, sin], axis=1)
    scale = RET_DK ** -0.5
    return (jnp.asarray(cos2, F32), jnp.asarray(sin2, F32),
            jnp.asarray(cos2 * scale, F32), jnp.asarray(sin2 * scale, F32))


def _rope(t, cos2, sin2):
    return t * cos2 + pltpu.roll(t, RET_DK // 2, 1) * sin2


def _head_layernorm(o, gain):
    oc = o - jnp.mean(o, axis=-1, keepdims=True)
    return oc * lax.rsqrt(jnp.mean(oc * oc, axis=-1, keepdims=True) + EPS) * gain


def _ret_kernel(rq_ref, rk_ref, rv_ref, rg_ref, cq_ref, sq_ref, ck_ref, sk_ref, dec_ref, qd_ref, kd_ref,
                gain_ref, y_ref, sout_ref, s_scr, *, gl):
    c = pl.program_id(1)

    @pl.when(c == 0)
    def _():
        s_scr[...] = jnp.zeros_like(s_scr)

    cq, sq, ck, sk = cq_ref[...], sq_ref[...], ck_ref[...], sk_ref[...]
    for h in range(RET_HEADS):
        ks = slice(h * RET_DK, (h + 1) * RET_DK)
        vs = slice(h * RET_DV, (h + 1) * RET_DV)
        q = _rope(rq_ref[:, ks].astype(F32), cq, sq)
        k = _rope(rk_ref[:, ks].astype(F32), ck, sk)
        v = rv_ref[:, vs].astype(BF16)
        att = (_dot_nt(q.astype(BF16), k.astype(BF16)) * dec_ref[h]).astype(BF16)
        s_old = s_scr[h]
        o = _dot(att, v) + _dot((q * qd_ref[h]).astype(BF16), s_old.astype(BF16))
        s_scr[h] = gl[h] * s_old + _dot_tn((k * kd_ref[h]).astype(BF16), v)
        y = _silu(rg_ref[:, vs].astype(F32)) * _head_layernorm(o, gain_ref[:, vs])
        y_ref[:, vs] = y.astype(y_ref.dtype)

    @pl.when(c == pl.num_programs(1) - 1)
    def _():
        sout_ref[0] = s_scr[...]


def _ret_prompt(proj, gain, batch, seq):
    r = CHUNK_ROWS
    nc = seq // r
    dec, qd, kd, gl = _ret_consts(r)
    cq, sq, ck, sk = _rope_tables(np.arange(seq))
    row = lambda b, c: b * nc + c
    tab = pl.BlockSpec((r, RET_DK), lambda b, c: (c, 0))
    const3 = lambda shape: pl.BlockSpec(shape, lambda b, c: (0, 0, 0))
    return pl.pallas_call(
        functools.partial(_ret_kernel, gl=gl),
        grid=(batch, nc),
        in_specs=[
            pl.BlockSpec((r, 512), lambda b, c: (row(b, c), 0)),
            pl.BlockSpec((r, 512), lambda b, c: (row(b, c), 1)),
            pl.BlockSpec((r, 1024), lambda b, c: (row(b, c), 1)),
            pl.BlockSpec((r, 1024), lambda b, c: (row(b, c), 2)),
            tab, tab, tab, tab,
            const3((RET_HEADS, r, r)), const3((RET_HEADS, r, LANES)), const3((RET_HEADS, r, LANES)),
            pl.BlockSpec((1, 1024), lambda b, c: (0, 0)),
        ],
        out_specs=[
            pl.BlockSpec((r, 1024), lambda b, c: (row(b, c), 0)),
            pl.BlockSpec((1, RET_HEADS, RET_DK, RET_DV), lambda b, c: (b, 0, 0, 0)),
        ],
        out_shape=[jax.ShapeDtypeStruct((batch * seq, 1024), BF16),
                   jax.ShapeDtypeStruct((batch, RET_HEADS, RET_DK, RET_DV), F32)],
        scratch_shapes=[pltpu.VMEM((RET_HEADS, RET_DK, RET_DV), F32)],
        compiler_params=_cparams(("parallel", "arbitrary")),
        name="ret_prompt",
    )(proj, proj, proj, proj, cq, sq, ck, sk, dec, qd, kd, gain)


HG_CHUNK = 128
HG_LEVELS = (64, 32, 16, 8)
HG_TILE_REF = SUBLANES // 2 - 1
LOG2E = 1.4426950408889634


def _hg_masks():
    i = np.arange(HG_CHUNK)
    out = [(i[:, None] // (2 * b)) == (i[None, :] // (2 * b)) for b in HG_LEVELS[1:]]
    out.append(((i[:, None] // SUBLANES) == (i[None, :] // SUBLANES)) & (i[:, None] >= i[None, :]))
    return jnp.asarray(np.stack(out).astype(np.float32))


def _tri_ones(r):
    i = np.arange(r)
    return jnp.asarray((i[:, None] >= i[None, :]).astype(np.float32), BF16)


def _halves(a, nb):
    a5 = a.reshape(HG_CHUNK // (2 * nb * SUBLANES), 2, nb, SUBLANES, a.shape[-1])
    return a5[:, 0], a5[:, 1]


def _join_halves(lower, upper):
    return jnp.stack([lower, upper], axis=1).reshape(HG_CHUNK, lower.shape[-1])


def _tile_row_bcast(a, j):
    a3 = a.reshape(HG_CHUNK // SUBLANES, SUBLANES, a.shape[-1])
    return jnp.broadcast_to(a3[:, j:j + 1, :], a3.shape).reshape(a.shape)


def _per_head_nt(a, b):
    return [_dot_nt(a[:, h * LANES:(h + 1) * LANES], b[:, h * LANES:(h + 1) * LANES]) for h in range(HG_HEADS)]


def _hg_chunk_att(q, kk, g2, mask_ref):
    tile_last = _tile_row_bcast(g2, SUBLANES - 1)
    att = None
    for lvl, b in enumerate(HG_LEVELS):
        nb = b // SUBLANES
        _, q_up = _halves(q, nb)
        k_lo, _ = _halves(kk, nb)
        g_lo, g_up = _halves(g2, nb)
        ref = _halves(tile_last, nb)[0][:, nb - 1:nb]
        qe = q_up * jnp.exp2(g_up - ref)
        ke = k_lo * jnp.exp2(ref - g_lo)
        zeros = jnp.zeros_like(qe)
        prods = _per_head_nt(_join_halves(zeros, qe).astype(BF16), _join_halves(ke, zeros).astype(BF16))
        att = prods if lvl == 0 else [a + p * mask_ref[lvl - 1] for a, p in zip(att, prods)]
    d = g2 - _tile_row_bcast(g2, HG_TILE_REF)
    prods = _per_head_nt((q * jnp.exp2(d)).astype(BF16), (kk * jnp.exp2(-d)).astype(BF16))
    same_tile = mask_ref[len(HG_LEVELS) - 1] > 0.5
    return [a + jnp.where(same_tile, p, 0.0) for a, p in zip(att, prods)]


def _hgrn_kernel(hq_ref, hf_ref, hi_ref, hg_ref, lb_ref, gain_ref, tri_ref, mask_ref, y_ref, sout_ref,
                 st_scr, *, r):
    c = pl.program_id(1)

    @pl.when(c == 0)
    def _():
        st_scr[...] = jnp.zeros_like(st_scr)

    lb = lb_ref[...]
    for ck in range(r // HG_CHUNK):
        rows = slice(ck * HG_CHUNK, (ck + 1) * HG_CHUNK)
        q = _silu(hq_ref[rows, :].astype(F32))
        f = jnp.clip(lb + (1.0 - lb) * _sigmoid(hf_ref[rows, :].astype(F32)), HG_CLIP_MIN, 1.0)
        kk = 1.0 - f
        g2 = _dot3_left(tri_ref[...], jnp.log(f)) * LOG2E
        g2_last = g2[HG_CHUNK - 1:HG_CHUNK, :]
        att = _hg_chunk_att(q, kk, g2, mask_ref)
        q_in = (q * jnp.exp2(g2)).astype(BF16)
        k_out = (kk * jnp.exp2(g2_last - g2)).astype(BF16)
        e_last = jnp.exp2(g2_last)
        gate = _sigmoid(hg_ref[rows, :].astype(F32))
        for h in range(HG_HEADS):
            sl = slice(h * LANES, (h + 1) * LANES)
            v = hi_ref[rows, sl]
            st_old = st_scr[h]
            o = _dot(att[h].astype(BF16), v) + _dot_nt(q_in[:, sl], st_old.astype(BF16))
            st_scr[h] = st_old * e_last[:, sl] + _dot_tn(v, k_out[:, sl])
            o = o * lax.rsqrt(jnp.mean(o * o, axis=-1, keepdims=True) + EPS) * gain_ref[:, sl]
            y_ref[rows, sl] = (o * gate[:, sl]).astype(y_ref.dtype)

    @pl.when(c == pl.num_programs(1) - 1)
    def _():
        for h in range(HG_HEADS):
            sout_ref[0, h] = st_scr[h].T


def _hgrn_prompt(proj, lb, gain, batch, seq):
    r = 2 * CHUNK_ROWS if seq % (2 * CHUNK_ROWS) == 0 else CHUNK_ROWS
    nc = seq // r
    masks = _hg_masks()
    row = lambda b, c: b * nc + c
    blk = lambda j: pl.BlockSpec((r, 1024), lambda b, c: (row(b, c), j))
    vec = pl.BlockSpec((1, 1024), lambda b, c: (0, 0))
    return pl.pallas_call(
        functools.partial(_hgrn_kernel, r=r),
        grid=(batch, nc),
        in_specs=[blk(3), blk(4), blk(5), blk(6), vec, vec,
                  pl.BlockSpec((HG_CHUNK, HG_CHUNK), lambda b, c: (0, 0)),
                  pl.BlockSpec(masks.shape, lambda b, c: (0, 0, 0))],
        out_specs=[
            pl.BlockSpec((r, 1024), lambda b, c: (row(b, c), 0)),
            pl.BlockSpec((1, HG_HEADS, HG_EXPAND, HG_DV), lambda b, c: (b, 0, 0, 0)),
        ],
        out_shape=[jax.ShapeDtypeStruct((batch * seq, 1024), BF16),
                   jax.ShapeDtypeStruct((batch, HG_HEADS, HG_EXPAND, HG_DV), F32)],
        scratch_shapes=[pltpu.VMEM((HG_HEADS, HG_DV, HG_EXPAND), F32)],
        compiler_params=_cparams(("parallel", "arbitrary")),
        name="hgrn_prompt",
    )(proj, proj, proj, proj, lb, gain, _tri_ones(HG_CHUNK), masks)


HEADS_PER_GROUP = SSM_HEADS // SSM_GROUPS
GROUP_W = HEADS_PER_GROUP * SSM_HEADDIM
CONV_PAD = SUBLANES
SSD_CHUNK = 128


def _head_expand(vals, h0, lane_head):
    out = jnp.zeros((vals.shape[0], GROUP_W), F32)
    for j in range(HEADS_PER_GROUP):
        out = jnp.where(lane_head == j, vals[:, h0 + j:h0 + j + 1], out)
    return out


def _ssd_kernel(z_ref, xbc_ref, xn_ref, wdt_ref, cw_ref, cb_ref, dtb_ref, alog_ref, dsk_ref, norm_ref, tri_ref,
                shift_ref, y_ref, sout_ref, cout_ref, hist_scr, s_scr, o_scr, *, r):
    c = pl.program_id(1)
    last = pl.num_programs(1) - 1

    @pl.when(c == 0)
    def _():
        hist_scr[0:CONV_PAD, :] = jnp.zeros((CONV_PAD, SSM_CONV_CH), F32)
        s_scr[...] = jnp.zeros_like(s_scr)

    u_bf = xbc_ref[...]
    u = u_bf.astype(F32)
    hist_scr[CONV_PAD:2 * CONV_PAD, :] = u[0:CONV_PAD]
    conv = cb_ref[...] + cw_ref[SSM_CONV - 1:SSM_CONV, :] * u
    head = cb_ref[...]
    for w in range(SSM_CONV):
        off = CONV_PAD - (SSM_CONV - 1) + w
        head = head + cw_ref[w:w + 1, :] * hist_scr[off:off + CONV_PAD, :]
    for back in range(1, SSM_CONV):
        conv = conv + cw_ref[SSM_CONV - 1 - back:SSM_CONV - back, :] * _dot(shift_ref[back - 1], u_bf)
    conv = jnp.concatenate([head, conv[CONV_PAD:]], axis=0)

    @pl.when(c == last)
    def _():
        cout_ref[0] = u[r - (SSM_CONV - 1):r, :]

    hist_scr[0:CONV_PAD, :] = u[r - CONV_PAD:r]

    xc = _silu(conv)
    xs = xc[:, :SSM_INNER]
    bm = xc[:, SSM_INNER:SSM_INNER + SSM_GROUPS * SSM_STATE]
    cm = xc[:, SSM_INNER + SSM_GROUPS * SSM_STATE:]

    dt_all = _softplus(_dot_nt(xn_ref[...], wdt_ref[...].astype(BF16)) + dtb_ref[...])
    rate = -jnp.exp(alog_ref[...]) * LOG2E
    rc = SSD_CHUNK
    ri = lax.broadcasted_iota(jnp.int32, (rc, rc), 0)
    ci = lax.broadcasted_iota(jnp.int32, (rc, rc), 1)
    causal = ri >= ci
    lane_head = lax.broadcasted_iota(jnp.int32, (rc, GROUP_W), 1) // SSM_HEADDIM
    lane_head1 = lax.broadcasted_iota(jnp.int32, (1, GROUP_W), 1) // SSM_HEADDIM
    groups = range(SSM_GROUPS)

    for ck in range(r // rc):
        rows = slice(ck * rc, (ck + 1) * rc)
        dt = dt_all[rows]
        g2 = _dot3_left(tri_ref[...], dt * rate)
        g2_last = g2[rc - 1:rc, :]
        g2_t = g2.T
        dt_t = dt.T
        e_cum = jnp.exp2(g2)
        w_in = dt * jnp.exp2(g2_last - g2)
        e_last = jnp.exp2(g2_last)

        cg = [cm[rows, g * SSM_STATE:(g + 1) * SSM_STATE].astype(BF16) for g in groups]
        bg = [bm[rows, g * SSM_STATE:(g + 1) * SSM_STATE].astype(BF16) for g in groups]
        xs_g = [xs[rows, g * GROUP_W:(g + 1) * GROUP_W] for g in groups]
        xs_b = [x.astype(BF16) for x in xs_g]
        s_old = [s_scr[g] for g in groups]
        cb = [jnp.where(causal, _dot_nt(cg[g], bg[g]), 0.0) for g in groups]
        carried = [_dot(cg[g], s_old[g].astype(BF16)) for g in groups]
        for g in groups:
            h0 = g * HEADS_PER_GROUP
            o_g = _head_expand(e_cum, h0, lane_head) * carried[g]
            for j in range(HEADS_PER_GROUP):
                h = h0 + j
                dec = jnp.exp2(jnp.minimum(g2[:, h:h + 1] - g2_t[h:h + 1, :], 0.0))
                att = (cb[g] * dec * dt_t[h:h + 1, :]).astype(BF16)
                o_g = o_g + jnp.where(lane_head == j, _dot(att, xs_b[g]), 0.0)
            s_scr[g] = (s_old[g] * _head_expand(e_last, h0, lane_head1)
                        + _dot_tn(bg[g], (xs_g[g] * _head_expand(w_in, h0, lane_head)).astype(BF16)))
            o_scr[rows, g * GROUP_W:(g + 1) * GROUP_W] = o_g + dsk_ref[:, g * GROUP_W:(g + 1) * GROUP_W] * xs_g[g]

    y_ref[...] = _rms(o_scr[...] * _silu(z_ref[...].astype(F32)), norm_ref[...]).astype(y_ref.dtype)

    @pl.when(c == last)
    def _():
        for h in range(SSM_HEADS):
            j = h % HEADS_PER_GROUP
            sout_ref[0, h] = s_scr[h // HEADS_PER_GROUP][:, j * SSM_HEADDIM:(j + 1) * SSM_HEADDIM]


def _shift_ones(r):
    i = np.arange(r)
    mats = [(i[:, None] - i[None, :]) == k for k in range(1, SSM_CONV)]
    return jnp.asarray(np.stack(mats).astype(np.float32), BF16)


def _pad_lanes(v):
    return jnp.pad(v.astype(F32), (0, LANES - v.shape[0])).reshape(1, LANES)


def _ssd_prompt(proj, xn, w_dt, conv_w, conv_b, dt_bias, a_log, dsk_exp, norm, batch, seq):
    r = CHUNK_ROWS
    nc = seq // r
    row = lambda b, c: b * nc + c
    vec = lambda n: pl.BlockSpec((1, n), lambda b, c: (0, 0))
    return pl.pallas_call(
        functools.partial(_ssd_kernel, r=r),
        grid=(batch, nc),
        in_specs=[
            pl.BlockSpec((r, 1024), lambda b, c: (row(b, c), 7)),
            pl.BlockSpec((r, SSM_CONV_CH), lambda b, c: (row(b, c), 4)),
            pl.BlockSpec((r, D_MODEL), lambda b, c: (row(b, c), 0)),
            pl.BlockSpec((DT_PAD, D_MODEL), lambda b, c: (0, 0)),
            pl.BlockSpec((SSM_CONV, SSM_CONV_CH), lambda b, c: (0, 0)),
            vec(SSM_CONV_CH), vec(LANES), vec(LANES), vec(SSM_INNER), vec(SSM_INNER),
            pl.BlockSpec((SSD_CHUNK, SSD_CHUNK), lambda b, c: (0, 0)),
            pl.BlockSpec((SSM_CONV - 1, r, r), lambda b, c: (0, 0, 0)),
        ],
        out_specs=[
            pl.BlockSpec((r, 1024), lambda b, c: (row(b, c), 0)),
            pl.BlockSpec((1, SSM_HEADS, SSM_STATE, SSM_HEADDIM), lambda b, c: (b, 0, 0, 0)),
            pl.BlockSpec((1, SSM_CONV - 1, SSM_CONV_CH), lambda b, c: (b, 0, 0)),
        ],
        out_shape=[jax.ShapeDtypeStruct((batch * seq, 1024), BF16),
                   jax.ShapeDtypeStruct((batch, SSM_HEADS, SSM_STATE, SSM_HEADDIM), F32),
                   jax.ShapeDtypeStruct((batch, SSM_CONV - 1, SSM_CONV_CH), F32)],
        scratch_shapes=[pltpu.VMEM((2 * CONV_PAD, SSM_CONV_CH), F32),
                        pltpu.VMEM((SSM_GROUPS, SSM_STATE, GROUP_W), F32),
                        pltpu.VMEM((r, SSM_INNER), F32)],
        compiler_params=_cparams(("parallel", "arbitrary")),
        name="ssd_prompt",
    )(proj, proj, xn, w_dt, conv_w, conv_b.reshape(1, -1), _pad_lanes(dt_bias), _pad_lanes(a_log), dsk_exp, norm,
      _tri_ones(SSD_CHUNK), _shift_ones(r))


def _merge_kernel(x_ref, yr_ref, yh_ref, ym_ref, g0_ref, g1_ref, g2_ref, wb_ref, wo_ref, npost_ref, nnext_ref,
                  o_ref, xn_ref):
    merged = _sigmoid(g0_ref[...].astype(F32)) * _dot(yr_ref[...].astype(BF16), wb_ref[0])
    merged = merged + _sigmoid(g1_ref[...].astype(F32)) * _dot(yh_ref[...].astype(BF16), wb_ref[1])
    merged = merged + _sigmoid(g2_ref[...].astype(F32)) * _dot(ym_ref[...].astype(BF16), wb_ref[2])
    m = _dot(merged.astype(BF16), wo_ref[...])
    out = x_ref[...] + _rms(m, npost_ref[...])
    o_ref[...] = out
    xn_ref[...] = _rms(out, nnext_ref[...]).astype(BF16)


def _merge(x, yr, yh, ym, proj, wb, wo, npost, nnext, tm):
    t = x.shape[0]
    rows = lambda j: pl.BlockSpec((tm, D_MODEL), lambda i: (i, j))
    vec = pl.BlockSpec((1, D_MODEL), lambda i: (0, 0))
    return pl.pallas_call(
        _merge_kernel,
        grid=(t // tm,),
        in_specs=[rows(0), rows(0), rows(0), rows(0), rows(GATE_TILE0), rows(GATE_TILE0 + 1), rows(GATE_TILE0 + 2),
                  pl.BlockSpec((3, D_MODEL, D_MODEL), lambda i: (0, 0, 0)),
                  pl.BlockSpec((D_MODEL, D_MODEL), lambda i: (0, 0)), vec, vec],
        out_specs=[rows(0), rows(0)],
        out_shape=[jax.ShapeDtypeStruct((t, D_MODEL), F32), jax.ShapeDtypeStruct((t, D_MODEL), BF16)],
        compiler_params=_cparams(("parallel",)),
        name="merge",
    )(x, yr, yh, ym, proj, proj, proj, wb, wo, npost, nnext)


def _ffn_kernel(x_ref, xn_ref, wu_ref, wd_ref, npost_ref, *rest):
    (nnext_ref, o_ref, xnn_ref, acc_scr) = rest if len(rest) == 4 else (None, rest[0], None, rest[1])
    j = pl.program_id(1)

    @pl.when(j == 0)
    def _():
        acc_scr[...] = jnp.zeros_like(acc_scr)

    hid = jnp.square(jnp.maximum(_dot(xn_ref[...], wu_ref[0].astype(BF16)), 0.0))
    acc_scr[...] += _dot(hid.astype(BF16), wd_ref[0].astype(BF16))

    @pl.when(j == pl.num_programs(1) - 1)
    def _():
        out = x_ref[...] + _rms(acc_scr[...], npost_ref[...])
        o_ref[...] = out
        if xnn_ref is not None:
            xnn_ref[...] = _rms(out, nnext_ref[...]).astype(BF16)


def _ffn(x, xn, npost, nnext, w_up, w_down, layer, tm):
    t = x.shape[0]
    tf = 1024
    vec = pl.BlockSpec((1, D_MODEL), lambda i, j: (0, 0))
    rows = pl.BlockSpec((tm, D_MODEL), lambda i, j: (i, 0))
    chain = nnext is not None
    wu, wd = w_up, w_down
    outs = pl.pallas_call(
        _ffn_kernel,
        grid=(t // tm, D_FF // tf),
        in_specs=[rows, rows,
                  pl.BlockSpec((1, D_MODEL, tf), lambda i, j: (layer, 0, j)),
                  pl.BlockSpec((1, tf, D_MODEL), lambda i, j: (layer, j, 0)), vec] + ([vec] if chain else []),
        out_specs=[rows, rows] if chain else [rows],
        out_shape=[jax.ShapeDtypeStruct((t, D_MODEL), F32)]
        + ([jax.ShapeDtypeStruct((t, D_MODEL), BF16)] if chain else []),
        scratch_shapes=[pltpu.VMEM((tm, D_MODEL), F32)],
        compiler_params=_cparams(("parallel", "arbitrary")),
        name="ffn",
    )(x, xn, wu, wd, npost, *([nnext] if chain else []))
    return (outs[0], outs[1]) if chain else (outs[0], None)


def _dec_prep_kernel(proj_ref, xn_ref, wdt_ref, cst_ref, cq_ref, sq_ref, ck_ref, sk_ref, lb_ref, cw_ref, cb_ref,
                     dtb_ref, alog_ref, hexp_ref,
                     cret_ref, chg_ref, cssd_ref, xs_ref, xdt_ref, drow_ref, cnew_ref):
    n = proj_ref.shape[0]
    cq, sq, ck, sk = cq_ref[...], sq_ref[...], ck_ref[...], sk_ref[...]
    for h in range(RET_HEADS):
        ks = slice(h * RET_DK, (h + 1) * RET_DK)
        cret_ref[:, ks] = _rope(proj_ref[:, ks], cq, sq)
        k_off = RET_HEADS * RET_DK
        cret_ref[:, k_off + h * RET_DK:k_off + (h + 1) * RET_DK] = _rope(
            proj_ref[:, k_off + h * RET_DK:k_off + (h + 1) * RET_DK], ck, sk)

    lb = lb_ref[...]
    chg_ref[:, 0:1024] = _silu(proj_ref[:, 3072:4096])
    chg_ref[:, 1024:2048] = jnp.clip(lb + (1.0 - lb) * _sigmoid(proj_ref[:, 4096:5120]), HG_CLIP_MIN, 1.0)

    u = proj_ref[:, 8192:8192 + SSM_CONV_CH]
    conv = cb_ref[...] + cw_ref[SSM_CONV - 1:SSM_CONV, :] * u
    for w in range(SSM_CONV - 1):
        conv = conv + cw_ref[w:w + 1, :] * cst_ref[w]
    cnew_ref[0] = cst_ref[1]
    cnew_ref[1] = cst_ref[2]
    cnew_ref[2] = u
    xc = _silu(conv)
    xs = xc[:, :SSM_INNER]
    xs_ref[...] = xs
    cssd_ref[:, 0:512] = xc[:, SSM_INNER + 512:]
    cssd_ref[:, 512:1024] = xc[:, SSM_INNER:SSM_INNER + 512]
    dt = _softplus(_dot_nt(xn_ref[...], wdt_ref[...].astype(BF16)) + dtb_ref[...])
    decay = jnp.exp(dt * (-jnp.exp(alog_ref[...])))
    hexp = hexp_ref[...]
    xdt_ref[...] = xs * _dot3_right(dt, hexp)
    drow_ref[...] = _dot3_right(decay, hexp)
    del n


def _dec_prep(proj, xn, w_dt, conv_state_t, lb, conv_w, conv_b, dt_bias, a_log):
    n = proj.shape[0]
    cq, sq, ck, sk = _rope_tables(np.full((n,), PAST_LEN))
    hexp = np.zeros((LANES, SSM_INNER), np.float32)
    for h in range(SSM_HEADS):
        hexp[h, h * SSM_HEADDIM:(h + 1) * SSM_HEADDIM] = 1.0
    outs = [(n, 1024), (n, 2048), (n, 1024), (n, 1024), (n, 1024), (n, 1024)]
    return pl.pallas_call(
        _dec_prep_kernel,
        out_shape=[jax.ShapeDtypeStruct(s, F32) for s in outs]
        + [jax.ShapeDtypeStruct((SSM_CONV - 1, n, SSM_CONV_CH), F32)],
        compiler_params=pltpu.CompilerParams(vmem_limit_bytes=VMEM_LIMIT),
        name="dec_prep",
    )(proj, xn, w_dt, conv_state_t, cq, sq, ck, sk, lb, conv_w, conv_b.reshape(1, -1), _pad_lanes(dt_bias),
      _pad_lanes(a_log), jnp.asarray(hexp, BF16))


def _cols(a, tb):
    n, ch = a.shape
    return a.reshape(n // tb, tb, ch).transpose(0, 2, 1)


def _state_update_call(body, name, layer, state, carried, n_tokens, heads_per_step, in_specs, operands,
                       scratch):
    depth, _, heads, sk, sv = state.shape
    tb = DEC_TOKENS
    in_spec = pl.BlockSpec((1, tb, heads_per_step, sk, sv), lambda t, j: (layer, t, j, 0, 0))
    if carried is None:
        carried, aliases = state, {}
        out_spec = pl.BlockSpec((depth, tb, heads_per_step, sk, sv), lambda t, j: (0, t, j, 0, 0))
        own, fill = layer, tuple(d for d in range(depth) if d != layer)
    else:
        aliases = {1: 1}
        out_spec, own, fill = in_spec, 0, ()
    return pl.pallas_call(
        functools.partial(body, own=own, fill=fill),
        grid=(n_tokens // tb, heads // heads_per_step),
        in_specs=[in_spec, pl.BlockSpec(memory_space=pl.ANY)] + in_specs,
        out_specs=[pl.BlockSpec((tb, 1024), lambda t, j: (t, 0)), out_spec],
        out_shape=[jax.ShapeDtypeStruct((n_tokens, 1024), F32), jax.ShapeDtypeStruct(state.shape, F32)],
        input_output_aliases=aliases,
        scratch_shapes=[scratch],
        compiler_params=_cparams(("parallel", "arbitrary")),
        name=name,
    )(state, carried, *operands)


def _zero_fill(so_ref, fill):
    for d in fill:
        so_ref[d] = jnp.zeros(so_ref.shape[1:], so_ref.dtype)


def _ret_dec_kernel(s_ref, carried_ref, ct_ref, v_ref, rg_ref, gain_ref, y_ref, so_ref, o_scr, *, gamma, own,
                    fill):
    del carried_ref
    for i in range(DEC_TOKENS):
        for h in range(RET_HEADS):
            qcol = ct_ref[0, h * RET_DK:(h + 1) * RET_DK, i:i + 1]
            kcol = ct_ref[0, (RET_HEADS + h) * RET_DK:(RET_HEADS + h + 1) * RET_DK, i:i + 1]
            vs = slice(h * RET_DV, (h + 1) * RET_DV)
            s_new = gamma[h] * s_ref[0, i, h] + kcol * v_ref[i:i + 1, vs]
            so_ref[own, i, h] = s_new
            o_scr[i:i + 1, vs] = jnp.sum(qcol * s_new, axis=0, keepdims=True)
    _zero_fill(so_ref, fill)
    for h in range(RET_HEADS):
        vs = slice(h * RET_DV, (h + 1) * RET_DV)
        y_ref[:, vs] = _silu(rg_ref[:, vs]) * _head_layernorm(o_scr[:, vs], gain_ref[:, vs])


def _ret_dec(layer, state, carried, c_ret, proj, gain):
    tb = DEC_TOKENS
    gamma = tuple(float(v) for v in np.exp(_ret_log_gamma()))
    return _state_update_call(
        functools.partial(_ret_dec_kernel, gamma=gamma), "ret_dec", layer, state, carried, proj.shape[0],
        RET_HEADS,
        [pl.BlockSpec((1, 2 * RET_HEADS * RET_DK, tb), lambda t, j: (t, 0, 0)),
         pl.BlockSpec((tb, 1024), lambda t, j: (t, 1)), pl.BlockSpec((tb, 1024), lambda t, j: (t, 2)),
         pl.BlockSpec((1, 1024), lambda t, j: (0, 0))],
        (_cols(c_ret, tb), proj, proj, gain), pltpu.VMEM((tb, 1024), F32))


def _hgrn_dec_kernel(s_ref, carried_ref, ct_ref, v_ref, hg_ref, gain_ref, y_ref, so_ref, o_scr, *, own, fill):
    del carried_ref
    for i in range(DEC_TOKENS):
        for h in range(HG_HEADS):
            sl = slice(h * LANES, (h + 1) * LANES)
            qcol = ct_ref[0, h * HG_EXPAND:(h + 1) * HG_EXPAND, i:i + 1]
            fcol = ct_ref[0, (HG_HEADS + h) * HG_EXPAND:(HG_HEADS + h + 1) * HG_EXPAND, i:i + 1]
            s_new = fcol * s_ref[0, i, h] + (1.0 - fcol) * v_ref[i:i + 1, sl]
            so_ref[own, i, h] = s_new
            o_scr[i:i + 1, sl] = jnp.sum(qcol * s_new, axis=0, keepdims=True)
    _zero_fill(so_ref, fill)
    for h in range(HG_HEADS):
        sl = slice(h * LANES, (h + 1) * LANES)
        o = o_scr[:, sl]
        o = o * lax.rsqrt(jnp.mean(o * o, axis=-1, keepdims=True) + EPS) * gain_ref[:, sl]
        y_ref[:, sl] = o * _sigmoid(hg_ref[:, sl])


def _hgrn_dec(layer, state, carried, c_hg, proj, gain):
    tb = DEC_TOKENS
    return _state_update_call(
        _hgrn_dec_kernel, "hgrn_dec", layer, state, carried, proj.shape[0], HG_HEADS,
        [pl.BlockSpec((1, 2 * HG_HEADS * HG_EXPAND, tb), lambda t, j: (t, 0, 0)),
         pl.BlockSpec((tb, 1024), lambda t, j: (t, 5)), pl.BlockSpec((tb, 1024), lambda t, j: (t, 6)),
         pl.BlockSpec((1, 1024), lambda t, j: (0, 0))],
        (_cols(c_hg, tb), proj, proj, gain), pltpu.VMEM((tb, 1024), F32))


def _ssd_dec_kernel(s_ref, carried_ref, c_ref, b_ref, xdt_ref, drow_ref, xs_ref, z_ref, dsk_ref, norm_ref,
                    y_ref, so_ref, o_scr, *, own, fill):
    del carried_ref
    for g in range(SSM_GROUPS):
        c_rows = c_ref[:, g * SSM_STATE:(g + 1) * SSM_STATE].astype(BF16)
        for i in range(DEC_TOKENS):
            b_row = b_ref[i:i + 1, g * SSM_STATE:(g + 1) * SSM_STATE]
            new = []
            for h in range(g * HEADS_PER_GROUP, (g + 1) * HEADS_PER_GROUP):
                hs = slice(h * SSM_HEADDIM, (h + 1) * SSM_HEADDIM)
                decay = drow_ref[i:i + 1, h * SSM_HEADDIM:h * SSM_HEADDIM + 1]
                s_new = decay * s_ref[0, i, h] + xdt_ref[0, hs, i:i + 1] * b_row
                so_ref[own, i, h] = s_new
                new.append(s_new)
            s_grp = jnp.concatenate(new, axis=0).astype(BF16)
            o_scr[i:i + 1, g * GROUP_W:(g + 1) * GROUP_W] = _dot_nt(c_rows, s_grp)[i:i + 1, :]
    _zero_fill(so_ref, fill)
    o = o_scr[...] + dsk_ref[...] * xs_ref[...]
    y_ref[...] = _rms(o * _silu(z_ref[...]), norm_ref[...])


def _ssd_dec(layer, state, carried, c_ssd, xs, xdt, drow, proj, dsk_exp, norm):
    tb = DEC_TOKENS
    rows = lambda col: pl.BlockSpec((tb, 1024), lambda t, j: (t, col))
    half = lambda col: pl.BlockSpec((tb, SSM_GROUPS * SSM_STATE), lambda t, j: (t, col))
    vec = pl.BlockSpec((1, 1024), lambda t, j: (0, 0))
    return _state_update_call(
        _ssd_dec_kernel, "ssd_dec", layer, state, carried, proj.shape[0], SSM_HEADS,
        [half(0), half(1), pl.BlockSpec((1, SSM_INNER, tb), lambda t, j: (t, 0, 0)),
         rows(0), rows(0), rows(7), vec, vec],
        (c_ssd, c_ssd, _cols(xdt, tb), drow, xs, proj, dsk_exp, norm), pltpu.VMEM((tb, 1024), F32))


def kernel(x_prompt, x_sample, state_ret, state_hgrn, state_ssm, state_conv, w_in, ret_norm, hg_norm,
           hg_lb_logits, conv_w, conv_b, dt_bias, a_log, d_skip, ssm_norm, w_branch, w_out, norm_mix_pre,
           norm_mix_post, norm_ffn_pre, norm_ffn_post, w_up, w_down):
    bp, lp, _ = x_prompt.shape
    bs, ls, _ = x_sample.shape
    assert ls == 1 and lp % CHUNK_ROWS == 0 and bs % DEC_TOKENS == 0
    depth = w_in.shape[0]

    lb_w = jax.nn.softmax(hg_lb_logits.astype(F32), axis=0)
    lb_all = jnp.cumsum(lb_w, axis=0) - lb_w[0]

    hp = x_prompt.reshape(bp * lp, D_MODEL)
    hs = x_sample.reshape(bs, D_MODEL)
    outs = [[] for _ in range(5)]
    new_ret = new_hg = new_ssm = None
    ssm_t = jnp.swapaxes(state_ssm, -1, -2)
    w_in_t = jnp.swapaxes(w_in, 1, 2)
    vec = lambda a: a.reshape(1, -1).astype(F32)
    hp_normed = _rmsnorm_bf16(hp, vec(norm_mix_pre[0]), tm=min(2048, bp * lp))
    hs_normed = _rmsnorm_bf16(hs, vec(norm_mix_pre[0]), tm=bs)
    for l in range(depth):
        next_gain = vec(norm_mix_pre[l + 1]) if l + 1 < depth else None
        w_dt = jnp.pad(w_in_t[l, N_MAIN:N_MAIN + SSM_HEADS], ((0, DT_PAD - SSM_HEADS), (0, 0)))
        wb, wo = w_branch[l].astype(BF16), w_out[l].astype(BF16)
        dsk_exp = jnp.repeat(d_skip[l].astype(F32), SSM_HEADDIM).reshape(1, SSM_INNER)
        lb = vec(lb_all[l])

        xn = hp_normed
        proj = _in_proj(xn, w_in_t, l, tm=min(4096, bp * lp), out_dtype=BF16)
        yr, s_ret = _ret_prompt(proj, vec(ret_norm[l]), bp, lp)
        yh, s_hg = _hgrn_prompt(proj, lb, vec(hg_norm[l]), bp, lp)
        ym, s_ssm, s_conv = _ssd_prompt(proj, xn, w_dt, conv_w[l], conv_b[l], dt_bias[l], a_log[l],
                                        dsk_exp, vec(ssm_norm[l]), bp, lp)
        hp, xn = _merge(hp, yr, yh, ym, proj, wb, wo, vec(norm_mix_post[l]), vec(norm_ffn_pre[l]), tm=512)
        hp, hp_normed = _ffn(hp, xn, vec(norm_ffn_post[l]), next_gain, w_up, w_down, l, tm=1024)
        for lst, val in zip(outs[:4], (s_ret, s_hg, s_ssm, s_conv)):
            lst.append(val)

        xn = hs_normed
        proj = _in_proj(xn, w_in_t, l, tm=bs, out_dtype=F32)
        c_ret, c_hg, c_ssd, xs, xdt, drow, conv_new = _dec_prep(
            proj, xn, w_dt, state_conv[l].transpose(1, 0, 2), lb, conv_w[l], conv_b[l], dt_bias[l], a_log[l])
        yr, new_ret = _ret_dec(l, state_ret, new_ret, c_ret, proj, vec(ret_norm[l]))
        yh, new_hg = _hgrn_dec(l, state_hgrn, new_hg, c_hg, proj, vec(hg_norm[l]))
        ym, new_ssm = _ssd_dec(l, ssm_t, new_ssm, c_ssd, xs, xdt, drow, proj, dsk_exp, vec(ssm_norm[l]))
        hs, xn = _merge(hs, yr, yh, ym, proj, wb, wo, vec(norm_mix_post[l]), vec(norm_ffn_pre[l]), tm=bs)
        hs, hs_normed = _ffn(hs, xn, vec(norm_ffn_post[l]), next_gain, w_up, w_down, l, tm=bs)
        outs[4].append(conv_new.transpose(1, 0, 2))

    p_ret, p_hg, p_ssm, p_conv, s_conv = [jnp.stack(o) for o in outs]
    return (hp.reshape(bp, lp, D_MODEL), hs.reshape(bs, ls, D_MODEL), p_ret, p_hg, p_ssm, p_conv,
            new_ret, new_hg, jnp.swapaxes(new_ssm, -1, -2), s_conv)
```
